```python
import functools
import jax, jax.numpy as jnp
from jax import lax
import numpy as np

D_MODEL = 1024
BATCH = 16
SEQ = 256
DEPTH = 2
DEC_BATCH = 2
DEC_SEQ = 2048
PAST_LEN = 256

GRID_W = 64
N_HEADS = 16
N_KV_HEADS = 4
GROUP = N_HEADS // N_KV_HEADS
HEAD_DIM = 64
Q_DIM = N_HEADS * HEAD_DIM
KV_DIM = N_KV_HEADS * HEAD_DIM
WINDOW = 128
BLOCK = 128
ATTN_SCALE = HEAD_DIM ** -0.5
ROPE_THETA = 10000.0
ROPE_HALF = HEAD_DIM // 2
ROPE_FREQS = ROPE_HALF // 2
C_CONV = D_MODEL // 2
CONV_WIDTH = 31
CONV_PAD = (CONV_WIDTH - 1) // 2
N_EXPERTS = 32
TOP_K = 4
D_FF = D_MODEL
SWIGLU_LIMIT = 7.0
SWIGLU_ALPHA = 1.702
EPS = 1e-6
NEG_INF = -1e30
IN_COLS = Q_DIM + 2 * KV_DIM + 2 * C_CONV + 2 * D_MODEL
SPLITS = (Q_DIM, Q_DIM + KV_DIM, Q_DIM + 2 * KV_DIM, Q_DIM + 2 * KV_DIM + 2 * C_CONV)

kernel_name = "hybrid_dit_prefix_swa_conformer_moe_step"


def _rmsnorm(x, g):
    xf = x.astype(jnp.float32)
    y = xf * lax.rsqrt(jnp.mean(xf * xf, axis=-1, keepdims=True) + EPS)
    return (y * g.astype(jnp.float32)).astype(x.dtype)


def _axial_rope(x):
    L = x.shape[1]
    rows = L // GRID_W
    row = jnp.repeat(jnp.arange(rows, dtype=jnp.float32), GRID_W)
    col = (jnp.arange(L) % GRID_W).astype(jnp.float32)
    inv = ROPE_THETA ** (-jnp.arange(ROPE_FREQS, dtype=jnp.float32) / ROPE_FREQS)

    def rot(xh, pos):
        ang = pos[:, None] * inv[None, :]
        cos = jnp.cos(ang)[None, :, None, :]
        sin = jnp.sin(ang)[None, :, None, :]
        x1, x2 = xh[..., :ROPE_FREQS], xh[..., ROPE_FREQS:]
        return jnp.concatenate([x1 * cos - x2 * sin, x2 * cos + x1 * sin], axis=-1)

    xf = x.astype(jnp.float32)
    out = jnp.concatenate([rot(xf[..., :ROPE_HALF], row), rot(xf[..., ROPE_HALF:], col)], axis=-1)
    return out.astype(x.dtype)


def _sink_softmax(s, sink):
    col = jnp.broadcast_to(sink.astype(jnp.float32).reshape(N_KV_HEADS, GROUP)[None, :, :, None, None],
                           s.shape[:-1] + (1,))
    p = jax.nn.softmax(jnp.concatenate([s, col], axis=-1), axis=-1)
    return p[..., :-1]


def _ctx_attention(q, k, v, sink):
    B, Lc = q.shape[:2]
    nq = Lc // BLOCK
    qb = q.reshape(B, nq, BLOCK, N_KV_HEADS, GROUP, HEAD_DIM).transpose(1, 0, 2, 3, 4, 5)

    def blk(qi):
        s = jnp.einsum('bqhgd,bkhd->bhgqk', qi, k).astype(jnp.float32) * ATTN_SCALE
        p = _sink_softmax(s, sink)
        return jnp.einsum('bhgqk,bkhd->bqhgd', p.astype(v.dtype), v)

    o = lax.map(blk, qb)
    return o.transpose(1, 0, 2, 3, 4, 5).reshape(B, Lc, Q_DIM)


def _latent_attention(q, k, v, kc, vc, sink):
    B, L = q.shape[:2]
    nb = L // BLOCK
    qb = q.reshape(B, nb, BLOCK, N_KV_HEADS, GROUP, HEAD_DIM).transpose(1, 0, 2, 3, 4, 5)
    kp = jnp.pad(k, ((0, 0), (BLOCK, BLOCK), (0, 0), (0, 0)))
    vp = jnp.pad(v, ((0, 0), (BLOCK, BLOCK), (0, 0), (0, 0)))
    i = jnp.arange(BLOCK)[:, None]
    j = jnp.arange(3 * BLOCK)[None, :]
    rel = j - BLOCK - i

    def blk(args):
        qi, b = args
        start = b * BLOCK
        kb = lax.dynamic_slice_in_dim(kp, start, 3 * BLOCK, axis=1)
        vb = lax.dynamic_slice_in_dim(vp, start, 3 * BLOCK, axis=1)
        kpos = start - BLOCK + j
        mask = (jnp.abs(rel) <= WINDOW) & (kpos >= 0) & (kpos < L)
        s_loc = jnp.einsum('bqhgd,bkhd->bhgqk', qi, kb).astype(jnp.float32) * ATTN_SCALE
        s_loc = jnp.where(mask, s_loc, NEG_INF)
        s_ctx = jnp.einsum('bqhgd,bkhd->bhgqk', qi, kc).astype(jnp.float32) * ATTN_SCALE
        p = _sink_softmax(jnp.concatenate([s_loc, s_ctx], axis=-1), sink).astype(v.dtype)
        return (jnp.einsum('bhgqk,bkhd->bqhgd', p[..., :3 * BLOCK], vb)
                + jnp.einsum('bhgqk,bkhd->bqhgd', p[..., 3 * BLOCK:], vc))

    o = lax.map(blk, (qb, jnp.arange(nb)))
    return o.transpose(1, 0, 2, 3, 4, 5).reshape(B, L, Q_DIM)


def _conv_branch(u, conv_w, conv_b, ln_g, ln_b, w_conv_o):
    ua, ub = jnp.split(u, 2, axis=-1)
    u = ua * jax.nn.sigmoid(ub)
    y = lax.conv_general_dilated(u, conv_w[:, None, :], window_strides=(1,),
                                 padding=[(CONV_PAD, CONV_PAD)],
                                 dimension_numbers=('NWC', 'WIO', 'NWC'),
                                 feature_group_count=C_CONV) + conv_b
    yf = y.astype(jnp.float32)
    mu = jnp.mean(yf, axis=-1, keepdims=True)
    var = jnp.mean(jnp.square(yf - mu), axis=-1, keepdims=True)
    y = ((yf - mu) * lax.rsqrt(var + EPS) * ln_g.astype(jnp.float32) + ln_b.astype(jnp.float32)).astype(u.dtype)
    return jax.nn.silu(y) @ w_conv_o


def _moe(h, w_router, b_router, w_gate_up, b_gate_up, w_down, b_down):
    shape = h.shape
    t = h.reshape(-1, shape[-1])
    logits = (t @ w_router + b_router).astype(jnp.float32)
    top_val, top_idx = lax.top_k(logits, TOP_K)
    top_w = jax.nn.softmax(top_val, axis=-1)
    combine = jnp.einsum('tk,tke->et', top_w, jax.nn.one_hot(top_idx, N_EXPERTS, dtype=jnp.float32))

    def expert(acc, xs):
        wgu, bgu, wdn, bdn, cw = xs
        gate, lin = jnp.split(t @ wgu + bgu, 2, axis=-1)
        gate = jnp.minimum(gate, SWIGLU_LIMIT)
        lin = jnp.clip(lin, -SWIGLU_LIMIT, SWIGLU_LIMIT)
        act = gate * jax.nn.sigmoid(SWIGLU_ALPHA * gate) * (lin + 1)
        y = act @ wdn + bdn
        return acc + cw[:, None].astype(y.dtype) * y, None

    out, _ = lax.scan(expert, jnp.zeros_like(t), (w_gate_up, b_gate_up, w_down, b_down, combine))
    return out.reshape(shape)


def _layer(x, cond, attend, use_rope, w_ada, b_ada, g_pre_mix, g_post_mix, g_pre_ffn, g_post_ffn,
           w_in, w_attn_o, conv_w, conv_b, conv_ln_g, conv_ln_b, w_conv_o, w_out,
           w_router, b_router, w_gate_up, b_gate_up, w_down, b_down):
    B, L, _ = x.shape
    mod = (jax.nn.silu(cond) @ w_ada + b_ada)[:, None, :]
    sh1, sc1, gt1, sh2, sc2, gt2 = jnp.split(mod, 6, axis=-1)
    h = _rmsnorm(x, g_pre_mix) * (1 + sc1) + sh1
    q, k, v, u, g = jnp.split(h @ w_in, SPLITS, axis=-1)
    q = q.reshape(B, L, N_HEADS, HEAD_DIM)
    k = k.reshape(B, L, N_KV_HEADS, HEAD_DIM)
    v = v.reshape(B, L, N_KV_HEADS, HEAD_DIM)
    if use_rope:
        q = _axial_rope(q)
        k = _axial_rope(k)
    a = attend(q, k, v) @ w_attn_o
    cv = _conv_branch(u, conv_w, conv_b, conv_ln_g, conv_ln_b, w_conv_o)
    ga, gc = jnp.split(g, 2, axis=-1)
    mix = (jax.nn.sigmoid(ga) * a + jax.nn.sigmoid(gc) * cv) @ w_out
    x = x + gt1 * _rmsnorm(mix, g_post_mix)
    h = _rmsnorm(x, g_pre_ffn) * (1 + sc2) + sh2
    x = x + gt2 * _rmsnorm(_moe(h, w_router, b_router, w_gate_up, b_gate_up, w_down, b_down), g_post_ffn)
    return x, k, v


def setup_inputs(seed: int = 0) -> dict:
    key = jax.random.key(seed)
    ks = jax.random.split(key, 28)

    def nrm(k, shape, scale):
        return jax.random.normal(k, shape, jnp.float32) * scale

    D = D_MODEL
    return {
        "x_prompt": nrm(ks[0], (BATCH, SEQ, D), 1.0),
        "x_sample": nrm(ks[1], (DEC_BATCH, DEC_SEQ, D), 1.0),
        "cache_k": nrm(ks[2], (DEC_BATCH, DEPTH, PAST_LEN, N_KV_HEADS, HEAD_DIM), 1.0),
        "cache_v": nrm(ks[3], (DEC_BATCH, DEPTH, PAST_LEN, N_KV_HEADS, HEAD_DIM), 1.0),
        "c": nrm(ks[4], (DEC_BATCH, D), 1.0),
        "c_ctx": nrm(ks[5], (D,), 1.0),
        "w_ada": nrm(ks[6], (DEPTH, D, 6 * D), 0.5 * D ** -0.5),
        "b_ada": nrm(ks[7], (DEPTH, 6 * D), 0.02),
        "g_pre_mix": 1.0 + nrm(ks[8], (DEPTH, D), 0.02),
        "g_post_mix": 1.0 + nrm(ks[9], (DEPTH, D), 0.02),
        "g_pre_ffn": 1.0 + nrm(ks[10], (DEPTH, D), 0.02),
        "g_post_ffn": 1.0 + nrm(ks[11], (DEPTH, D), 0.02),
        "w_in": nrm(ks[12], (DEPTH, D, IN_COLS), D ** -0.5),
        "attn_sink": nrm(ks[13], (DEPTH, N_HEADS), 0.5),
        "w_attn_o": nrm(ks[14], (DEPTH, Q_DIM, D), Q_DIM ** -0.5),
        "conv_w": nrm(ks[15], (DEPTH, CONV_WIDTH, C_CONV), CONV_WIDTH ** -0.5),
        "conv_b": nrm(ks[16], (DEPTH, C_CONV), 0.02),
        "conv_ln_g": 1.0 + nrm(ks[17], (DEPTH, C_CONV), 0.02),
        "conv_ln_b": nrm(ks[18], (DEPTH, C_CONV), 0.02),
        "w_conv_o": nrm(ks[19], (DEPTH, C_CONV, D), C_CONV ** -0.5),
        "w_out": nrm(ks[20], (DEPTH, D, D), D ** -0.5),
        "w_router": nrm(ks[21], (DEPTH, D, N_EXPERTS), D ** -0.5),
        "b_router": nrm(ks[22], (DEPTH, N_EXPERTS), 0.01),
        "w_gate_up": nrm(ks[23], (DEPTH, N_EXPERTS, D, 2 * D_FF), D ** -0.5),
        "b_gate_up": nrm(ks[24], (DEPTH, N_EXPERTS, 2 * D_FF), 0.02),
        "w_down": nrm(ks[25], (DEPTH, N_EXPERTS, D_FF, D), D_FF ** -0.5),
        "b_down": nrm(ks[26], (DEPTH, N_EXPERTS, D), 0.02),
    }


def reference(x_prompt, x_sample, cache_k, cache_v, c, c_ctx, w_ada, b_ada, g_pre_mix, g_post_mix,
              g_pre_ffn, g_post_ffn, w_in, attn_sink, w_attn_o, conv_w, conv_b, conv_ln_g, conv_ln_b,
              w_conv_o, w_out, w_router, b_router, w_gate_up, b_gate_up, w_down, b_down):
    xp = x_prompt
    xs = x_sample
    new_k_list = []
    new_v_list = []
    for l in range(DEPTH):
        lp = (w_ada[l], b_ada[l], g_pre_mix[l], g_post_mix[l], g_pre_ffn[l], g_post_ffn[l],
              w_in[l], w_attn_o[l], conv_w[l], conv_b[l], conv_ln_g[l], conv_ln_b[l], w_conv_o[l], w_out[l],
              w_router[l], b_router[l], w_gate_up[l], b_gate_up[l], w_down[l], b_down[l])
        ctx_attend = functools.partial(_ctx_attention, sink=attn_sink[l])
        xp, k_ctx, v_ctx = _layer(xp, c_ctx[None, :], ctx_attend, False, *lp)
        new_k_list.append(k_ctx)
        new_v_list.append(v_ctx)
        lat_attend = functools.partial(_latent_attention, kc=cache_k[:, l], vc=cache_v[:, l], sink=attn_sink[l])
        xs, _, _ = _layer(xs, c, lat_attend, True, *lp)
    new_k = jnp.stack(new_k_list, axis=1)
    new_v = jnp.stack(new_v_list, axis=1)
    return (xp, xs, new_k, new_v)
```

```python
import functools

import jax
import jax.numpy as jnp
from jax import lax
from jax.experimental import pallas as pl
from jax.experimental.pallas import tpu as pltpu

D_MODEL = 1024
BATCH = 16
SEQ = 256
DEPTH = 2
DEC_BATCH = 2
DEC_SEQ = 2048
PAST_LEN = 256
GRID_W = 64
N_HEADS = 16
N_KV_HEADS = 4
GROUP = N_HEADS // N_KV_HEADS
HEAD_DIM = 64
Q_DIM = N_HEADS * HEAD_DIM
KV_DIM = N_KV_HEADS * HEAD_DIM
WINDOW = 128
ATTN_SCALE = HEAD_DIM ** -0.5
ROPE_THETA = 10000.0
ROPE_HALF = HEAD_DIM // 2
ROPE_FREQS = ROPE_HALF // 2
C_CONV = D_MODEL // 2
CONV_WIDTH = 31
CONV_PAD = (CONV_WIDTH - 1) // 2
N_EXPERTS = 32
TOP_K = 4
D_FF = D_MODEL
SWIGLU_LIMIT = 7.0
SWIGLU_ALPHA = 1.702
EPS = 1e-6
IN_COLS = Q_DIM + 2 * KV_DIM + 2 * C_CONV + 2 * D_MODEL

N_CTX = BATCH * SEQ
N_LAT = DEC_BATCH * DEC_SEQ
N_TOK = N_CTX + N_LAT
N_COND = 1 + DEC_BATCH
COND_ROWS = 8

LANE = 128
TM = 512
TQ = 256
KWIN = TQ + 2 * WINDOW
TE = 256
N_PAIRS = N_TOK * TOP_K
N_ETILES = N_PAIRS // TE + N_EXPERTS
HALO = 16
VMEM_LIMIT = 56 * 1024 * 1024

_F32 = jnp.float32
_BF16 = jnp.bfloat16


def _rms(x, g):
    return x * lax.rsqrt(jnp.mean(x * x, axis=-1, keepdims=True) + EPS) * g


def _cond_index(i):
    n_ctx_tiles = N_CTX // TM
    return jnp.where(i < n_ctx_tiles, 0, 1 + (i - n_ctx_tiles) // (DEC_SEQ // TM))


def _mod_kernel(cond_ref, w_ref, b_ref, out_ref):
    cnd = cond_ref[...]
    s = cnd * jax.nn.sigmoid(cnd)
    out_ref[...] = jnp.dot(s, w_ref[...], precision=lax.Precision.HIGHEST,
                           preferred_element_type=_F32) + b_ref[...]


def _modulation(cond, w_ada, b_ada):
    tn = 1536
    nt = 6 * D_MODEL // tn
    return pl.pallas_call(
        _mod_kernel,
        grid=(DEPTH, nt),
        in_specs=[
            pl.BlockSpec((COND_ROWS, D_MODEL), lambda l, n: (0, 0)),
            pl.BlockSpec((None, D_MODEL, tn), lambda l, n: (l, 0, n)),
            pl.BlockSpec((None, 1, tn), lambda l, n: (l, 0, n)),
        ],
        out_specs=pl.BlockSpec((None, COND_ROWS, tn), lambda l, n: (l, 0, n)),
        out_shape=jax.ShapeDtypeStruct((DEPTH, COND_ROWS, 6 * D_MODEL), _F32),
        compiler_params=pltpu.CompilerParams(
            dimension_semantics=("arbitrary", "arbitrary"), vmem_limit_bytes=VMEM_LIMIT),
        name="modulation",
    )(cond, w_ada, b_ada.reshape(DEPTH, 1, 6 * D_MODEL))


def _rope_chunk(x, cos, sin):
    lane = lax.broadcasted_iota(jnp.int32, x.shape, 1)
    partner = jnp.where((lane & ROPE_FREQS) == 0,
                        pltpu.roll(x, LANE - ROPE_FREQS, 1), pltpu.roll(x, ROPE_FREQS, 1))
    return x * cos + partner * sin


def _inproj_kernel(x_ref, mod_ref, g_ref, w_ref, cos_ref, sin_ref,
                   q_ref, kv_ref, u_ref, sg_ref):
    i = pl.program_id(0)
    x = x_ref[...]
    sh = mod_ref[:, 0:D_MODEL]
    sc = mod_ref[:, D_MODEL:2 * D_MODEL]
    h = (_rms(x, g_ref[...]) * (1.0 + sc) + sh).astype(_BF16)

    c0 = 0
    q = jnp.dot(h, w_ref[:, c0:c0 + Q_DIM], preferred_element_type=_F32)
    c0 += Q_DIM
    kv = jnp.dot(h, w_ref[:, c0:c0 + 2 * KV_DIM], preferred_element_type=_F32)
    c0 += 2 * KV_DIM
    ua = jnp.dot(h, w_ref[:, c0:c0 + C_CONV], preferred_element_type=_F32)
    c0 += C_CONV
    ub = jnp.dot(h, w_ref[:, c0:c0 + C_CONV], preferred_element_type=_F32)
    c0 += C_CONV
    g = jnp.dot(h, w_ref[:, c0:c0 + 2 * D_MODEL], preferred_element_type=_F32)

    u_ref[...] = ua * jax.nn.sigmoid(ub)
    sg_ref[...] = jax.nn.sigmoid(g).astype(_BF16)

    is_latent = i >= N_CTX // TM

    @pl.when(jnp.logical_not(is_latent))
    def _():
        q_ref[...] = q.astype(_BF16)
        kv_ref[...] = kv

    @pl.when(is_latent)
    def _():
        cos = cos_ref[...]
        sin = sin_ref[...]
        for j in range(Q_DIM // LANE):
            sl = slice(j * LANE, (j + 1) * LANE)
            q_ref[:, sl] = _rope_chunk(q[:, sl], cos, sin).astype(_BF16)
        for j in range(KV_DIM // LANE):
            sl = slice(j * LANE, (j + 1) * LANE)
            kv_ref[:, sl] = _rope_chunk(kv[:, sl], cos, sin)
        kv_ref[:, KV_DIM:] = kv[:, KV_DIM:]


def _inproj(x, mod3, g_pre, w_in_bf, cos_t, sin_t, layer):
    n_ctx_tiles = N_CTX // TM
    lat_tiles = DEC_SEQ // TM

    def rope_map(i):
        return (jnp.where(i < n_ctx_tiles, 0, (i - n_ctx_tiles) % lat_tiles), 0)

    return pl.pallas_call(
        _inproj_kernel,
        grid=(N_TOK // TM,),
        in_specs=[
            pl.BlockSpec((TM, D_MODEL), lambda i: (i, 0)),
            pl.BlockSpec((None, 1, 6 * D_MODEL),
                         lambda i: (layer * COND_ROWS + _cond_index(i), 0, 0)),
            pl.BlockSpec((1, D_MODEL), lambda i: (0, 0)),
            pl.BlockSpec((D_MODEL, IN_COLS), lambda i: (0, 0)),
            pl.BlockSpec((TM, LANE), rope_map),
            pl.BlockSpec((TM, LANE), rope_map),
        ],
        out_specs=[
            pl.BlockSpec((TM, Q_DIM), lambda i: (i, 0)),
            pl.BlockSpec((TM, 2 * KV_DIM), lambda i: (i, 0)),
            pl.BlockSpec((TM, C_CONV), lambda i: (i, 0)),
            pl.BlockSpec((TM, 2 * D_MODEL), lambda i: (i, 0)),
        ],
        out_shape=[
            jax.ShapeDtypeStruct((N_TOK, Q_DIM), _BF16),
            jax.ShapeDtypeStruct((N_TOK, 2 * KV_DIM), _F32),
            jax.ShapeDtypeStruct((N_TOK, C_CONV), _F32),
            jax.ShapeDtypeStruct((N_TOK, 2 * D_MODEL), _BF16),
        ],
        compiler_params=pltpu.CompilerParams(
            dimension_semantics=("arbitrary",), vmem_limit_bytes=VMEM_LIMIT),
        name="inproj",
    )(x, mod3, g_pre, w_in_bf, cos_t, sin_t)


def _attend_head(q, keys, vals, masks, sink):
    scores = []
    for k, mask in zip(keys, masks):
        s = lax.dot_general(q, k, (((1,), (1,)), ((), ())),
                            preferred_element_type=_F32) * ATTN_SCALE
        if mask is not None:
            s = jnp.where(mask, s, -jnp.inf)
        scores.append(s)
    m = jnp.full((q.shape[0], 1), sink, _F32)
    for s in scores:
        m = jnp.maximum(m, jnp.max(s, axis=-1, keepdims=True))
    denom = jnp.exp(sink - m)
    out = jnp.zeros((q.shape[0], HEAD_DIM), _F32)
    for s, v in zip(scores, vals):
        p = jnp.exp(s - m)
        denom = denom + jnp.sum(p, axis=-1, keepdims=True)
        out = out + jnp.dot(p.astype(_BF16), v, preferred_element_type=_F32)
    return out / denom


def _ctx_attn_kernel(sink_ref, q_ref, kv_ref, o_ref):
    for hk in range(N_KV_HEADS):
        k = kv_ref[:, hk * HEAD_DIM:(hk + 1) * HEAD_DIM].astype(_BF16)
        v = kv_ref[:, KV_DIM + hk * HEAD_DIM:KV_DIM + (hk + 1) * HEAD_DIM].astype(_BF16)
        for gi in range(GROUP):
            n = hk * GROUP + gi
            sl = slice(n * HEAD_DIM, (n + 1) * HEAD_DIM)
            o = _attend_head(q_ref[:, sl], [k], [v], [None], sink_ref[n])
            o_ref[:, sl] = o.astype(_BF16)


def _ctx_attention(q, kv, sink):
    return pl.pallas_call(
        _ctx_attn_kernel,
        grid=(BATCH,),
        in_specs=[
            pl.BlockSpec(memory_space=pltpu.SMEM),
            pl.BlockSpec((SEQ, Q_DIM), lambda b: (b, 0)),
            pl.BlockSpec((SEQ, 2 * KV_DIM), lambda b: (b, 0)),
        ],
        out_specs=pl.BlockSpec((SEQ, Q_DIM), lambda b: (b, 0)),
        out_shape=jax.ShapeDtypeStruct((N_CTX, Q_DIM), _BF16),
        compiler_params=pltpu.CompilerParams(
            dimension_semantics=("arbitrary",), vmem_limit_bytes=VMEM_LIMIT),
        name="ctx_attention",
    )(sink, q, kv)


def _lat_attn_kernel(sink_ref, q_ref, kv_ref, ck_ref, cv_ref, o_ref):
    qb = pl.program_id(1)
    q_start = qb * TQ
    k_start = pl.multiple_of(jnp.clip(q_start - WINDOW, 0, DEC_SEQ - KWIN), WINDOW)
    qpos = q_start + lax.broadcasted_iota(jnp.int32, (TQ, KWIN), 0)
    kpos = k_start + lax.broadcasted_iota(jnp.int32, (TQ, KWIN), 1)
    mask = jnp.abs(kpos - qpos) <= WINDOW
    for hk in range(N_KV_HEADS):
        ks = slice(hk * HEAD_DIM, (hk + 1) * HEAD_DIM)
        vs = slice(KV_DIM + hk * HEAD_DIM, KV_DIM + (hk + 1) * HEAD_DIM)
        k = kv_ref[pl.ds(k_start, KWIN), ks].astype(_BF16)
        v = kv_ref[pl.ds(k_start, KWIN), vs].astype(_BF16)
        kc = ck_ref[:, ks].astype(_BF16)
        vc = cv_ref[:, ks].astype(_BF16)
        for gi in range(GROUP):
            n = hk * GROUP + gi
            sl = slice(n * HEAD_DIM, (n + 1) * HEAD_DIM)
            o = _attend_head(q_ref[:, sl], [k, kc], [v, vc], [mask, None], sink_ref[n])
            o_ref[:, sl] = o.astype(_BF16)


def _lat_attention(q, kv, cache_k, cache_v, sink, layer):
    q_off = N_CTX // TQ
    kv_off = N_CTX // DEC_SEQ
    nq = DEC_SEQ // TQ
    return pl.pallas_call(
        _lat_attn_kernel,
        grid=(DEC_BATCH, nq),
        in_specs=[
            pl.BlockSpec(memory_space=pltpu.SMEM),
            pl.BlockSpec((TQ, Q_DIM), lambda b, i: (q_off + b * nq + i, 0)),
            pl.BlockSpec((DEC_SEQ, 2 * KV_DIM), lambda b, i: (kv_off + b, 0)),
            pl.BlockSpec((None, None, PAST_LEN, KV_DIM), lambda b, i: (b, layer, 0, 0)),
            pl.BlockSpec((None, None, PAST_LEN, KV_DIM), lambda b, i: (b, layer, 0, 0)),
        ],
        out_specs=pl.BlockSpec((TQ, Q_DIM), lambda b, i: (b * nq + i, 0)),
        out_shape=jax.ShapeDtypeStruct((N_LAT, Q_DIM), _BF16),
        compiler_params=pltpu.CompilerParams(
            dimension_semantics=("arbitrary", "arbitrary"), vmem_limit_bytes=VMEM_LIMIT),
        name="lat_attention",
    )(sink, q, kv, cache_k, cache_v)


def _conv_kernel(prev_ref, cur_ref, next_ref, w_ref, b_ref, lg_ref, lb_ref, y_ref, pad_ref):
    i = pl.program_id(0)
    n_ctx_tiles = N_CTX // SEQ
    tiles_per_seq = jnp.where(i < n_ctx_tiles, 1, DEC_SEQ // SEQ)
    j = jnp.where(i < n_ctx_tiles, 0, (i - n_ctx_tiles) % (DEC_SEQ // SEQ))
    pad_ref[0:HALO, :] = jnp.where(j > 0, prev_ref[...], 0.0)
    pad_ref[HALO:HALO + SEQ, :] = cur_ref[...]
    pad_ref[HALO + SEQ:HALO + SEQ + HALO, :] = jnp.where(j < tiles_per_seq - 1, next_ref[...], 0.0)

    rows = 64
    for r0 in range(0, SEQ, rows):
        acc = jnp.zeros((rows, C_CONV), _F32) + b_ref[...]
        for t in range(CONV_WIDTH):
            start = HALO - CONV_PAD + r0 + t
            acc = acc + pad_ref[start:start + rows, :] * w_ref[t:t + 1, :]
        mu = jnp.mean(acc, axis=-1, keepdims=True)
        d = acc - mu
        var = jnp.mean(d * d, axis=-1, keepdims=True)
        y = d * lax.rsqrt(var + EPS) * lg_ref[...] + lb_ref[...]
        y_ref[r0:r0 + rows, :] = (y * jax.nn.sigmoid(y)).astype(_BF16)


def _conv_branch(u, conv_w, conv_b, ln_g, ln_b):
    n_tiles = N_TOK // SEQ
    hb = SEQ // HALO
    last = N_TOK // HALO - 1
    return pl.pallas_call(
        _conv_kernel,
        grid=(n_tiles,),
        in_specs=[
            pl.BlockSpec((HALO, C_CONV), lambda i: (jnp.maximum(i * hb - 1, 0), 0)),
            pl.BlockSpec((SEQ, C_CONV), lambda i: (i, 0)),
            pl.BlockSpec((HALO, C_CONV), lambda i: (jnp.minimum((i + 1) * hb, last), 0)),
            pl.BlockSpec((CONV_WIDTH, C_CONV), lambda i: (0, 0)),
            pl.BlockSpec((1, C_CONV), lambda i: (0, 0)),
            pl.BlockSpec((1, C_CONV), lambda i: (0, 0)),
            pl.BlockSpec((1, C_CONV), lambda i: (0, 0)),
        ],
        out_specs=pl.BlockSpec((SEQ, C_CONV), lambda i: (i, 0)),
        out_shape=jax.ShapeDtypeStruct((N_TOK, C_CONV), _BF16),
        scratch_shapes=[pltpu.VMEM((SEQ + 2 * HALO, C_CONV), _F32)],
        compiler_params=pltpu.CompilerParams(
            dimension_semantics=("arbitrary",), vmem_limit_bytes=VMEM_LIMIT),
        name="conv_branch",
    )(u, u, u, conv_w, conv_b, ln_g, ln_b)


def _mix_kernel(x_ref, mod_ref, att_ref, cv_ref, sg_ref, wa_ref, wc_ref, wo_ref,
                gpost_ref, gffn_ref, wr_ref, br_ref,
                x1_ref, h2_ref, idx_ref, wgt_ref):
    a = jnp.dot(att_ref[...], wa_ref[...], preferred_element_type=_F32)
    cv = jnp.dot(cv_ref[...], wc_ref[...], preferred_element_type=_F32)
    m = sg_ref[:, 0:D_MODEL].astype(_F32) * a + sg_ref[:, D_MODEL:].astype(_F32) * cv
    mix = jnp.dot(m.astype(_BF16), wo_ref[...], preferred_element_type=_F32)
    gt1 = mod_ref[:, 2 * D_MODEL:3 * D_MODEL]
    sh2 = mod_ref[:, 3 * D_MODEL:4 * D_MODEL]
    sc2 = mod_ref[:, 4 * D_MODEL:5 * D_MODEL]
    x1 = x_ref[...] + gt1 * _rms(mix, gpost_ref[...])
    x1_ref[...] = x1
    h2 = _rms(x1, gffn_ref[...]) * (1.0 + sc2) + sh2
    h2_ref[...] = h2.astype(_BF16)

    logits = jnp.dot(h2, wr_ref[...], precision=lax.Precision.HIGHEST,
                     preferred_element_type=_F32) + br_ref[...]
    lane = lax.broadcasted_iota(jnp.int32, logits.shape, 1).astype(_F32)
    out_lane = lax.broadcasted_iota(jnp.int32, (logits.shape[0], LANE), 1)
    idx_out = jnp.zeros((logits.shape[0], LANE), _F32)
    e_out = jnp.zeros((logits.shape[0], LANE), _F32)
    top = None
    total = jnp.zeros((logits.shape[0], 1), _F32)
    for k in range(TOP_K):
        mval = jnp.max(logits, axis=-1, keepdims=True)
        sel = jnp.min(jnp.where(logits == mval, lane, float(N_EXPERTS)), axis=-1, keepdims=True)
        if top is None:
            top = mval
        e = jnp.exp(mval - top)
        total = total + e
        idx_out = jnp.where(out_lane == k, sel, idx_out)
        e_out = jnp.where(out_lane == k, e, e_out)
        logits = jnp.where(lane == sel, -jnp.inf, logits)
    idx_ref[...] = idx_out.astype(jnp.int32)
    wgt_ref[...] = e_out / total


def _mix(x, mod3, att, cvn, sg, wa, wc, wo, g_post, g_ffn, w_router, b_router, layer):
    full = lambda shape: pl.BlockSpec(shape, lambda i: (0,) * len(shape))
    return pl.pallas_call(
        _mix_kernel,
        grid=(N_TOK // TM,),
        in_specs=[
            pl.BlockSpec((TM, D_MODEL), lambda i: (i, 0)),
            pl.BlockSpec((None, 1, 6 * D_MODEL),
                         lambda i: (layer * COND_ROWS + _cond_index(i), 0, 0)),
            pl.BlockSpec((TM, Q_DIM), lambda i: (i, 0)),
            pl.BlockSpec((TM, C_CONV), lambda i: (i, 0)),
            pl.BlockSpec((TM, 2 * D_MODEL), lambda i: (i, 0)),
            full((Q_DIM, D_MODEL)),
            full((C_CONV, D_MODEL)),
            full((D_MODEL, D_MODEL)),
            full((1, D_MODEL)),
            full((1, D_MODEL)),
            full((D_MODEL, N_EXPERTS)),
            full((1, N_EXPERTS)),
        ],
        out_specs=[
            pl.BlockSpec((TM, D_MODEL), lambda i: (i, 0)),
            pl.BlockSpec((TM, D_MODEL), lambda i: (i, 0)),
            pl.BlockSpec((TM, LANE), lambda i: (i, 0)),
            pl.BlockSpec((TM, LANE), lambda i: (i, 0)),
        ],
        out_shape=[
            jax.ShapeDtypeStruct((N_TOK, D_MODEL), _F32),
            jax.ShapeDtypeStruct((N_TOK, D_MODEL), _BF16),
            jax.ShapeDtypeStruct((N_TOK, LANE), jnp.int32),
            jax.ShapeDtypeStruct((N_TOK, LANE), _F32),
        ],
        compiler_params=pltpu.CompilerParams(
            dimension_semantics=("arbitrary",), vmem_limit_bytes=VMEM_LIMIT),
        name="mix_router",
    )(x, mod3, att, cvn, sg, wa, wc, wo, g_post, g_ffn, w_router, b_router)


def _expert_kernel(te_ref, nt_ref, x_ref, wr_ref, wgu_ref, bgu_ref, wd_ref, bd_ref,
                   y_ref, wgu_bf, wd_bf):
    j = pl.program_id(0)
    changed = jnp.logical_or(j == 0, te_ref[j] != te_ref[jnp.maximum(j - 1, 0)])

    @pl.when(changed)
    def _():
        wgu_bf[...] = wgu_ref[...].astype(_BF16)
        wd_bf[...] = wd_ref[...].astype(_BF16)

    @pl.when(j < nt_ref[0])
    def _():
        gu = jnp.dot(x_ref[...], wgu_bf[...], preferred_element_type=_F32) + bgu_ref[...]
        gate = jnp.minimum(gu[:, :D_FF], SWIGLU_LIMIT)
        lin = jnp.clip(gu[:, D_FF:], -SWIGLU_LIMIT, SWIGLU_LIMIT)
        act = gate * jax.nn.sigmoid(SWIGLU_ALPHA * gate) * (lin + 1.0)
        y = jnp.dot(act.astype(_BF16), wd_bf[...], preferred_element_type=_F32) + bd_ref[...]
        y_ref[...] = (y * wr_ref[...]).astype(_BF16)

    @pl.when(j >= nt_ref[0])
    def _():
        y_ref[...] = jnp.zeros_like(y_ref)


def _experts(tile_expert, n_tiles, x_sorted, w_row, w_gate_up, b_gate_up, w_down, b_down, layer):
    grid_spec = pltpu.PrefetchScalarGridSpec(
        num_scalar_prefetch=2,
        grid=(N_ETILES,),
        in_specs=[
            pl.BlockSpec((TE, D_MODEL), lambda j, te, nt: (j, 0)),
            pl.BlockSpec((TE, 1), lambda j, te, nt: (j, 0)),
            pl.BlockSpec((None, None, D_MODEL, 2 * D_FF), lambda j, te, nt: (layer, te[j], 0, 0)),
            pl.BlockSpec((None, None, 1, 2 * D_FF), lambda j, te, nt: (layer, te[j], 0, 0)),
            pl.BlockSpec((None, None, D_FF, D_MODEL), lambda j, te, nt: (layer, te[j], 0, 0)),
            pl.BlockSpec((None, None, 1, D_MODEL), lambda j, te, nt: (layer, te[j], 0, 0)),
        ],
        out_specs=pl.BlockSpec((TE, D_MODEL), lambda j, te, nt: (j, 0)),
        scratch_shapes=[
            pltpu.VMEM((D_MODEL, 2 * D_FF), _BF16),
            pltpu.VMEM((D_FF, D_MODEL), _BF16),
        ],
    )
    return pl.pallas_call(
        _expert_kernel,
        grid_spec=grid_spec,
        out_shape=jax.ShapeDtypeStruct((N_ETILES * TE, D_MODEL), _BF16),
        compiler_params=pltpu.CompilerParams(
            dimension_semantics=("arbitrary",), vmem_limit_bytes=VMEM_LIMIT),
        name="experts",
    )(tile_expert, n_tiles, x_sorted, w_row, w_gate_up,
      b_gate_up.reshape(DEPTH, N_EXPERTS, 1, 2 * D_FF), w_down,
      b_down.reshape(DEPTH, N_EXPERTS, 1, D_MODEL))


def _route(top_idx, top_w):
    e_flat = top_idx.reshape(-1)
    w_flat = top_w.reshape(-1)
    onehot = (e_flat[:, None] == jnp.arange(N_EXPERTS, dtype=jnp.int32)[None, :]).astype(jnp.int32)
    csum = jnp.cumsum(onehot, axis=0)
    counts = csum[-1]
    rank = jnp.take_along_axis(csum, e_flat[:, None], axis=1)[:, 0] - 1
    padded = ((counts + TE - 1) // TE) * TE
    off_end = jnp.cumsum(padded)
    off = off_end - padded
    start = jnp.cumsum(counts) - counts
    n_tiles = off_end[-1] // TE
    pos = off[e_flat] + rank

    tile_row = jnp.arange(N_ETILES, dtype=jnp.int32) * TE
    te = jnp.searchsorted(off_end, tile_row, side="right").astype(jnp.int32)
    te_last = te[jnp.maximum(n_tiles - 1, 0)]
    te = jnp.where(jnp.arange(N_ETILES) < n_tiles, te, te_last)

    order = jnp.argsort(e_flat, stable=True).astype(jnp.int32)
    r = jnp.arange(N_ETILES * TE, dtype=jnp.int32)
    er = te[r // TE]
    local = r - off[er]
    valid = (local >= 0) & (local < counts[er]) & (r < off_end[-1])
    src = order[jnp.clip(start[er] + local, 0, N_PAIRS - 1)]
    tok_row = jnp.where(valid, src // TOP_K, 0)
    w_row = jnp.where(valid, w_flat[src], 0.0)
    return te, n_tiles.reshape(1).astype(jnp.int32), tok_row, w_row[:, None], pos.reshape(N_TOK, TOP_K)


def _ffn_res_kernel(x_ref, mod_ref, y_ref, g_ref, o_ref):
    gt2 = mod_ref[:, 5 * D_MODEL:6 * D_MODEL]
    o_ref[...] = x_ref[...] + gt2 * _rms(y_ref[...], g_ref[...])


def _ffn_residual(x1, mod3, moe, g_post, layer):
    return pl.pallas_call(
        _ffn_res_kernel,
        grid=(N_TOK // TM,),
        in_specs=[
            pl.BlockSpec((TM, D_MODEL), lambda i: (i, 0)),
            pl.BlockSpec((None, 1, 6 * D_MODEL),
                         lambda i: (layer * COND_ROWS + _cond_index(i), 0, 0)),
            pl.BlockSpec((TM, D_MODEL), lambda i: (i, 0)),
            pl.BlockSpec((1, D_MODEL), lambda i: (0, 0)),
        ],
        out_specs=pl.BlockSpec((TM, D_MODEL), lambda i: (i, 0)),
        out_shape=jax.ShapeDtypeStruct((N_TOK, D_MODEL), _F32),
        compiler_params=pltpu.CompilerParams(
            dimension_semantics=("arbitrary",), vmem_limit_bytes=VMEM_LIMIT),
        name="ffn_residual",
    )(x1, mod3, moe, g_post)


def _rope_tables():
    pos = jnp.arange(DEC_SEQ)
    row = (pos // GRID_W).astype(_F32)
    col = (pos % GRID_W).astype(_F32)
    inv = ROPE_THETA ** (-jnp.arange(ROPE_FREQS, dtype=_F32) / ROPE_FREQS)
    ang_r = row[:, None] * inv[None, :]
    ang_c = col[:, None] * inv[None, :]
    cos = jnp.concatenate([jnp.cos(ang_r)] * 2 + [jnp.cos(ang_c)] * 2, axis=-1)
    sin = jnp.concatenate([-jnp.sin(ang_r), jnp.sin(ang_r), -jnp.sin(ang_c), jnp.sin(ang_c)], axis=-1)
    reps = LANE // HEAD_DIM
    return jnp.tile(cos, (1, reps)), jnp.tile(sin, (1, reps))


def kernel(x_prompt, x_sample, cache_k, cache_v, c, c_ctx, w_ada, b_ada, g_pre_mix, g_post_mix,
           g_pre_ffn, g_post_ffn, w_in, attn_sink, w_attn_o, conv_w, conv_b, conv_ln_g, conv_ln_b,
           w_conv_o, w_out, w_router, b_router, w_gate_up, b_gate_up, w_down, b_down):
    x = jnp.concatenate([x_prompt.reshape(N_CTX, D_MODEL), x_sample.reshape(N_LAT, D_MODEL)], axis=0)
    cond = jnp.concatenate([c_ctx[None, :], c, jnp.zeros((COND_ROWS - N_COND, D_MODEL), _F32)], axis=0)
    mod = _modulation(cond, w_ada, b_ada)
    mod3 = mod.reshape(DEPTH * COND_ROWS, 1, 6 * D_MODEL)
    cos_t, sin_t = _rope_tables()
    ck = cache_k.reshape(DEC_BATCH, DEPTH, PAST_LEN, KV_DIM)
    cv = cache_v.reshape(DEC_BATCH, DEPTH, PAST_LEN, KV_DIM)

    new_k, new_v = [], []
    for l in range(DEPTH):
        row = lambda a: a[l][None, :]
        q, kv, u, sg = _inproj(x, mod3, row(g_pre_mix), w_in[l].astype(_BF16), cos_t, sin_t, l)
        new_k.append(kv[:N_CTX, :KV_DIM].reshape(BATCH, SEQ, N_KV_HEADS, HEAD_DIM))
        new_v.append(kv[:N_CTX, KV_DIM:].reshape(BATCH, SEQ, N_KV_HEADS, HEAD_DIM))
        att_ctx = _ctx_attention(q, kv, attn_sink[l])
        att_lat = _lat_attention(q, kv, ck, cv, attn_sink[l], l)
        att = jnp.concatenate([att_ctx, att_lat], axis=0)
        cvn = _conv_branch(u, conv_w[l], row(conv_b), row(conv_ln_g), row(conv_ln_b))
        x1, h2, top_idx, top_w = _mix(
            x, mod3, att, cvn, sg, w_attn_o[l].astype(_BF16), w_conv_o[l].astype(_BF16),
            w_out[l].astype(_BF16), row(g_post_mix), row(g_pre_ffn), w_router[l], row(b_router), l)
        te, n_tiles, tok_row, w_row, pos = _route(top_idx[:, :TOP_K], top_w[:, :TOP_K])
        x_sorted = jnp.take(h2, tok_row, axis=0)
        ys = _experts(te, n_tiles, x_sorted, w_row, w_gate_up, b_gate_up, w_down, b_down, l)
        moe = jnp.sum(jnp.take(ys, pos, axis=0).astype(_F32), axis=1)
        x = _ffn_residual(x1, mod3, moe, row(g_post_ffn), l)

    y_prompt = x[:N_CTX].reshape(BATCH, SEQ, D_MODEL)
    y_sample = x[N_CTX:].reshape(DEC_BATCH, DEC_SEQ, D_MODEL)
    return (y_prompt, y_sample, jnp.stack(new_k, axis=1), jnp.stack(new_v, axis=1))
```

```python
import jax
import jax.numpy as jnp
from jax import lax
from jax.experimental import pallas as pl
from jax.experimental.pallas import tpu as pltpu

D_MODEL = 1024
BATCH = 16
SEQ = 256
DEPTH = 2
DEC_BATCH = 2
DEC_SEQ = 2048
PAST_LEN = 256
GRID_W = 64
N_HEADS = 16
N_KV_HEADS = 4
GROUP = N_HEADS // N_KV_HEADS
HEAD_DIM = 64
Q_DIM = N_HEADS * HEAD_DIM
KV_DIM = N_KV_HEADS * HEAD_DIM
WINDOW = 128
ATTN_SCALE = HEAD_DIM ** -0.5
ROPE_THETA = 10000.0
ROPE_HALF = HEAD_DIM // 2
ROPE_FREQS = ROPE_HALF // 2
C_CONV = D_MODEL // 2
CONV_WIDTH = 31
CONV_PAD = (CONV_WIDTH - 1) // 2
N_EXPERTS = 32
TOP_K = 4
D_FF = D_MODEL
SWIGLU_LIMIT = 7.0
SWIGLU_ALPHA = 1.702
EPS = 1e-6
IN_COLS = Q_DIM + 2 * KV_DIM + 2 * C_CONV + 2 * D_MODEL

N_CTX = BATCH * SEQ
N_LAT = DEC_BATCH * DEC_SEQ
N_TOK = N_CTX + N_LAT
N_COND = 1 + DEC_BATCH
COND_ROWS = 8

LANE = 128
SUBLANE = 8
TM = 512
TQ = 256
KWIN = TQ + 2 * WINDOW
HALO = 16

TB = 256
N_BLOCKS = N_TOK // TB
UNIT = SUBLANE
SLOTS = 1280
BLOCK_UNITS = SLOTS // UNIT
TE = 256
TILE_UNITS = TE // UNIT
N_ETILES = (N_BLOCKS * BLOCK_UNITS) // TILE_UNITS + N_EXPERTS
PACK = D_MODEL // 2
VMEM_LIMIT = 56 * 1024 * 1024

assert SLOTS >= TB * TOP_K + N_EXPERTS * (UNIT - 1) and SLOTS % UNIT == 0

_F32 = jnp.float32
_BF16 = jnp.bfloat16
_U32 = jnp.uint32
_I32 = jnp.int32


def _rms(x, g):
    return x * lax.rsqrt(jnp.mean(x * x, axis=-1, keepdims=True) + EPS) * g


def _cond_index(i, tile):
    n_ctx_tiles = N_CTX // tile
    return jnp.where(i < n_ctx_tiles, 0, 1 + (i - n_ctx_tiles) // (DEC_SEQ // tile))


def _pack_halves(lo, hi):
    return (lax.bitcast_convert_type(lo, _U32) >> 16) | lax.bitcast_convert_type(hi, _U32)


def _unpack_halves(w):
    lo = lax.bitcast_convert_type(w << 16, _F32).astype(_BF16)
    hi = lax.bitcast_convert_type(w & jnp.uint32(0xFFFF0000), _F32).astype(_BF16)
    return lo, hi


def _mod_kernel(cond_ref, w_ref, b_ref, out_ref):
    cnd = cond_ref[...]
    s = cnd * jax.nn.sigmoid(cnd)
    out_ref[...] = jnp.dot(s, w_ref[...], precision=lax.Precision.HIGHEST,
                           preferred_element_type=_F32) + b_ref[...]


def _modulation(cond, w_ada, b_ada):
    tn = 1536
    nt = 6 * D_MODEL // tn
    return pl.pallas_call(
        _mod_kernel,
        grid=(DEPTH, nt),
        in_specs=[
            pl.BlockSpec((COND_ROWS, D_MODEL), lambda l, n: (0, 0)),
            pl.BlockSpec((None, D_MODEL, tn), lambda l, n: (l, 0, n)),
            pl.BlockSpec((None, 1, tn), lambda l, n: (l, 0, n)),
        ],
        out_specs=pl.BlockSpec((None, COND_ROWS, tn), lambda l, n: (l, 0, n)),
        out_shape=jax.ShapeDtypeStruct((DEPTH, COND_ROWS, 6 * D_MODEL), _F32),
        compiler_params=pltpu.CompilerParams(
            dimension_semantics=("arbitrary", "arbitrary"), vmem_limit_bytes=VMEM_LIMIT),
        name="modulation",
    )(cond, w_ada, b_ada.reshape(DEPTH, 1, 6 * D_MODEL))


def _rope_chunk(x, cos, sin):
    lane = lax.broadcasted_iota(_I32, x.shape, 1)
    partner = jnp.where((lane & ROPE_FREQS) == 0,
                        pltpu.roll(x, LANE - ROPE_FREQS, 1), pltpu.roll(x, ROPE_FREQS, 1))
    return x * cos + partner * sin


def _inproj_kernel(x_ref, mod_ref, g_ref, w_ref, cos_ref, sin_ref,
                   q_ref, kv_ref, u_ref, sg_ref):
    i = pl.program_id(0)
    x = x_ref[...]
    sh = mod_ref[:, 0:D_MODEL]
    sc = mod_ref[:, D_MODEL:2 * D_MODEL]
    h = (_rms(x, g_ref[...]) * (1.0 + sc) + sh).astype(_BF16)

    c0 = 0
    q = jnp.dot(h, w_ref[:, c0:c0 + Q_DIM], preferred_element_type=_F32)
    c0 += Q_DIM
    kv = jnp.dot(h, w_ref[:, c0:c0 + 2 * KV_DIM], preferred_element_type=_F32)
    c0 += 2 * KV_DIM
    ua = jnp.dot(h, w_ref[:, c0:c0 + C_CONV], preferred_element_type=_F32)
    c0 += C_CONV
    ub = jnp.dot(h, w_ref[:, c0:c0 + C_CONV], preferred_element_type=_F32)
    c0 += C_CONV
    g = jnp.dot(h, w_ref[:, c0:c0 + 2 * D_MODEL], preferred_element_type=_F32)

    u_ref[...] = ua * jax.nn.sigmoid(ub)
    sg_ref[...] = jax.nn.sigmoid(g).astype(_BF16)

    is_latent = i >= N_CTX // TM

    @pl.when(jnp.logical_not(is_latent))
    def _():
        q_ref[...] = q.astype(_BF16)
        kv_ref[...] = kv

    @pl.when(is_latent)
    def _():
        cos = cos_ref[...]
        sin = sin_ref[...]
        for j in range(Q_DIM // LANE):
            sl = slice(j * LANE, (j + 1) * LANE)
            q_ref[:, sl] = _rope_chunk(q[:, sl], cos, sin).astype(_BF16)
        for j in range(KV_DIM // LANE):
            sl = slice(j * LANE, (j + 1) * LANE)
            kv_ref[:, sl] = _rope_chunk(kv[:, sl], cos, sin)
        kv_ref[:, KV_DIM:] = kv[:, KV_DIM:]


def _inproj(x, mod3, g_pre, w_in_bf, cos_t, sin_t, layer):
    n_ctx_tiles = N_CTX // TM
    lat_tiles = DEC_SEQ // TM

    def rope_map(i):
        return (jnp.where(i < n_ctx_tiles, 0, (i - n_ctx_tiles) % lat_tiles), 0)

    return pl.pallas_call(
        _inproj_kernel,
        grid=(N_TOK // TM,),
        in_specs=[
            pl.BlockSpec((TM, D_MODEL), lambda i: (i, 0)),
            pl.BlockSpec((None, 1, 6 * D_MODEL),
                         lambda i: (layer * COND_ROWS + _cond_index(i, TM), 0, 0)),
            pl.BlockSpec((1, D_MODEL), lambda i: (0, 0)),
            pl.BlockSpec((D_MODEL, IN_COLS), lambda i: (0, 0)),
            pl.BlockSpec((TM, LANE), rope_map),
            pl.BlockSpec((TM, LANE), rope_map),
        ],
        out_specs=[
            pl.BlockSpec((TM, Q_DIM), lambda i: (i, 0)),
            pl.BlockSpec((TM, 2 * KV_DIM), lambda i: (i, 0)),
            pl.BlockSpec((TM, C_CONV), lambda i: (i, 0)),
            pl.BlockSpec((TM, 2 * D_MODEL), lambda i: (i, 0)),
        ],
        out_shape=[
            jax.ShapeDtypeStruct((N_TOK, Q_DIM), _BF16),
            jax.ShapeDtypeStruct((N_TOK, 2 * KV_DIM), _F32),
            jax.ShapeDtypeStruct((N_TOK, C_CONV), _F32),
            jax.ShapeDtypeStruct((N_TOK, 2 * D_MODEL), _BF16),
        ],
        compiler_params=pltpu.CompilerParams(
            dimension_semantics=("arbitrary",), vmem_limit_bytes=VMEM_LIMIT),
        name="inproj",
    )(x, mod3, g_pre, w_in_bf, cos_t, sin_t)


def _attend_head(q, keys, vals, masks, sink):
    scores = []
    for k, mask in zip(keys, masks):
        s = lax.dot_general(q, k, (((1,), (1,)), ((), ())),
                            preferred_element_type=_F32) * ATTN_SCALE
        if mask is not None:
            s = jnp.where(mask, s, -jnp.inf)
        scores.append(s)
    m = jnp.full((q.shape[0], 1), sink, _F32)
    for s in scores:
        m = jnp.maximum(m, jnp.max(s, axis=-1, keepdims=True))
    denom = jnp.exp(sink - m)
    out = jnp.zeros((q.shape[0], HEAD_DIM), _F32)
    for s, v in zip(scores, vals):
        p = jnp.exp(s - m)
        denom = denom + jnp.sum(p, axis=-1, keepdims=True)
        out = out + jnp.dot(p.astype(_BF16), v, preferred_element_type=_F32)
    return out / denom


def _attn_kernel(sink_ref, q_ref, kv_own_ref, kv_seq_ref, ck_ref, cv_ref, o_ref):
    i = pl.program_id(0)
    n_ctx_steps = N_CTX // TQ

    @pl.when(i < n_ctx_steps)
    def _():
        for hk in range(N_KV_HEADS):
            k = kv_own_ref[:, hk * HEAD_DIM:(hk + 1) * HEAD_DIM].astype(_BF16)
            v = kv_own_ref[:, KV_DIM + hk * HEAD_DIM:KV_DIM + (hk + 1) * HEAD_DIM].astype(_BF16)
            for gi in range(GROUP):
                n = hk * GROUP + gi
                sl = slice(n * HEAD_DIM, (n + 1) * HEAD_DIM)
                o = _attend_head(q_ref[:, sl], [k], [v], [None], sink_ref[n])
                o_ref[:, sl] = o.astype(_BF16)

    @pl.when(i >= n_ctx_steps)
    def _():
        qb = (i - n_ctx_steps) % (DEC_SEQ // TQ)
        q_start = qb * TQ
        k_start = pl.multiple_of(jnp.clip(q_start - WINDOW, 0, DEC_SEQ - KWIN), WINDOW)
        qpos = q_start + lax.broadcasted_iota(_I32, (TQ, KWIN), 0)
        kpos = k_start + lax.broadcasted_iota(_I32, (TQ, KWIN), 1)
        mask = jnp.abs(kpos - qpos) <= WINDOW
        for hk in range(N_KV_HEADS):
            ks = slice(hk * HEAD_DIM, (hk + 1) * HEAD_DIM)
            vs = slice(KV_DIM + hk * HEAD_DIM, KV_DIM + (hk + 1) * HEAD_DIM)
            k = kv_seq_ref[pl.ds(k_start, KWIN), ks].astype(_BF16)
            v = kv_seq_ref[pl.ds(k_start, KWIN), vs].astype(_BF16)
            kc = ck_ref[:, ks].astype(_BF16)
            vc = cv_ref[:, ks].astype(_BF16)
            for gi in range(GROUP):
                n = hk * GROUP + gi
                sl = slice(n * HEAD_DIM, (n + 1) * HEAD_DIM)
                o = _attend_head(q_ref[:, sl], [k, kc], [v, vc], [mask, None], sink_ref[n])
                o_ref[:, sl] = o.astype(_BF16)


def _attention(q, kv, cache_k, cache_v, sink, layer):
    n_ctx_steps = N_CTX // TQ
    nq = DEC_SEQ // TQ
    kv_off = N_CTX // DEC_SEQ

    def lat_batch(i):
        return jnp.maximum(i - n_ctx_steps, 0) // nq

    return pl.pallas_call(
        _attn_kernel,
        grid=(N_TOK // TQ,),
        in_specs=[
            pl.BlockSpec(memory_space=pltpu.SMEM),
            pl.BlockSpec((TQ, Q_DIM), lambda i: (i, 0)),
            pl.BlockSpec((TQ, 2 * KV_DIM), lambda i: (i, 0)),
            pl.BlockSpec((DEC_SEQ, 2 * KV_DIM), lambda i: (kv_off + lat_batch(i), 0)),
            pl.BlockSpec((None, None, PAST_LEN, KV_DIM), lambda i: (lat_batch(i), layer, 0, 0)),
            pl.BlockSpec((None, None, PAST_LEN, KV_DIM), lambda i: (lat_batch(i), layer, 0, 0)),
        ],
        out_specs=pl.BlockSpec((TQ, Q_DIM), lambda i: (i, 0)),
        out_shape=jax.ShapeDtypeStruct((N_TOK, Q_DIM), _BF16),
        compiler_params=pltpu.CompilerParams(
            dimension_semantics=("arbitrary",), vmem_limit_bytes=VMEM_LIMIT),
        name="attention",
    )(sink, q, kv, kv, cache_k, cache_v)


def _conv_kernel(prev_ref, cur_ref, next_ref, w_ref, b_ref, lg_ref, lb_ref, y_ref, pad_ref):
    i = pl.program_id(0)
    n_ctx_tiles = N_CTX // SEQ
    tiles_per_seq = jnp.where(i < n_ctx_tiles, 1, DEC_SEQ // SEQ)
    j = jnp.where(i < n_ctx_tiles, 0, (i - n_ctx_tiles) % (DEC_SEQ // SEQ))
    pad_ref[0:HALO, :] = jnp.where(j > 0, prev_ref[...], 0.0)
    pad_ref[HALO:HALO + SEQ, :] = cur_ref[...]
    pad_ref[HALO + SEQ:HALO + SEQ + HALO, :] = jnp.where(j < tiles_per_seq - 1, next_ref[...], 0.0)

    rows = 64
    for r0 in range(0, SEQ, rows):
        acc = jnp.zeros((rows, C_CONV), _F32) + b_ref[...]
        for t in range(CONV_WIDTH):
            start = HALO - CONV_PAD + r0 + t
            acc = acc + pad_ref[start:start + rows, :] * w_ref[t:t + 1, :]
        mu = jnp.mean(acc, axis=-1, keepdims=True)
        d = acc - mu
        var = jnp.mean(d * d, axis=-1, keepdims=True)
        y = d * lax.rsqrt(var + EPS) * lg_ref[...] + lb_ref[...]
        y_ref[r0:r0 + rows, :] = (y * jax.nn.sigmoid(y)).astype(_BF16)


def _conv_branch(u, conv_w, conv_b, ln_g, ln_b):
    n_tiles = N_TOK // SEQ
    hb = SEQ // HALO
    last = N_TOK // HALO - 1
    return pl.pallas_call(
        _conv_kernel,
        grid=(n_tiles,),
        in_specs=[
            pl.BlockSpec((HALO, C_CONV), lambda i: (jnp.maximum(i * hb - 1, 0), 0)),
            pl.BlockSpec((SEQ, C_CONV), lambda i: (i, 0)),
            pl.BlockSpec((HALO, C_CONV), lambda i: (jnp.minimum((i + 1) * hb, last), 0)),
            pl.BlockSpec((CONV_WIDTH, C_CONV), lambda i: (0, 0)),
            pl.BlockSpec((1, C_CONV), lambda i: (0, 0)),
            pl.BlockSpec((1, C_CONV), lambda i: (0, 0)),
            pl.BlockSpec((1, C_CONV), lambda i: (0, 0)),
        ],
        out_specs=pl.BlockSpec((SEQ, C_CONV), lambda i: (i, 0)),
        out_shape=jax.ShapeDtypeStruct((N_TOK, C_CONV), _BF16),
        scratch_shapes=[pltpu.VMEM((SEQ + 2 * HALO, C_CONV), _F32)],
        compiler_params=pltpu.CompilerParams(
            dimension_semantics=("arbitrary",), vmem_limit_bytes=VMEM_LIMIT),
        name="conv_branch",
    )(u, u, u, conv_w, conv_b, ln_g, ln_b)


def _mix_kernel(x_ref, mod_ref, att_ref, cv_ref, sg_ref, wa_ref, wc_ref, wo_ref,
                gpost_ref, gffn_ref, wr_ref, br_ref,
                x1_ref, xp_ref, qw_ref, meta_ref):
    a = jnp.dot(att_ref[...], wa_ref[...], preferred_element_type=_F32)
    cv = jnp.dot(cv_ref[...], wc_ref[...], preferred_element_type=_F32)
    m = sg_ref[:, 0:D_MODEL].astype(_F32) * a + sg_ref[:, D_MODEL:].astype(_F32) * cv
    mix = jnp.dot(m.astype(_BF16), wo_ref[...], preferred_element_type=_F32)
    gt1 = mod_ref[:, 2 * D_MODEL:3 * D_MODEL]
    sh2 = mod_ref[:, 3 * D_MODEL:4 * D_MODEL]
    sc2 = mod_ref[:, 4 * D_MODEL:5 * D_MODEL]
    x1 = x_ref[...] + gt1 * _rms(mix, gpost_ref[...])
    x1_ref[...] = x1
    h2 = _rms(x1, gffn_ref[...]) * (1.0 + sc2) + sh2
    h2b = h2.astype(_BF16)

    logits = jnp.dot(h2, wr_ref[...], precision=lax.Precision.HIGHEST,
                     preferred_element_type=_F32) + br_ref[...]
    lane = lax.broadcasted_iota(_I32, (TB, LANE), 1).astype(_F32)
    member = jnp.zeros((TB, LANE), _F32)
    hots, exps = [], []
    top = None
    total = jnp.zeros((TB, 1), _F32)
    for k in range(TOP_K):
        mval = jnp.max(logits, axis=-1, keepdims=True)
        sel = jnp.min(jnp.where(logits == mval, lane, float(LANE)), axis=-1, keepdims=True)
        if top is None:
            top = mval
        e = jnp.exp(mval - top)
        total = total + e
        hot = lane == sel
        hots.append(hot)
        exps.append(e)
        member = member + jnp.where(hot, 1.0, 0.0)
        logits = jnp.where(hot, -jnp.inf, logits)

    r_i = lax.broadcasted_iota(_I32, (TB, TB), 0)
    c_i = lax.broadcasted_iota(_I32, (TB, TB), 1)
    lower = jnp.where(r_i > c_i, 1.0, 0.0).astype(_BF16)
    rank = jnp.dot(lower, member.astype(_BF16), preferred_element_type=_F32)
    count = jnp.sum(member, axis=0, keepdims=True)
    units = jnp.floor((count + float(UNIT - 1)) * (1.0 / UNIT))
    r_l = lax.broadcasted_iota(_I32, (LANE, LANE), 0)
    c_l = lax.broadcasted_iota(_I32, (LANE, LANE), 1)
    upper = jnp.where(r_l < c_l, 1.0, 0.0).astype(_BF16)
    unit_off = jnp.dot(jnp.broadcast_to(units, (SUBLANE, LANE)).astype(_BF16), upper,
                       preferred_element_type=_F32)[0:1, :]
    base = unit_off * float(UNIT) + rank

    slot_lane = lax.broadcasted_iota(_I32, (TB, SLOTS), 1).astype(_F32)
    qw = jnp.zeros((TB, SLOTS), _F32)
    slot_cols = jnp.zeros((TB, LANE), _F32)
    for k in range(TOP_K):
        slot = jnp.sum(jnp.where(hots[k], base, 0.0), axis=-1, keepdims=True)
        qw = qw + jnp.where(slot_lane == slot, exps[k] / total, 0.0)
        slot_cols = jnp.where(lane == float(k), slot, slot_cols)
    qw_ref[...] = qw.astype(_BF16)

    slot_rows = slot_cols.T
    slot_sub = lax.broadcasted_iota(_I32, (SLOTS, TB), 0).astype(_F32)
    perm = jnp.zeros((SLOTS, TB), _F32)
    for k in range(TOP_K):
        perm = perm + jnp.where(slot_sub == slot_rows[k:k + 1, :], 1.0, 0.0)
    perm = perm.astype(_BF16)
    lo = jnp.dot(perm, h2b[:, :PACK], preferred_element_type=_F32)
    hi = jnp.dot(perm, h2b[:, PACK:], preferred_element_type=_F32)
    xp_ref[...] = _pack_halves(lo, hi)

    sub = lax.broadcasted_iota(_I32, (SUBLANE, LANE), 0)
    meta = jnp.where(sub == 0, units, jnp.where(sub == 1, unit_off, 0.0))
    meta_ref[...] = meta.astype(_I32)


def _mix(x, mod3, att, cvn, sg, wa, wc, wo, g_post, g_ffn, w_router_pad, b_router_pad, layer):
    full = lambda shape: pl.BlockSpec(shape, lambda i: (0,) * len(shape))
    return pl.pallas_call(
        _mix_kernel,
        grid=(N_BLOCKS,),
        in_specs=[
            pl.BlockSpec((TB, D_MODEL), lambda i: (i, 0)),
            pl.BlockSpec((None, 1, 6 * D_MODEL),
                         lambda i: (layer * COND_ROWS + _cond_index(i, TB), 0, 0)),
            pl.BlockSpec((TB, Q_DIM), lambda i: (i, 0)),
            pl.BlockSpec((TB, C_CONV), lambda i: (i, 0)),
            pl.BlockSpec((TB, 2 * D_MODEL), lambda i: (i, 0)),
            full((Q_DIM, D_MODEL)),
            full((C_CONV, D_MODEL)),
            full((D_MODEL, D_MODEL)),
            full((1, D_MODEL)),
            full((1, D_MODEL)),
            full((D_MODEL, LANE)),
            full((1, LANE)),
        ],
        out_specs=[
            pl.BlockSpec((TB, D_MODEL), lambda i: (i, 0)),
            pl.BlockSpec((SLOTS, PACK), lambda i: (i, 0)),
            pl.BlockSpec((TB, SLOTS), lambda i: (i, 0)),
            pl.BlockSpec((None, SUBLANE, LANE), lambda i: (i, 0, 0)),
        ],
        out_shape=[
            jax.ShapeDtypeStruct((N_TOK, D_MODEL), _F32),
            jax.ShapeDtypeStruct((N_BLOCKS * SLOTS, PACK), _U32),
            jax.ShapeDtypeStruct((N_TOK, SLOTS), _BF16),
            jax.ShapeDtypeStruct((N_BLOCKS, SUBLANE, LANE), _I32),
        ],
        compiler_params=pltpu.CompilerParams(
            dimension_semantics=("arbitrary",), vmem_limit_bytes=VMEM_LIMIT),
        name="mix_router",
    )(x, mod3, att, cvn, sg, wa, wc, wo, g_post, g_ffn, w_router_pad, b_router_pad)


def _plan(meta):
    units = meta[:, 0, :N_EXPERTS]
    seg_off = meta[:, 1, :N_EXPERTS]
    tiles = (jnp.sum(units, axis=0) + TILE_UNITS - 1) // TILE_UNITS
    tile_end = jnp.cumsum(tiles)
    n_tiles = tile_end[-1]
    region = (tile_end - tiles) * TILE_UNITS
    dst = region[None, :] + jnp.cumsum(units, axis=0) - units
    src = jnp.arange(N_BLOCKS, dtype=_I32)[:, None] * BLOCK_UNITS + seg_off

    tile_id = jnp.arange(N_ETILES, dtype=_I32)
    te = jnp.sum((tile_end[None, :] <= tile_id[:, None]).astype(_I32), axis=1)
    te = jnp.minimum(te, N_EXPERTS - 1)
    te = jnp.where(tile_id < n_tiles, te, te[jnp.maximum(n_tiles - 1, 0)])

    dst_f, len_f, src_f = dst.T.reshape(-1), units.T.reshape(-1), src.T.reshape(-1)
    d = jnp.arange(N_ETILES * TILE_UNITS, dtype=_I32)
    seg = jnp.maximum(jnp.sum((dst_f[None, :] <= d[:, None]).astype(_I32), axis=1) - 1, 0)
    within = d - dst_f[seg]
    src_unit = jnp.where(within < len_f[seg], src_f[seg] + within, 0)

    u = jnp.arange(BLOCK_UNITS, dtype=_I32)
    seg_c = jnp.maximum(
        jnp.sum((seg_off[:, None, :] <= u[None, :, None]).astype(_I32), axis=2) - 1, 0)
    within_c = u[None, :] - jnp.take_along_axis(seg_off, seg_c, axis=1)
    live_c = within_c < jnp.take_along_axis(units, seg_c, axis=1)
    back_unit = jnp.where(live_c, jnp.take_along_axis(dst, seg_c, axis=1) + within_c, 0)
    return (te.astype(_I32), n_tiles.reshape(1).astype(_I32), src_unit.astype(_I32),
            back_unit.reshape(-1).astype(_I32))


def _unit_gather(src_hbm, unit_ref, first, n_units, dst_buf, sem):
    for i in range(n_units):
        row = pl.multiple_of(unit_ref[first + i] * UNIT, UNIT)
        pltpu.make_async_copy(src_hbm.at[pl.ds(row, UNIT), :],
                              dst_buf.at[pl.ds(i * UNIT, UNIT), :], sem).start()


def _unit_gather_wait(src_hbm, n_units, dst_buf, sem):
    pltpu.make_async_copy(src_hbm.at[pl.ds(0, n_units * UNIT), :], dst_buf, sem).wait()


def _expert_kernel(te_ref, nt_ref, src_ref, xp_hbm, wgu_ref, bgu_ref, wd_ref, bd_ref,
                   y_ref, wgu_bf, wd_bf, xbuf, sem):
    j = pl.program_id(0)
    n_live = nt_ref[0]
    slot = j % 2

    @pl.when(j == 0)
    def _():
        _unit_gather(xp_hbm, src_ref, 0, TILE_UNITS, xbuf.at[0], sem.at[0])

    @pl.when(j + 1 < n_live)
    def _():
        _unit_gather(xp_hbm, src_ref, (j + 1) * TILE_UNITS, TILE_UNITS,
                     xbuf.at[1 - slot], sem.at[1 - slot])

    changed = jnp.logical_or(j == 0, te_ref[j] != te_ref[jnp.maximum(j - 1, 0)])

    @pl.when(changed)
    def _():
        wgu_bf[...] = wgu_ref[...].astype(_BF16)
        wd_bf[...] = wd_ref[...].astype(_BF16)

    @pl.when(j < n_live)
    def _():
        _unit_gather_wait(xp_hbm, TILE_UNITS, xbuf.at[slot], sem.at[slot])
        lo, hi = _unpack_halves(xbuf[slot])
        gu = (jnp.dot(lo, wgu_bf[0:PACK, :], preferred_element_type=_F32)
              + jnp.dot(hi, wgu_bf[PACK:, :], preferred_element_type=_F32) + bgu_ref[...])
        gate = jnp.minimum(gu[:, :D_FF], SWIGLU_LIMIT)
        lin = jnp.clip(gu[:, D_FF:], -SWIGLU_LIMIT, SWIGLU_LIMIT)
        act = gate * jax.nn.sigmoid(SWIGLU_ALPHA * gate) * (lin + 1.0)
        y = jnp.dot(act.astype(_BF16), wd_bf[...], preferred_element_type=_F32) + bd_ref[...]
        yr = y.astype(_BF16).astype(_F32)
        y_ref[...] = _pack_halves(yr[:, :PACK], yr[:, PACK:])

    @pl.when(j >= n_live)
    def _():
        y_ref[...] = jnp.zeros_like(y_ref)


def _experts(tile_expert, n_tiles, src_unit, xp, w_gate_up, b_gate_up, w_down, b_down, layer):
    grid_spec = pltpu.PrefetchScalarGridSpec(
        num_scalar_prefetch=3,
        grid=(N_ETILES,),
        in_specs=[
            pl.BlockSpec(memory_space=pl.ANY),
            pl.BlockSpec((None, None, D_MODEL, 2 * D_FF), lambda j, te, nt, su: (layer, te[j], 0, 0)),
            pl.BlockSpec((None, None, 1, 2 * D_FF), lambda j, te, nt, su: (layer, te[j], 0, 0)),
            pl.BlockSpec((None, None, D_FF, D_MODEL), lambda j, te, nt, su: (layer, te[j], 0, 0)),
            pl.BlockSpec((None, None, 1, D_MODEL), lambda j, te, nt, su: (layer, te[j], 0, 0)),
        ],
        out_specs=pl.BlockSpec((TE, PACK), lambda j, te, nt, su: (j, 0)),
        scratch_shapes=[
            pltpu.VMEM((D_MODEL, 2 * D_FF), _BF16),
            pltpu.VMEM((D_FF, D_MODEL), _BF16),
            pltpu.VMEM((2, TE, PACK), _U32),
            pltpu.SemaphoreType.DMA((2,)),
        ],
    )
    return pl.pallas_call(
        _expert_kernel,
        grid_spec=grid_spec,
        out_shape=jax.ShapeDtypeStruct((N_ETILES * TE, PACK), _U32),
        compiler_params=pltpu.CompilerParams(
            dimension_semantics=("arbitrary",), vmem_limit_bytes=VMEM_LIMIT),
        name="experts",
    )(tile_expert, n_tiles, src_unit, xp, w_gate_up,
      b_gate_up.reshape(DEPTH, N_EXPERTS, 1, 2 * D_FF), w_down,
      b_down.reshape(DEPTH, N_EXPERTS, 1, D_MODEL))


def _combine_kernel(back_ref, ys_hbm, x_ref, mod_ref, qw_ref, g_ref, o_ref, ybuf, sem):
    b = pl.program_id(0)
    slot = b % 2

    @pl.when(b == 0)
    def _():
        _unit_gather(ys_hbm, back_ref, 0, BLOCK_UNITS, ybuf.at[0], sem.at[0])

    @pl.when(b + 1 < N_BLOCKS)
    def _():
        _unit_gather(ys_hbm, back_ref, (b + 1) * BLOCK_UNITS, BLOCK_UNITS,
                     ybuf.at[1 - slot], sem.at[1 - slot])

    _unit_gather_wait(ys_hbm, BLOCK_UNITS, ybuf.at[slot], sem.at[slot])
    lo, hi = _unpack_halves(ybuf[slot])
    qw = qw_ref[...]
    m_lo = jnp.dot(qw, lo, preferred_element_type=_F32)
    m_hi = jnp.dot(qw, hi, preferred_element_type=_F32)
    ms = (jnp.sum(m_lo * m_lo, axis=-1, keepdims=True)
          + jnp.sum(m_hi * m_hi, axis=-1, keepdims=True)) * (1.0 / D_MODEL)
    inv = lax.rsqrt(ms + EPS)
    gt2 = mod_ref[:, 5 * D_MODEL:6 * D_MODEL]
    g = g_ref[...]
    o_ref[:, :PACK] = x_ref[:, :PACK] + gt2[:, :PACK] * (m_lo * inv * g[:, :PACK])
    o_ref[:, PACK:] = x_ref[:, PACK:] + gt2[:, PACK:] * (m_hi * inv * g[:, PACK:])


def _combine(back_unit, ys, x1, mod3, qw, g_post, layer):
    grid_spec = pltpu.PrefetchScalarGridSpec(
        num_scalar_prefetch=1,
        grid=(N_BLOCKS,),
        in_specs=[
            pl.BlockSpec(memory_space=pl.ANY),
            pl.BlockSpec((TB, D_MODEL), lambda b, bu: (b, 0)),
            pl.BlockSpec((None, 1, 6 * D_MODEL),
                         lambda b, bu: (layer * COND_ROWS + _cond_index(b, TB), 0, 0)),
            pl.BlockSpec((TB, SLOTS), lambda b, bu: (b, 0)),
            pl.BlockSpec((1, D_MODEL), lambda b, bu: (0, 0)),
        ],
        out_specs=pl.BlockSpec((TB, D_MODEL), lambda b, bu: (b, 0)),
        scratch_shapes=[
            pltpu.VMEM((2, SLOTS, PACK), _U32),
            pltpu.SemaphoreType.DMA((2,)),
        ],
    )
    return pl.pallas_call(
        _combine_kernel,
        grid_spec=grid_spec,
        out_shape=jax.ShapeDtypeStruct((N_TOK, D_MODEL), _F32),
        compiler_params=pltpu.CompilerParams(
            dimension_semantics=("arbitrary",), vmem_limit_bytes=VMEM_LIMIT),
        name="combine_residual",
    )(back_unit, ys, x1, mod3, qw, g_post)


def _rope_tables():
    pos = jnp.arange(DEC_SEQ)
    row = (pos // GRID_W).astype(_F32)
    col = (pos % GRID_W).astype(_F32)
    inv = ROPE_THETA ** (-jnp.arange(ROPE_FREQS, dtype=_F32) / ROPE_FREQS)
    ang_r = row[:, None] * inv[None, :]
    ang_c = col[:, None] * inv[None, :]
    cos = jnp.concatenate([jnp.cos(ang_r)] * 2 + [jnp.cos(ang_c)] * 2, axis=-1)
    sin = jnp.concatenate([-jnp.sin(ang_r), jnp.sin(ang_r), -jnp.sin(ang_c), jnp.sin(ang_c)], axis=-1)
    reps = LANE // HEAD_DIM
    return jnp.tile(cos, (1, reps)), jnp.tile(sin, (1, reps))


def kernel(x_prompt, x_sample, cache_k, cache_v, c, c_ctx, w_ada, b_ada, g_pre_mix, g_post_mix,
           g_pre_ffn, g_post_ffn, w_in, attn_sink, w_attn_o, conv_w, conv_b, conv_ln_g, conv_ln_b,
           w_conv_o, w_out, w_router, b_router, w_gate_up, b_gate_up, w_down, b_down):
    x = jnp.concatenate([x_prompt.reshape(N_CTX, D_MODEL), x_sample.reshape(N_LAT, D_MODEL)], axis=0)
    cond = jnp.concatenate([c_ctx[None, :], c, jnp.zeros((COND_ROWS - N_COND, D_MODEL), _F32)], axis=0)
    mod = _modulation(cond, w_ada, b_ada)
    mod3 = mod.reshape(DEPTH * COND_ROWS, 1, 6 * D_MODEL)
    cos_t, sin_t = _rope_tables()
    ck = cache_k.reshape(DEC_BATCH, DEPTH, PAST_LEN, KV_DIM)
    cv = cache_v.reshape(DEC_BATCH, DEPTH, PAST_LEN, KV_DIM)
    w_router_pad = jnp.pad(w_router, ((0, 0), (0, 0), (0, LANE - N_EXPERTS)))
    b_router_pad = jnp.pad(b_router, ((0, 0), (0, LANE - N_EXPERTS)), constant_values=-jnp.inf)

    new_k, new_v = [], []
    for l in range(DEPTH):
        row = lambda a: a[l][None, :]
        q, kv, u, sg = _inproj(x, mod3, row(g_pre_mix), w_in[l].astype(_BF16), cos_t, sin_t, l)
        new_k.append(kv[:N_CTX, :KV_DIM].reshape(BATCH, SEQ, N_KV_HEADS, HEAD_DIM))
        new_v.append(kv[:N_CTX, KV_DIM:].reshape(BATCH, SEQ, N_KV_HEADS, HEAD_DIM))
        att = _attention(q, kv, ck, cv, attn_sink[l], l)
        cvn = _conv_branch(u, conv_w[l], row(conv_b), row(conv_ln_g), row(conv_ln_b))
        x1, xp, qw, meta = _mix(
            x, mod3, att, cvn, sg, w_attn_o[l].astype(_BF16), w_conv_o[l].astype(_BF16),
            w_out[l].astype(_BF16), row(g_post_mix), row(g_pre_ffn), w_router_pad[l],
            row(b_router_pad), l)
        te, n_tiles, src_unit, back_unit = _plan(meta)
        ys = _experts(te, n_tiles, src_unit, xp, w_gate_up, b_gate_up, w_down, b_down, l)
        x = _combine(back_unit, ys, x1, mod3, qw, row(g_post_ffn), l)

    y_prompt = x[:N_CTX].reshape(BATCH, SEQ, D_MODEL)
    y_sample = x[N_CTX:].reshape(DEC_BATCH, DEC_SEQ, D_MODEL)
    return (y_prompt, y_sample, jnp.stack(new_k, axis=1), jnp.stack(new_v, axis=1))
```

```python
import jax
import jax.numpy as jnp
from jax import lax
from jax.experimental import pallas as pl
from jax.experimental.pallas import tpu as pltpu

D_MODEL = 1024
BATCH = 16
SEQ = 256
DEPTH = 2
DEC_BATCH = 2
DEC_SEQ = 2048
PAST_LEN = 256
GRID_W = 64
N_HEADS = 16
N_KV_HEADS = 4
GROUP = N_HEADS // N_KV_HEADS
HEAD_DIM = 64
Q_DIM = N_HEADS * HEAD_DIM
KV_DIM = N_KV_HEADS * HEAD_DIM
WINDOW = 128
ATTN_SCALE = HEAD_DIM ** -0.5
ROPE_THETA = 10000.0
ROPE_HALF = HEAD_DIM // 2
ROPE_FREQS = ROPE_HALF // 2
C_CONV = D_MODEL // 2
CONV_WIDTH = 31
CONV_PAD = (CONV_WIDTH - 1) // 2
N_EXPERTS = 32
TOP_K = 4
D_FF = D_MODEL
SWIGLU_LIMIT = 7.0
SWIGLU_ALPHA = 1.702
EPS = 1e-6
IN_COLS = Q_DIM + 2 * KV_DIM + 2 * C_CONV + 2 * D_MODEL

N_CTX = BATCH * SEQ
N_LAT = DEC_BATCH * DEC_SEQ
N_TOK = N_CTX + N_LAT
N_COND = 1 + DEC_BATCH
COND_ROWS = 8

LANE = 128
SUBLANE = 8
TM = 512
TQ = 256
KWIN = TQ + 2 * WINDOW
HALO = 16

TB = 256
N_BLOCKS = N_TOK // TB
UNIT = SUBLANE
SLOTS = 1280
BLOCK_UNITS = SLOTS // UNIT
TE = 256
TILE_UNITS = TE // UNIT
N_ETILES = (N_BLOCKS * BLOCK_UNITS) // TILE_UNITS + N_EXPERTS
VMEM_LIMIT = 56 * 1024 * 1024

assert SLOTS >= TB * TOP_K + N_EXPERTS * (UNIT - 1) and SLOTS % UNIT == 0

_F32 = jnp.float32
_BF16 = jnp.bfloat16
_I32 = jnp.int32


def _rms(x, g):
    return x * lax.rsqrt(jnp.mean(x * x, axis=-1, keepdims=True) + EPS) * g


def _cond_index(i, tile):
    n_ctx_tiles = N_CTX // tile
    return jnp.where(i < n_ctx_tiles, 0, 1 + (i - n_ctx_tiles) // (DEC_SEQ // tile))


def _mod_kernel(cond_ref, w_ref, b_ref, out_ref):
    cnd = cond_ref[...]
    s = cnd * jax.nn.sigmoid(cnd)
    out_ref[...] = jnp.dot(s, w_ref[...], precision=lax.Precision.HIGHEST,
                           preferred_element_type=_F32) + b_ref[...]


def _modulation(cond, w_ada, b_ada):
    tn = 1536
    nt = 6 * D_MODEL // tn
    return pl.pallas_call(
        _mod_kernel,
        grid=(DEPTH, nt),
        in_specs=[
            pl.BlockSpec((COND_ROWS, D_MODEL), lambda l, n: (0, 0)),
            pl.BlockSpec((None, D_MODEL, tn), lambda l, n: (l, 0, n)),
            pl.BlockSpec((None, 1, tn), lambda l, n: (l, 0, n)),
        ],
        out_specs=pl.BlockSpec((None, COND_ROWS, tn), lambda l, n: (l, 0, n)),
        out_shape=jax.ShapeDtypeStruct((DEPTH, COND_ROWS, 6 * D_MODEL), _F32),
        compiler_params=pltpu.CompilerParams(
            dimension_semantics=("arbitrary", "arbitrary"), vmem_limit_bytes=VMEM_LIMIT),
        name="modulation",
    )(cond, w_ada, b_ada.reshape(DEPTH, 1, 6 * D_MODEL))


def _rope_chunk(x, cos, sin):
    lane = lax.broadcasted_iota(_I32, x.shape, 1)
    partner = jnp.where((lane & ROPE_FREQS) == 0,
                        pltpu.roll(x, LANE - ROPE_FREQS, 1), pltpu.roll(x, ROPE_FREQS, 1))
    return x * cos + partner * sin


def _inproj_kernel(x_ref, mod_ref, g_ref, w_ref, cos_ref, sin_ref,
                   q_ref, kv_ref, u_ref, sg_ref):
    i = pl.program_id(0)
    x = x_ref[...]
    sh = mod_ref[:, 0:D_MODEL]
    sc = mod_ref[:, D_MODEL:2 * D_MODEL]
    h = (_rms(x, g_ref[...]) * (1.0 + sc) + sh).astype(_BF16)

    c0 = 0
    q = jnp.dot(h, w_ref[:, c0:c0 + Q_DIM], preferred_element_type=_F32)
    c0 += Q_DIM
    kv = jnp.dot(h, w_ref[:, c0:c0 + 2 * KV_DIM], preferred_element_type=_F32)
    c0 += 2 * KV_DIM
    ua = jnp.dot(h, w_ref[:, c0:c0 + C_CONV], preferred_element_type=_F32)
    c0 += C_CONV
    ub = jnp.dot(h, w_ref[:, c0:c0 + C_CONV], preferred_element_type=_F32)
    c0 += C_CONV
    g = jnp.dot(h, w_ref[:, c0:c0 + 2 * D_MODEL], preferred_element_type=_F32)

    u_ref[...] = ua * jax.nn.sigmoid(ub)
    sg_ref[...] = jax.nn.sigmoid(g).astype(_BF16)

    is_latent = i >= N_CTX // TM

    @pl.when(jnp.logical_not(is_latent))
    def _():
        q_ref[...] = q.astype(_BF16)
        kv_ref[...] = kv

    @pl.when(is_latent)
    def _():
        cos = cos_ref[...]
        sin = sin_ref[...]
        for j in range(Q_DIM // LANE):
            sl = slice(j * LANE, (j + 1) * LANE)
            q_ref[:, sl] = _rope_chunk(q[:, sl], cos, sin).astype(_BF16)
        for j in range(KV_DIM // LANE):
            sl = slice(j * LANE, (j + 1) * LANE)
            kv_ref[:, sl] = _rope_chunk(kv[:, sl], cos, sin)
        kv_ref[:, KV_DIM:] = kv[:, KV_DIM:]


def _inproj(x, mod3, g_pre, w_in_bf, cos_t, sin_t, layer):
    n_ctx_tiles = N_CTX // TM
    lat_tiles = DEC_SEQ // TM

    def rope_map(i):
        return (jnp.where(i < n_ctx_tiles, 0, (i - n_ctx_tiles) % lat_tiles), 0)

    return pl.pallas_call(
        _inproj_kernel,
        grid=(N_TOK // TM,),
        in_specs=[
            pl.BlockSpec((TM, D_MODEL), lambda i: (i, 0)),
            pl.BlockSpec((None, 1, 6 * D_MODEL),
                         lambda i: (layer * COND_ROWS + _cond_index(i, TM), 0, 0)),
            pl.BlockSpec((1, D_MODEL), lambda i: (0, 0)),
            pl.BlockSpec((D_MODEL, IN_COLS), lambda i: (0, 0)),
            pl.BlockSpec((TM, LANE), rope_map),
            pl.BlockSpec((TM, LANE), rope_map),
        ],
        out_specs=[
            pl.BlockSpec((TM, Q_DIM), lambda i: (i, 0)),
            pl.BlockSpec((TM, 2 * KV_DIM), lambda i: (i, 0)),
            pl.BlockSpec((TM, C_CONV), lambda i: (i, 0)),
            pl.BlockSpec((TM, 2 * D_MODEL), lambda i: (i, 0)),
        ],
        out_shape=[
            jax.ShapeDtypeStruct((N_TOK, Q_DIM), _BF16),
            jax.ShapeDtypeStruct((N_TOK, 2 * KV_DIM), _F32),
            jax.ShapeDtypeStruct((N_TOK, C_CONV), _F32),
            jax.ShapeDtypeStruct((N_TOK, 2 * D_MODEL), _BF16),
        ],
        compiler_params=pltpu.CompilerParams(
            dimension_semantics=("arbitrary",), vmem_limit_bytes=VMEM_LIMIT),
        name="inproj",
    )(x, mod3, g_pre, w_in_bf, cos_t, sin_t)


def _attend_head(q, keys, vals, masks, sink):
    scores = []
    for k, mask in zip(keys, masks):
        s = lax.dot_general(q, k, (((1,), (1,)), ((), ())),
                            preferred_element_type=_F32) * ATTN_SCALE
        if mask is not None:
            s = jnp.where(mask, s, -jnp.inf)
        scores.append(s)
    m = jnp.full((q.shape[0], 1), sink, _F32)
    for s in scores:
        m = jnp.maximum(m, jnp.max(s, axis=-1, keepdims=True))
    denom = jnp.exp(sink - m)
    out = jnp.zeros((q.shape[0], HEAD_DIM), _F32)
    for s, v in zip(scores, vals):
        p = jnp.exp(s - m)
        denom = denom + jnp.sum(p, axis=-1, keepdims=True)
        out = out + jnp.dot(p.astype(_BF16), v, preferred_element_type=_F32)
    return out / denom


def _attn_kernel(sink_ref, q_ref, kv_own_ref, kv_seq_ref, ck_ref, cv_ref, o_ref):
    i = pl.program_id(0)
    n_ctx_steps = N_CTX // TQ

    @pl.when(i < n_ctx_steps)
    def _():
        for hk in range(N_KV_HEADS):
            k = kv_own_ref[:, hk * HEAD_DIM:(hk + 1) * HEAD_DIM].astype(_BF16)
            v = kv_own_ref[:, KV_DIM + hk * HEAD_DIM:KV_DIM + (hk + 1) * HEAD_DIM].astype(_BF16)
            for gi in range(GROUP):
                n = hk * GROUP + gi
                sl = slice(n * HEAD_DIM, (n + 1) * HEAD_DIM)
                o = _attend_head(q_ref[:, sl], [k], [v], [None], sink_ref[n])
                o_ref[:, sl] = o.astype(_BF16)

    @pl.when(i >= n_ctx_steps)
    def _():
        qb = (i - n_ctx_steps) % (DEC_SEQ // TQ)
        q_start = qb * TQ
        k_start = pl.multiple_of(jnp.clip(q_start - WINDOW, 0, DEC_SEQ - KWIN), WINDOW)
        qpos = q_start + lax.broadcasted_iota(_I32, (TQ, KWIN), 0)
        kpos = k_start + lax.broadcasted_iota(_I32, (TQ, KWIN), 1)
        mask = jnp.abs(kpos - qpos) <= WINDOW
        for hk in range(N_KV_HEADS):
            ks = slice(hk * HEAD_DIM, (hk + 1) * HEAD_DIM)
            vs = slice(KV_DIM + hk * HEAD_DIM, KV_DIM + (hk + 1) * HEAD_DIM)
            k = kv_seq_ref[pl.ds(k_start, KWIN), ks].astype(_BF16)
            v = kv_seq_ref[pl.ds(k_start, KWIN), vs].astype(_BF16)
            kc = ck_ref[:, ks].astype(_BF16)
            vc = cv_ref[:, ks].astype(_BF16)
            for gi in range(GROUP):
                n = hk * GROUP + gi
                sl = slice(n * HEAD_DIM, (n + 1) * HEAD_DIM)
                o = _attend_head(q_ref[:, sl], [k, kc], [v, vc], [mask, None], sink_ref[n])
                o_ref[:, sl] = o.astype(_BF16)


def _attention(q, kv, cache_k, cache_v, sink, layer):
    n_ctx_steps = N_CTX // TQ
    nq = DEC_SEQ // TQ
    kv_off = N_CTX // DEC_SEQ

    def lat_batch(i):
        return jnp.maximum(i - n_ctx_steps, 0) // nq

    return pl.pallas_call(
        _attn_kernel,
        grid=(N_TOK // TQ,),
        in_specs=[
            pl.BlockSpec(memory_space=pltpu.SMEM),
            pl.BlockSpec((TQ, Q_DIM), lambda i: (i, 0)),
            pl.BlockSpec((TQ, 2 * KV_DIM), lambda i: (i, 0)),
            pl.BlockSpec((DEC_SEQ, 2 * KV_DIM), lambda i: (kv_off + lat_batch(i), 0)),
            pl.BlockSpec((None, None, PAST_LEN, KV_DIM), lambda i: (lat_batch(i), layer, 0, 0)),
            pl.BlockSpec((None, None, PAST_LEN, KV_DIM), lambda i: (lat_batch(i), layer, 0, 0)),
        ],
        out_specs=pl.BlockSpec((TQ, Q_DIM), lambda i: (i, 0)),
        out_shape=jax.ShapeDtypeStruct((N_TOK, Q_DIM), _BF16),
        compiler_params=pltpu.CompilerParams(
            dimension_semantics=("arbitrary",), vmem_limit_bytes=VMEM_LIMIT),
        name="attention",
    )(sink, q, kv, kv, cache_k, cache_v)


def _conv_kernel(prev_ref, cur_ref, next_ref, w_ref, b_ref, lg_ref, lb_ref, y_ref, pad_ref):
    i = pl.program_id(0)
    n_ctx_tiles = N_CTX // SEQ
    tiles_per_seq = jnp.where(i < n_ctx_tiles, 1, DEC_SEQ // SEQ)
    j = jnp.where(i < n_ctx_tiles, 0, (i - n_ctx_tiles) % (DEC_SEQ // SEQ))
    pad_ref[0:HALO, :] = jnp.where(j > 0, prev_ref[...], 0.0)
    pad_ref[HALO:HALO + SEQ, :] = cur_ref[...]
    pad_ref[HALO + SEQ:HALO + SEQ + HALO, :] = jnp.where(j < tiles_per_seq - 1, next_ref[...], 0.0)

    rows = 64
    for r0 in range(0, SEQ, rows):
        acc = jnp.zeros((rows, C_CONV), _F32) + b_ref[...]
        for t in range(CONV_WIDTH):
            start = HALO - CONV_PAD + r0 + t
            acc = acc + pad_ref[start:start + rows, :] * w_ref[t:t + 1, :]
        mu = jnp.mean(acc, axis=-1, keepdims=True)
        d = acc - mu
        var = jnp.mean(d * d, axis=-1, keepdims=True)
        y = d * lax.rsqrt(var + EPS) * lg_ref[...] + lb_ref[...]
        y_ref[r0:r0 + rows, :] = (y * jax.nn.sigmoid(y)).astype(_BF16)


def _conv_branch(u, conv_w, conv_b, ln_g, ln_b):
    n_tiles = N_TOK // SEQ
    hb = SEQ // HALO
    last = N_TOK // HALO - 1
    return pl.pallas_call(
        _conv_kernel,
        grid=(n_tiles,),
        in_specs=[
            pl.BlockSpec((HALO, C_CONV), lambda i: (jnp.maximum(i * hb - 1, 0), 0)),
            pl.BlockSpec((SEQ, C_CONV), lambda i: (i, 0)),
            pl.BlockSpec((HALO, C_CONV), lambda i: (jnp.minimum((i + 1) * hb, last), 0)),
            pl.BlockSpec((CONV_WIDTH, C_CONV), lambda i: (0, 0)),
            pl.BlockSpec((1, C_CONV), lambda i: (0, 0)),
            pl.BlockSpec((1, C_CONV), lambda i: (0, 0)),
            pl.BlockSpec((1, C_CONV), lambda i: (0, 0)),
        ],
        out_specs=pl.BlockSpec((SEQ, C_CONV), lambda i: (i, 0)),
        out_shape=jax.ShapeDtypeStruct((N_TOK, C_CONV), _BF16),
        scratch_shapes=[pltpu.VMEM((SEQ + 2 * HALO, C_CONV), _F32)],
        compiler_params=pltpu.CompilerParams(
            dimension_semantics=("arbitrary",), vmem_limit_bytes=VMEM_LIMIT),
        name="conv_branch",
    )(u, u, u, conv_w, conv_b, ln_g, ln_b)


def _mix_kernel(x_ref, mod_ref, att_ref, cv_ref, sg_ref, wa_ref, wc_ref, wo_ref,
                gpost_ref, gffn_ref, wr_ref, br_ref,
                x1_ref, xp_ref, qw_ref, meta_ref):
    a = jnp.dot(att_ref[...], wa_ref[...], preferred_element_type=_F32)
    cv = jnp.dot(cv_ref[...], wc_ref[...], preferred_element_type=_F32)
    m = sg_ref[:, 0:D_MODEL].astype(_F32) * a + sg_ref[:, D_MODEL:].astype(_F32) * cv
    mix = jnp.dot(m.astype(_BF16), wo_ref[...], preferred_element_type=_F32)
    gt1 = mod_ref[:, 2 * D_MODEL:3 * D_MODEL]
    sh2 = mod_ref[:, 3 * D_MODEL:4 * D_MODEL]
    sc2 = mod_ref[:, 4 * D_MODEL:5 * D_MODEL]
    x1 = x_ref[...] + gt1 * _rms(mix, gpost_ref[...])
    x1_ref[...] = x1
    h2 = _rms(x1, gffn_ref[...]) * (1.0 + sc2) + sh2
    h2b = h2.astype(_BF16)

    logits = jnp.dot(h2, wr_ref[...], precision=lax.Precision.HIGHEST,
                     preferred_element_type=_F32) + br_ref[...]
    lane = lax.broadcasted_iota(_I32, (TB, LANE), 1).astype(_F32)
    member = jnp.zeros((TB, LANE), _F32)
    hots, exps = [], []
    top = None
    total = jnp.zeros((TB, 1), _F32)
    for k in range(TOP_K):
        mval = jnp.max(logits, axis=-1, keepdims=True)
        sel = jnp.min(jnp.where(logits == mval, lane, float(LANE)), axis=-1, keepdims=True)
        if top is None:
            top = mval
        e = jnp.exp(mval - top)
        total = total + e
        hot = lane == sel
        hots.append(hot)
        exps.append(e)
        member = member + jnp.where(hot, 1.0, 0.0)
        logits = jnp.where(hot, -jnp.inf, logits)

    r_i = lax.broadcasted_iota(_I32, (TB, TB), 0)
    c_i = lax.broadcasted_iota(_I32, (TB, TB), 1)
    lower = jnp.where(r_i > c_i, 1.0, 0.0).astype(_BF16)
    rank = jnp.dot(lower, member.astype(_BF16), preferred_element_type=_F32)
    count = jnp.sum(member, axis=0, keepdims=True)
    units = jnp.floor((count + float(UNIT - 1)) * (1.0 / UNIT))
    r_l = lax.broadcasted_iota(_I32, (LANE, LANE), 0)
    c_l = lax.broadcasted_iota(_I32, (LANE, LANE), 1)
    upper = jnp.where(r_l < c_l, 1.0, 0.0).astype(_BF16)
    unit_off = jnp.dot(jnp.broadcast_to(units, (SUBLANE, LANE)).astype(_BF16), upper,
                       preferred_element_type=_F32)[0:1, :]
    base = unit_off * float(UNIT) + rank

    slot_lane = lax.broadcasted_iota(_I32, (TB, SLOTS), 1).astype(_F32)
    qw = jnp.zeros((TB, SLOTS), _F32)
    slot_cols = jnp.zeros((TB, LANE), _F32)
    for k in range(TOP_K):
        slot = jnp.sum(jnp.where(hots[k], base, 0.0), axis=-1, keepdims=True)
        qw = qw + jnp.where(slot_lane == slot, exps[k] / total, 0.0)
        slot_cols = jnp.where(lane == float(k), slot, slot_cols)
    qw_ref[...] = qw.astype(_BF16)

    slot_rows = slot_cols.T
    slot_sub = lax.broadcasted_iota(_I32, (SLOTS, TB), 0).astype(_F32)
    perm = jnp.zeros((SLOTS, TB), _F32)
    for k in range(TOP_K):
        perm = perm + jnp.where(slot_sub == slot_rows[k:k + 1, :], 1.0, 0.0)
    perm = perm.astype(_BF16)
    xp_ref[...] = jnp.dot(perm, h2b, preferred_element_type=_F32)

    sub = lax.broadcasted_iota(_I32, (SUBLANE, LANE), 0)
    meta = jnp.where(sub == 0, units, jnp.where(sub == 1, unit_off, 0.0))
    meta_ref[...] = meta.astype(_I32)


def _mix(x, mod3, att, cvn, sg, wa, wc, wo, g_post, g_ffn, w_router_pad, b_router_pad, layer):
    full = lambda shape: pl.BlockSpec(shape, lambda i: (0,) * len(shape))
    return pl.pallas_call(
        _mix_kernel,
        grid=(N_BLOCKS,),
        in_specs=[
            pl.BlockSpec((TB, D_MODEL), lambda i: (i, 0)),
            pl.BlockSpec((None, 1, 6 * D_MODEL),
                         lambda i: (layer * COND_ROWS + _cond_index(i, TB), 0, 0)),
            pl.BlockSpec((TB, Q_DIM), lambda i: (i, 0)),
            pl.BlockSpec((TB, C_CONV), lambda i: (i, 0)),
            pl.BlockSpec((TB, 2 * D_MODEL), lambda i: (i, 0)),
            full((Q_DIM, D_MODEL)),
            full((C_CONV, D_MODEL)),
            full((D_MODEL, D_MODEL)),
            full((1, D_MODEL)),
            full((1, D_MODEL)),
            full((D_MODEL, LANE)),
            full((1, LANE)),
        ],
        out_specs=[
            pl.BlockSpec((TB, D_MODEL), lambda i: (i, 0)),
            pl.BlockSpec((SLOTS, D_MODEL), lambda i: (i, 0)),
            pl.BlockSpec((TB, SLOTS), lambda i: (i, 0)),
            pl.BlockSpec((None, SUBLANE, LANE), lambda i: (i, 0, 0)),
        ],
        out_shape=[
            jax.ShapeDtypeStruct((N_TOK, D_MODEL), _F32),
            jax.ShapeDtypeStruct((N_BLOCKS * SLOTS, D_MODEL), _F32),
            jax.ShapeDtypeStruct((N_TOK, SLOTS), _BF16),
            jax.ShapeDtypeStruct((N_BLOCKS, SUBLANE, LANE), _I32),
        ],
        compiler_params=pltpu.CompilerParams(
            dimension_semantics=("arbitrary",), vmem_limit_bytes=VMEM_LIMIT),
        name="mix_router",
    )(x, mod3, att, cvn, sg, wa, wc, wo, g_post, g_ffn, w_router_pad, b_router_pad)


def _plan(meta):
    units = meta[:, 0, :N_EXPERTS]
    seg_off = meta[:, 1, :N_EXPERTS]
    tiles = (jnp.sum(units, axis=0) + TILE_UNITS - 1) // TILE_UNITS
    tile_end = jnp.cumsum(tiles)
    n_tiles = tile_end[-1]
    region = (tile_end - tiles) * TILE_UNITS
    dst = region[None, :] + jnp.cumsum(units, axis=0) - units
    src = jnp.arange(N_BLOCKS, dtype=_I32)[:, None] * BLOCK_UNITS + seg_off

    tile_id = jnp.arange(N_ETILES, dtype=_I32)
    te = jnp.sum((tile_end[None, :] <= tile_id[:, None]).astype(_I32), axis=1)
    last_expert = jnp.max(jnp.where(tiles > 0, jnp.arange(N_EXPERTS, dtype=_I32), 0))
    te = jnp.where(tile_id < n_tiles, te, last_expert)

    dst_f, len_f, src_f = dst.reshape(1, -1), units.reshape(1, -1), src.reshape(1, -1)
    d = jnp.arange(N_ETILES * TILE_UNITS, dtype=_I32)[:, None]
    in_seg = (dst_f <= d) & (d < dst_f + len_f)
    src_unit = jnp.sum(jnp.where(in_seg, src_f + d - dst_f, 0), axis=1)

    u = jnp.arange(BLOCK_UNITS, dtype=_I32)[None, :, None]
    so, un = seg_off[:, None, :], units[:, None, :]
    in_blk = (so <= u) & (u < so + un)
    back_unit = jnp.sum(jnp.where(in_blk, dst[:, None, :] + u - so, 0), axis=2)
    return (te.astype(_I32), n_tiles.reshape(1).astype(_I32), src_unit.astype(_I32),
            back_unit.reshape(-1).astype(_I32))


def _unit_gather(src_hbm, unit_ref, first, n_units, dst_buf, sem):
    for i in range(n_units):
        row = pl.multiple_of(unit_ref[first + i] * UNIT, UNIT)
        pltpu.make_async_copy(src_hbm.at[pl.ds(row, UNIT), :],
                              dst_buf.at[pl.ds(i * UNIT, UNIT), :], sem).start()


def _unit_gather_wait(src_hbm, n_units, dst_buf, sem):
    pltpu.make_async_copy(src_hbm.at[pl.ds(0, n_units * UNIT), :], dst_buf, sem).wait()


def _expert_kernel(te_ref, nt_ref, src_ref, xp_hbm, wgu_ref, bgu_ref, wd_ref, bd_ref,
                   y_ref, wgu_bf, wd_bf, xbuf, sem):
    j = pl.program_id(0)
    n_live = nt_ref[0]
    slot = j % 2

    @pl.when(j == 0)
    def _():
        _unit_gather(xp_hbm, src_ref, 0, TILE_UNITS, xbuf.at[0], sem.at[0])

    @pl.when(j + 1 < n_live)
    def _():
        _unit_gather(xp_hbm, src_ref, (j + 1) * TILE_UNITS, TILE_UNITS,
                     xbuf.at[1 - slot], sem.at[1 - slot])

    changed = jnp.logical_or(j == 0, te_ref[j] != te_ref[jnp.maximum(j - 1, 0)])

    @pl.when(changed)
    def _():
        wgu_bf[...] = wgu_ref[...].astype(_BF16)
        wd_bf[...] = wd_ref[...].astype(_BF16)

    @pl.when(j < n_live)
    def _():
        _unit_gather_wait(xp_hbm, TILE_UNITS, xbuf.at[slot], sem.at[slot])
        gu = jnp.dot(xbuf[slot].astype(_BF16), wgu_bf[...],
                     preferred_element_type=_F32) + bgu_ref[...]
        gate = jnp.minimum(gu[:, :D_FF], SWIGLU_LIMIT)
        lin = jnp.clip(gu[:, D_FF:], -SWIGLU_LIMIT, SWIGLU_LIMIT)
        act = gate * jax.nn.sigmoid(SWIGLU_ALPHA * gate) * (lin + 1.0)
        y_ref[...] = jnp.dot(act.astype(_BF16), wd_bf[...],
                             preferred_element_type=_F32) + bd_ref[...]

    @pl.when(j >= n_live)
    def _():
        y_ref[...] = jnp.zeros_like(y_ref)


def _experts(tile_expert, n_tiles, src_unit, xp, w_gate_up, b_gate_up, w_down, b_down, layer):
    grid_spec = pltpu.PrefetchScalarGridSpec(
        num_scalar_prefetch=3,
        grid=(N_ETILES,),
        in_specs=[
            pl.BlockSpec(memory_space=pl.ANY),
            pl.BlockSpec((None, None, D_MODEL, 2 * D_FF), lambda j, te, nt, su: (layer, te[j], 0, 0)),
            pl.BlockSpec((None, None, 1, 2 * D_FF), lambda j, te, nt, su: (layer, te[j], 0, 0)),
            pl.BlockSpec((None, None, D_FF, D_MODEL), lambda j, te, nt, su: (layer, te[j], 0, 0)),
            pl.BlockSpec((None, None, 1, D_MODEL), lambda j, te, nt, su: (layer, te[j], 0, 0)),
        ],
        out_specs=pl.BlockSpec((TE, D_MODEL), lambda j, te, nt, su: (j, 0)),
        scratch_shapes=[
            pltpu.VMEM((D_MODEL, 2 * D_FF), _BF16),
            pltpu.VMEM((D_FF, D_MODEL), _BF16),
            pltpu.VMEM((2, TE, D_MODEL), _F32),
            pltpu.SemaphoreType.DMA((2,)),
        ],
    )
    return pl.pallas_call(
        _expert_kernel,
        grid_spec=grid_spec,
        out_shape=jax.ShapeDtypeStruct((N_ETILES * TE, D_MODEL), _F32),
        compiler_params=pltpu.CompilerParams(
            dimension_semantics=("arbitrary",), vmem_limit_bytes=VMEM_LIMIT),
        name="experts",
    )(tile_expert, n_tiles, src_unit, xp, w_gate_up,
      b_gate_up.reshape(DEPTH, N_EXPERTS, 1, 2 * D_FF), w_down,
      b_down.reshape(DEPTH, N_EXPERTS, 1, D_MODEL))


def _combine_kernel(back_ref, ys_hbm, x_ref, mod_ref, qw_ref, g_ref, o_ref, ybuf, sem):
    b = pl.program_id(0)
    slot = b % 2

    @pl.when(b == 0)
    def _():
        _unit_gather(ys_hbm, back_ref, 0, BLOCK_UNITS, ybuf.at[0], sem.at[0])

    @pl.when(b + 1 < N_BLOCKS)
    def _():
        _unit_gather(ys_hbm, back_ref, (b + 1) * BLOCK_UNITS, BLOCK_UNITS,
                     ybuf.at[1 - slot], sem.at[1 - slot])

    _unit_gather_wait(ys_hbm, BLOCK_UNITS, ybuf.at[slot], sem.at[slot])
    moe = jnp.dot(qw_ref[...], ybuf[slot].astype(_BF16), preferred_element_type=_F32)
    gt2 = mod_ref[:, 5 * D_MODEL:6 * D_MODEL]
    o_ref[...] = x_ref[...] + gt2 * _rms(moe, g_ref[...])


def _combine(back_unit, ys, x1, mod3, qw, g_post, layer):
    grid_spec = pltpu.PrefetchScalarGridSpec(
        num_scalar_prefetch=1,
        grid=(N_BLOCKS,),
        in_specs=[
            pl.BlockSpec(memory_space=pl.ANY),
            pl.BlockSpec((TB, D_MODEL), lambda b, bu: (b, 0)),
            pl.BlockSpec((None, 1, 6 * D_MODEL),
                         lambda b, bu: (layer * COND_ROWS + _cond_index(b, TB), 0, 0)),
            pl.BlockSpec((TB, SLOTS), lambda b, bu: (b, 0)),
            pl.BlockSpec((1, D_MODEL), lambda b, bu: (0, 0)),
        ],
        out_specs=pl.BlockSpec((TB, D_MODEL), lambda b, bu: (b, 0)),
        scratch_shapes=[
            pltpu.VMEM((2, SLOTS, D_MODEL), _F32),
            pltpu.SemaphoreType.DMA((2,)),
        ],
    )
    return pl.pallas_call(
        _combine_kernel,
        grid_spec=grid_spec,
        out_shape=jax.ShapeDtypeStruct((N_TOK, D_MODEL), _F32),
        compiler_params=pltpu.CompilerParams(
            dimension_semantics=("arbitrary",), vmem_limit_bytes=VMEM_LIMIT),
        name="combine_residual",
    )(back_unit, ys, x1, mod3, qw, g_post)


def _rope_tables():
    pos = jnp.arange(DEC_SEQ)
    row = (pos // GRID_W).astype(_F32)
    col = (pos % GRID_W).astype(_F32)
    inv = ROPE_THETA ** (-jnp.arange(ROPE_FREQS, dtype=_F32) / ROPE_FREQS)
    ang_r = row[:, None] * inv[None, :]
    ang_c = col[:, None] * inv[None, :]
    cos = jnp.concatenate([jnp.cos(ang_r)] * 2 + [jnp.cos(ang_c)] * 2, axis=-1)
    sin = jnp.concatenate([-jnp.sin(ang_r), jnp.sin(ang_r), -jnp.sin(ang_c), jnp.sin(ang_c)], axis=-1)
    reps = LANE // HEAD_DIM
    return jnp.tile(cos, (1, reps)), jnp.tile(sin, (1, reps))


def kernel(x_prompt, x_sample, cache_k, cache_v, c, c_ctx, w_ada, b_ada, g_pre_mix, g_post_mix,
           g_pre_ffn, g_post_ffn, w_in, attn_sink, w_attn_o, conv_w, conv_b, conv_ln_g, conv_ln_b,
           w_conv_o, w_out, w_router, b_router, w_gate_up, b_gate_up, w_down, b_down):
    x = jnp.concatenate([x_prompt.reshape(N_CTX, D_MODEL), x_sample.reshape(N_LAT, D_MODEL)], axis=0)
    cond = jnp.concatenate([c_ctx[None, :], c, jnp.zeros((COND_ROWS - N_COND, D_MODEL), _F32)], axis=0)
    mod = _modulation(cond, w_ada, b_ada)
    mod3 = mod.reshape(DEPTH * COND_ROWS, 1, 6 * D_MODEL)
    cos_t, sin_t = _rope_tables()
    ck = cache_k.reshape(DEC_BATCH, DEPTH, PAST_LEN, KV_DIM)
    cv = cache_v.reshape(DEC_BATCH, DEPTH, PAST_LEN, KV_DIM)
    w_router_pad = jnp.pad(w_router, ((0, 0), (0, 0), (0, LANE - N_EXPERTS)))
    b_router_pad = jnp.pad(b_router, ((0, 0), (0, LANE - N_EXPERTS)), constant_values=-jnp.inf)

    new_k, new_v = [], []
    for l in range(DEPTH):
        row = lambda a: a[l][None, :]
        q, kv, u, sg = _inproj(x, mod3, row(g_pre_mix), w_in[l].astype(_BF16), cos_t, sin_t, l)
        new_k.append(kv[:N_CTX, :KV_DIM].reshape(BATCH, SEQ, N_KV_HEADS, HEAD_DIM))
        new_v.append(kv[:N_CTX, KV_DIM:].reshape(BATCH, SEQ, N_KV_HEADS, HEAD_DIM))
        att = _attention(q, kv, ck, cv, attn_sink[l], l)
        cvn = _conv_branch(u, conv_w[l], row(conv_b), row(conv_ln_g), row(conv_ln_b))
        x1, xp, qw, meta = _mix(
            x, mod3, att, cvn, sg, w_attn_o[l].astype(_BF16), w_conv_o[l].astype(_BF16),
            w_out[l].astype(_BF16), row(g_post_mix), row(g_pre_ffn), w_router_pad[l],
            row(b_router_pad), l)
        te, n_tiles, src_unit, back_unit = _plan(meta)
        ys = _experts(te, n_tiles, src_unit, xp, w_gate_up, b_gate_up, w_down, b_down, l)
        x = _combine(back_unit, ys, x1, mod3, qw, row(g_post_ffn), l)

    y_prompt = x[:N_CTX].reshape(BATCH, SEQ, D_MODEL)
    y_sample = x[N_CTX:].reshape(DEC_BATCH, DEC_SEQ, D_MODEL)
    return (y_prompt, y_sample, jnp.stack(new_k, axis=1), jnp.stack(new_v, axis=1))
```

```python
import jax
import jax.numpy as jnp
from jax import lax
from jax.experimental import pallas as pl
from jax.experimental.pallas import tpu as pltpu

D_MODEL = 1024
BATCH = 16
SEQ = 256
DEPTH = 2
DEC_BATCH = 2
DEC_SEQ = 2048
PAST_LEN = 256
GRID_W = 64
N_HEADS = 16
N_KV_HEADS = 4
GROUP = N_HEADS // N_KV_HEADS
HEAD_DIM = 64
Q_DIM = N_HEADS * HEAD_DIM
KV_DIM = N_KV_HEADS * HEAD_DIM
WINDOW = 128
ATTN_SCALE = HEAD_DIM ** -0.5
ROPE_THETA = 10000.0
ROPE_HALF = HEAD_DIM // 2
ROPE_FREQS = ROPE_HALF // 2
C_CONV = D_MODEL // 2
CONV_WIDTH = 31
CONV_PAD = (CONV_WIDTH - 1) // 2
N_EXPERTS = 32
TOP_K = 4
D_FF = D_MODEL
SWIGLU_LIMIT = 7.0
SWIGLU_ALPHA = 1.702
EPS = 1e-6
IN_COLS = Q_DIM + 2 * KV_DIM + 2 * C_CONV + 2 * D_MODEL

N_CTX = BATCH * SEQ
N_LAT = DEC_BATCH * DEC_SEQ
N_TOK = N_CTX + N_LAT
N_COND = 1 + DEC_BATCH
COND_ROWS = 8

LANE = 128
SUBLANE = 8
TM = 512
TQ = 256
KWIN = TQ + 2 * WINDOW
assert GROUP == 4 and 2 * HEAD_DIM == LANE
LOG2E = 1.4426950408889634
QK_SCALE = ATTN_SCALE * LOG2E
HALO = 16
CONV_SPAN = SEQ + ((HALO - CONV_PAD + CONV_WIDTH - 1) // SUBLANE) * SUBLANE
assert CONV_SPAN + SUBLANE - 1 <= SEQ + 2 * HALO

TB = 256
N_BLOCKS = N_TOK // TB
UNIT = SUBLANE
SLOTS = 1280
BLOCK_UNITS = SLOTS // UNIT
TE = 256
TILE_UNITS = TE // UNIT
N_ETILES = (N_BLOCKS * BLOCK_UNITS) // TILE_UNITS + N_EXPERTS
VMEM_LIMIT = 56 * 1024 * 1024

assert SLOTS >= TB * TOP_K + N_EXPERTS * (UNIT - 1) and SLOTS % UNIT == 0

_F32 = jnp.float32
_BF16 = jnp.bfloat16
_I32 = jnp.int32


def _rms(x, g):
    return x * lax.rsqrt(jnp.mean(x * x, axis=-1, keepdims=True) + EPS) * g


def _cond_index(i, tile):
    n_ctx_tiles = N_CTX // tile
    return jnp.where(i < n_ctx_tiles, 0, 1 + (i - n_ctx_tiles) // (DEC_SEQ // tile))


def _mod_kernel(cond_ref, w_ref, b_ref, out_ref):
    cnd = cond_ref[...]
    s = cnd * jax.nn.sigmoid(cnd)
    out_ref[...] = jnp.dot(s, w_ref[...], precision=lax.Precision.HIGHEST,
                           preferred_element_type=_F32) + b_ref[...]


def _modulation(cond, w_ada, b_ada):
    tn = 1536
    nt = 6 * D_MODEL // tn
    return pl.pallas_call(
        _mod_kernel,
        grid=(DEPTH, nt),
        in_specs=[
            pl.BlockSpec((COND_ROWS, D_MODEL), lambda l, n: (0, 0)),
            pl.BlockSpec((None, D_MODEL, tn), lambda l, n: (l, 0, n)),
            pl.BlockSpec((None, 1, tn), lambda l, n: (l, 0, n)),
        ],
        out_specs=pl.BlockSpec((None, COND_ROWS, tn), lambda l, n: (l, 0, n)),
        out_shape=jax.ShapeDtypeStruct((DEPTH, COND_ROWS, 6 * D_MODEL), _F32),
        compiler_params=pltpu.CompilerParams(
            dimension_semantics=("arbitrary", "arbitrary"), vmem_limit_bytes=VMEM_LIMIT),
        name="modulation",
    )(cond, w_ada, b_ada.reshape(DEPTH, 1, 6 * D_MODEL))


def _rope_chunk(x, cos, sin):
    lane = lax.broadcasted_iota(_I32, x.shape, 1)
    partner = jnp.where((lane & ROPE_FREQS) == 0,
                        pltpu.roll(x, LANE - ROPE_FREQS, 1), pltpu.roll(x, ROPE_FREQS, 1))
    return x * cos + partner * sin


def _inproj_kernel(x_ref, mod_ref, g_ref, w_ref, cos_ref, sin_ref,
                   q_ref, kv_ref, u_ref, sg_ref):
    i = pl.program_id(0)
    x = x_ref[...]
    sh = mod_ref[:, 0:D_MODEL]
    sc = mod_ref[:, D_MODEL:2 * D_MODEL]
    h = (_rms(x, g_ref[...]) * (1.0 + sc) + sh).astype(_BF16)

    c0 = 0
    q = jnp.dot(h, w_ref[:, c0:c0 + Q_DIM], preferred_element_type=_F32) * QK_SCALE
    c0 += Q_DIM
    kv = jnp.dot(h, w_ref[:, c0:c0 + 2 * KV_DIM], preferred_element_type=_F32)
    c0 += 2 * KV_DIM
    ua = jnp.dot(h, w_ref[:, c0:c0 + C_CONV], preferred_element_type=_F32)
    c0 += C_CONV
    ub = jnp.dot(h, w_ref[:, c0:c0 + C_CONV], preferred_element_type=_F32)
    c0 += C_CONV
    g = jnp.dot(h, w_ref[:, c0:c0 + 2 * D_MODEL], preferred_element_type=_F32)

    u_ref[...] = ua * jax.nn.sigmoid(ub)
    sg_ref[...] = jax.nn.sigmoid(g).astype(_BF16)

    is_latent = i >= N_CTX // TM

    @pl.when(jnp.logical_not(is_latent))
    def _():
        q_ref[...] = q.astype(_BF16)
        kv_ref[...] = kv

    @pl.when(is_latent)
    def _():
        cos = cos_ref[...]
        sin = sin_ref[...]
        for j in range(Q_DIM // LANE):
            sl = slice(j * LANE, (j + 1) * LANE)
            q_ref[:, sl] = _rope_chunk(q[:, sl], cos, sin).astype(_BF16)
        for j in range(KV_DIM // LANE):
            sl = slice(j * LANE, (j + 1) * LANE)
            kv_ref[:, sl] = _rope_chunk(kv[:, sl], cos, sin)
        kv_ref[:, KV_DIM:] = kv[:, KV_DIM:]


def _inproj(x, mod3, g_pre, w_in_bf, cos_t, sin_t, layer):
    n_ctx_tiles = N_CTX // TM
    lat_tiles = DEC_SEQ // TM

    def rope_map(i):
        return (jnp.where(i < n_ctx_tiles, 0, (i - n_ctx_tiles) % lat_tiles), 0)

    return pl.pallas_call(
        _inproj_kernel,
        grid=(N_TOK // TM,),
        in_specs=[
            pl.BlockSpec((TM, D_MODEL), lambda i: (i, 0)),
            pl.BlockSpec((None, 1, 6 * D_MODEL),
                         lambda i: (layer * COND_ROWS + _cond_index(i, TM), 0, 0)),
            pl.BlockSpec((1, D_MODEL), lambda i: (0, 0)),
            pl.BlockSpec((D_MODEL, IN_COLS), lambda i: (0, 0)),
            pl.BlockSpec((TM, LANE), rope_map),
            pl.BlockSpec((TM, LANE), rope_map),
        ],
        out_specs=[
            pl.BlockSpec((TM, Q_DIM), lambda i: (i, 0)),
            pl.BlockSpec((TM, 2 * KV_DIM), lambda i: (i, 0)),
            pl.BlockSpec((TM, C_CONV), lambda i: (i, 0)),
            pl.BlockSpec((TM, 2 * D_MODEL), lambda i: (i, 0)),
        ],
        out_shape=[
            jax.ShapeDtypeStruct((N_TOK, Q_DIM), _BF16),
            jax.ShapeDtypeStruct((N_TOK, 2 * KV_DIM), _F32),
            jax.ShapeDtypeStruct((N_TOK, C_CONV), _F32),
            jax.ShapeDtypeStruct((N_TOK, 2 * D_MODEL), _BF16),
        ],
        compiler_params=pltpu.CompilerParams(
            dimension_semantics=("arbitrary",), vmem_limit_bytes=VMEM_LIMIT),
        name="inproj",
    )(x, mod3, g_pre, w_in_bf, cos_t, sin_t)


def _pair_operands(k, v):
    zero = jnp.zeros_like(k)
    one = jnp.ones_like(v)
    ka = jnp.concatenate([k, zero], axis=1).astype(_BF16)
    kb = jnp.concatenate([zero, k], axis=1).astype(_BF16)
    va = jnp.concatenate([v, zero, one, zero], axis=1).astype(_BF16)
    vb = jnp.concatenate([zero, v, zero, one], axis=1).astype(_BF16)
    return ka, kb, va, vb


def _pair_attend(qq, operands, masks, sink_a, sink_b):
    def scores(which):
        out = []
        for ops, mask in zip(operands, masks):
            s = lax.dot_general(qq, ops[which], (((1,), (1,)), ((), ())),
                                preferred_element_type=_F32)
            out.append(s if mask is None else jnp.where(mask, s, -jnp.inf))
        return out

    acc = jnp.zeros((qq.shape[0], 2 * LANE), _F32)
    sink_terms = []
    for which, sink in ((0, sink_a), (1, sink_b)):
        ss = scores(which)
        m = sink
        for s in ss:
            m = jnp.maximum(m, jnp.max(s, axis=-1, keepdims=True))
        for s, ops in zip(ss, operands):
            acc = acc + jnp.dot(jnp.exp2(s - m).astype(_BF16), ops[2 + which],
                                preferred_element_type=_F32)
        sink_terms.append(jnp.exp2(sink - m))
    lane = lax.broadcasted_iota(_I32, (qq.shape[0], LANE), 1)
    sink_term = jnp.where(lane < HEAD_DIM, sink_terms[0], sink_terms[1])
    return acc[:, :LANE] / (acc[:, LANE:] + sink_term)


def _group_attend(q_ref, o_ref, hk, operands, masks, sink_ref):
    pairs = [slice((2 * hk + j) * LANE, (2 * hk + j + 1) * LANE) for j in range(GROUP // 2)]
    qq = jnp.concatenate([q_ref[:, sl] for sl in pairs], axis=0)
    row = lax.broadcasted_iota(_I32, (qq.shape[0], 1), 0)
    first = row < TQ
    sink_a = jnp.where(first, sink_ref[GROUP * hk], sink_ref[GROUP * hk + 2]) * LOG2E
    sink_b = jnp.where(first, sink_ref[GROUP * hk + 1], sink_ref[GROUP * hk + 3]) * LOG2E
    out = _pair_attend(qq, operands, masks, sink_a, sink_b)
    for j, sl in enumerate(pairs):
        o_ref[:, sl] = out[j * TQ:(j + 1) * TQ].astype(_BF16)


def _attn_kernel(sink_ref, q_ref, kv_own_ref, kv_seq_ref, ck_ref, cv_ref, o_ref):
    i = pl.program_id(0)
    n_ctx_steps = N_CTX // TQ

    @pl.when(i < n_ctx_steps)
    def _():
        for hk in range(N_KV_HEADS):
            ks = slice(hk * HEAD_DIM, (hk + 1) * HEAD_DIM)
            vs = slice(KV_DIM + hk * HEAD_DIM, KV_DIM + (hk + 1) * HEAD_DIM)
            own = _pair_operands(kv_own_ref[:, ks], kv_own_ref[:, vs])
            _group_attend(q_ref, o_ref, hk, [own], [None], sink_ref)

    @pl.when(i >= n_ctx_steps)
    def _():
        qb = (i - n_ctx_steps) % (DEC_SEQ // TQ)
        q_start = qb * TQ
        k_start = pl.multiple_of(jnp.clip(q_start - WINDOW, 0, DEC_SEQ - KWIN), WINDOW)
        stacked = (GROUP // 2) * TQ
        qpos = q_start + lax.broadcasted_iota(_I32, (stacked, KWIN), 0) % TQ
        kpos = k_start + lax.broadcasted_iota(_I32, (stacked, KWIN), 1)
        mask = jnp.abs(kpos - qpos) <= WINDOW
        for hk in range(N_KV_HEADS):
            ks = slice(hk * HEAD_DIM, (hk + 1) * HEAD_DIM)
            vs = slice(KV_DIM + hk * HEAD_DIM, KV_DIM + (hk + 1) * HEAD_DIM)
            local = _pair_operands(kv_seq_ref[pl.ds(k_start, KWIN), ks],
                                   kv_seq_ref[pl.ds(k_start, KWIN), vs])
            cached = _pair_operands(ck_ref[:, ks], cv_ref[:, ks])
            _group_attend(q_ref, o_ref, hk, [local, cached], [mask, None], sink_ref)


def _attention(q, kv, cache_k, cache_v, sink, layer):
    n_ctx_steps = N_CTX // TQ
    nq = DEC_SEQ // TQ
    kv_off = N_CTX // DEC_SEQ

    def lat_batch(i):
        return jnp.maximum(i - n_ctx_steps, 0) // nq

    return pl.pallas_call(
        _attn_kernel,
        grid=(N_TOK // TQ,),
        in_specs=[
            pl.BlockSpec(memory_space=pltpu.SMEM),
            pl.BlockSpec((TQ, Q_DIM), lambda i: (i, 0)),
            pl.BlockSpec((TQ, 2 * KV_DIM), lambda i: (i, 0)),
            pl.BlockSpec((DEC_SEQ, 2 * KV_DIM), lambda i: (kv_off + lat_batch(i), 0)),
            pl.BlockSpec((None, None, PAST_LEN, KV_DIM), lambda i: (lat_batch(i), layer, 0, 0)),
            pl.BlockSpec((None, None, PAST_LEN, KV_DIM), lambda i: (lat_batch(i), layer, 0, 0)),
        ],
        out_specs=pl.BlockSpec((TQ, Q_DIM), lambda i: (i, 0)),
        out_shape=jax.ShapeDtypeStruct((N_TOK, Q_DIM), _BF16),
        compiler_params=pltpu.CompilerParams(
            dimension_semantics=("arbitrary",), vmem_limit_bytes=VMEM_LIMIT),
        name="attention",
    )(sink, q, kv, kv, cache_k, cache_v)


def _conv_kernel(prev_ref, cur_ref, next_ref, w_ref, b_ref, lg_ref, lb_ref, y_ref, pad_ref, sh_ref):
    i = pl.program_id(0)
    n_ctx_tiles = N_CTX // SEQ
    tiles_per_seq = jnp.where(i < n_ctx_tiles, 1, DEC_SEQ // SEQ)
    j = jnp.where(i < n_ctx_tiles, 0, (i - n_ctx_tiles) % (DEC_SEQ // SEQ))
    pad_ref[0:HALO, :] = jnp.where(j > 0, prev_ref[...], 0.0)
    pad_ref[HALO:HALO + SEQ, :] = cur_ref[...]
    pad_ref[HALO + SEQ:HALO + SEQ + HALO, :] = jnp.where(j < tiles_per_seq - 1, next_ref[...], 0.0)

    for r in range(SUBLANE):
        sh_ref[r] = pad_ref[r:r + CONV_SPAN, :]

    rows = 64
    for r0 in range(0, SEQ, rows):
        acc = jnp.zeros((rows, C_CONV), _F32) + b_ref[...]
        for t in range(CONV_WIDTH):
            off = HALO - CONV_PAD + t
            start = (off // SUBLANE) * SUBLANE + r0
            acc = acc + sh_ref[off % SUBLANE, start:start + rows, :] * w_ref[t:t + 1, :]
        mu = jnp.mean(acc, axis=-1, keepdims=True)
        d = acc - mu
        var = jnp.mean(d * d, axis=-1, keepdims=True)
        y = d * lax.rsqrt(var + EPS) * lg_ref[...] + lb_ref[...]
        y_ref[r0:r0 + rows, :] = (y * jax.nn.sigmoid(y)).astype(_BF16)


def _conv_branch(u, conv_w, conv_b, ln_g, ln_b):
    n_tiles = N_TOK // SEQ
    hb = SEQ // HALO
    last = N_TOK // HALO - 1
    return pl.pallas_call(
        _conv_kernel,
        grid=(n_tiles,),
        in_specs=[
            pl.BlockSpec((HALO, C_CONV), lambda i: (jnp.maximum(i * hb - 1, 0), 0)),
            pl.BlockSpec((SEQ, C_CONV), lambda i: (i, 0)),
            pl.BlockSpec((HALO, C_CONV), lambda i: (jnp.minimum((i + 1) * hb, last), 0)),
            pl.BlockSpec((CONV_WIDTH, C_CONV), lambda i: (0, 0)),
            pl.BlockSpec((1, C_CONV), lambda i: (0, 0)),
            pl.BlockSpec((1, C_CONV), lambda i: (0, 0)),
            pl.BlockSpec((1, C_CONV), lambda i: (0, 0)),
        ],
        out_specs=pl.BlockSpec((SEQ, C_CONV), lambda i: (i, 0)),
        out_shape=jax.ShapeDtypeStruct((N_TOK, C_CONV), _BF16),
        scratch_shapes=[pltpu.VMEM((SEQ + 2 * HALO, C_CONV), _F32),
                        pltpu.VMEM((SUBLANE, CONV_SPAN, C_CONV), _F32)],
        compiler_params=pltpu.CompilerParams(
            dimension_semantics=("arbitrary",), vmem_limit_bytes=VMEM_LIMIT),
        name="conv_branch",
    )(u, u, u, conv_w, conv_b, ln_g, ln_b)


def _mix_kernel(x_ref, mod_ref, att_ref, cv_ref, sg_ref, wa_ref, wc_ref, wo_ref,
                gpost_ref, gffn_ref, wr_ref, wrlo_ref, br_ref,
                x1_ref, xp_ref, qw_ref, meta_ref):
    a = jnp.dot(att_ref[...], wa_ref[...], preferred_element_type=_F32)
    cv = jnp.dot(cv_ref[...], wc_ref[...], preferred_element_type=_F32)
    m = sg_ref[:, 0:D_MODEL].astype(_F32) * a + sg_ref[:, D_MODEL:].astype(_F32) * cv
    mix = jnp.dot(m.astype(_BF16), wo_ref[...], preferred_element_type=_F32)
    gt1 = mod_ref[:, 2 * D_MODEL:3 * D_MODEL]
    sh2 = mod_ref[:, 3 * D_MODEL:4 * D_MODEL]
    sc2 = mod_ref[:, 4 * D_MODEL:5 * D_MODEL]
    x1 = x_ref[...] + gt1 * _rms(mix, gpost_ref[...])
    x1_ref[...] = x1
    h2 = _rms(x1, gffn_ref[...]) * (1.0 + sc2) + sh2
    h2b = h2.astype(_BF16)

    h2_lo = (h2 - h2b.astype(_F32)).astype(_BF16)
    logits = (jnp.dot(h2b, wr_ref[...], preferred_element_type=_F32)
              + jnp.dot(h2_lo, wr_ref[...], preferred_element_type=_F32)
              + jnp.dot(h2b, wrlo_ref[...], preferred_element_type=_F32) + br_ref[...])
    lane = lax.broadcasted_iota(_I32, (TB, LANE), 1).astype(_F32)
    member = jnp.zeros((TB, LANE), _F32)
    hots, exps = [], []
    top = None
    total = jnp.zeros((TB, 1), _F32)
    for k in range(TOP_K):
        mval = jnp.max(logits, axis=-1, keepdims=True)
        sel = jnp.min(jnp.where(logits == mval, lane, float(LANE)), axis=-1, keepdims=True)
        if top is None:
            top = mval
        e = jnp.exp(mval - top)
        total = total + e
        hot = lane == sel
        hots.append(hot)
        exps.append(e)
        member = member + jnp.where(hot, 1.0, 0.0)
        logits = jnp.where(hot, -jnp.inf, logits)

    r_i = lax.broadcasted_iota(_I32, (TB, TB), 0)
    c_i = lax.broadcasted_iota(_I32, (TB, TB), 1)
    lower = jnp.where(r_i > c_i, 1.0, 0.0).astype(_BF16)
    rank = jnp.dot(lower, member.astype(_BF16), preferred_element_type=_F32)
    count = jnp.sum(member, axis=0, keepdims=True)
    units = jnp.floor((count + float(UNIT - 1)) * (1.0 / UNIT))
    r_l = lax.broadcasted_iota(_I32, (LANE, LANE), 0)
    c_l = lax.broadcasted_iota(_I32, (LANE, LANE), 1)
    upper = jnp.where(r_l < c_l, 1.0, 0.0).astype(_BF16)
    unit_off = jnp.dot(jnp.broadcast_to(units, (SUBLANE, LANE)).astype(_BF16), upper,
                       preferred_element_type=_F32)[0:1, :]
    base = unit_off * float(UNIT) + rank

    slot_lane = lax.broadcasted_iota(_I32, (TB, SLOTS), 1).astype(_F32)
    qw = jnp.zeros((TB, SLOTS), _F32)
    slot_cols = jnp.zeros((TB, LANE), _F32)
    for k in range(TOP_K):
        slot = jnp.sum(jnp.where(hots[k], base, 0.0), axis=-1, keepdims=True)
        qw = qw + jnp.where(slot_lane == slot, exps[k] / total, 0.0)
        slot_cols = jnp.where(lane == float(k), slot, slot_cols)
    qw_ref[...] = qw.astype(_BF16)

    slot_rows = slot_cols.T
    slot_sub = lax.broadcasted_iota(_I32, (SLOTS, TB), 0).astype(_F32)
    perm = jnp.zeros((SLOTS, TB), _F32)
    for k in range(TOP_K):
        perm = perm + jnp.where(slot_sub == slot_rows[k:k + 1, :], 1.0, 0.0)
    perm = perm.astype(_BF16)
    xp_ref[...] = jnp.dot(perm, h2b, preferred_element_type=_F32)

    sub = lax.broadcasted_iota(_I32, (SUBLANE, LANE), 0)
    meta = jnp.where(sub == 0, units, jnp.where(sub == 1, unit_off, 0.0))
    meta_ref[...] = meta.astype(_I32)


def _mix(x, mod3, att, cvn, sg, wa, wc, wo, g_post, g_ffn, wr_hi, wr_lo, b_router_pad, layer):
    full = lambda shape: pl.BlockSpec(shape, lambda i: (0,) * len(shape))
    return pl.pallas_call(
        _mix_kernel,
        grid=(N_BLOCKS,),
        in_specs=[
            pl.BlockSpec((TB, D_MODEL), lambda i: (i, 0)),
            pl.BlockSpec((None, 1, 6 * D_MODEL),
                         lambda i: (layer * COND_ROWS + _cond_index(i, TB), 0, 0)),
            pl.BlockSpec((TB, Q_DIM), lambda i: (i, 0)),
            pl.BlockSpec((TB, C_CONV), lambda i: (i, 0)),
            pl.BlockSpec((TB, 2 * D_MODEL), lambda i: (i, 0)),
            full((Q_DIM, D_MODEL)),
            full((C_CONV, D_MODEL)),
            full((D_MODEL, D_MODEL)),
            full((1, D_MODEL)),
            full((1, D_MODEL)),
            full((D_MODEL, LANE)),
            full((D_MODEL, LANE)),
            full((1, LANE)),
        ],
        out_specs=[
            pl.BlockSpec((TB, D_MODEL), lambda i: (i, 0)),
            pl.BlockSpec((SLOTS, D_MODEL), lambda i: (i, 0)),
            pl.BlockSpec((TB, SLOTS), lambda i: (i, 0)),
            pl.BlockSpec((None, SUBLANE, LANE), lambda i: (i, 0, 0)),
        ],
        out_shape=[
            jax.ShapeDtypeStruct((N_TOK, D_MODEL), _F32),
            jax.ShapeDtypeStruct((N_BLOCKS * SLOTS, D_MODEL), _F32),
            jax.ShapeDtypeStruct((N_TOK, SLOTS), _BF16),
            jax.ShapeDtypeStruct((N_BLOCKS, SUBLANE, LANE), _I32),
        ],
        compiler_params=pltpu.CompilerParams(
            dimension_semantics=("arbitrary",), vmem_limit_bytes=VMEM_LIMIT),
        name="mix_router",
    )(x, mod3, att, cvn, sg, wa, wc, wo, g_post, g_ffn, wr_hi, wr_lo, b_router_pad)


def _plan(meta):
    units = meta[:, 0, :N_EXPERTS]
    seg_off = meta[:, 1, :N_EXPERTS]
    tiles = (jnp.sum(units, axis=0) + TILE_UNITS - 1) // TILE_UNITS
    tile_end = jnp.cumsum(tiles)
    n_tiles = tile_end[-1]
    region = (tile_end - tiles) * TILE_UNITS
    dst = region[None, :] + jnp.cumsum(units, axis=0) - units
    src = jnp.arange(N_BLOCKS, dtype=_I32)[:, None] * BLOCK_UNITS + seg_off

    tile_first = jnp.concatenate([jnp.zeros((1,), _I32), tile_end.astype(_I32)])

    dst_f, len_f, src_f = dst.reshape(1, -1), units.reshape(1, -1), src.reshape(1, -1)
    d = jnp.arange(N_ETILES * TILE_UNITS, dtype=_I32)[:, None]
    in_seg = (dst_f <= d) & (d < dst_f + len_f)
    src_unit = jnp.sum(jnp.where(in_seg, src_f + d - dst_f, 0), axis=1)

    u = jnp.arange(BLOCK_UNITS, dtype=_I32)[None, :, None]
    so, un = seg_off[:, None, :], units[:, None, :]
    in_blk = (so <= u) & (u < so + un)
    back_unit = jnp.sum(jnp.where(in_blk, dst[:, None, :] + u - so, 0), axis=2)
    return (tile_first, n_tiles.reshape(1).astype(_I32), src_unit.astype(_I32),
            back_unit.reshape(-1).astype(_I32))


def _unit_gather(src_hbm, unit_ref, first, n_units, dst_buf, sem):
    for i in range(n_units):
        row = pl.multiple_of(unit_ref[first + i] * UNIT, UNIT)
        pltpu.make_async_copy(src_hbm.at[pl.ds(row, UNIT), :],
                              dst_buf.at[pl.ds(i * UNIT, UNIT), :], sem).start()


def _unit_gather_wait(src_hbm, n_units, dst_buf, sem):
    pltpu.make_async_copy(src_hbm.at[pl.ds(0, n_units * UNIT), :], dst_buf, sem).wait()


def _tile_write(ybuf_slot, ys_hbm, tile, sem):
    row = pl.multiple_of(tile * TE, TE)
    return pltpu.make_async_copy(ybuf_slot, ys_hbm.at[pl.ds(row, TE), :], sem)


def _expert_kernel(first_ref, nt_ref, src_ref, xp_hbm, wgu_ref, bgu_ref, wd_ref, bd_ref,
                   ys_hbm, wgu_bf, wd_bf, xbuf, ybuf, xsem, ysem):
    e = pl.program_id(0)
    n_live = nt_ref[0]
    t_lo = first_ref[e]
    t_hi = first_ref[e + 1]

    @pl.when(e == 0)
    def _():
        _unit_gather(xp_hbm, src_ref, 0, TILE_UNITS, xbuf.at[0], xsem.at[0])

    @pl.when(t_hi > t_lo)
    def _():
        wgu_bf[...] = wgu_ref[...].astype(_BF16)
        wd_bf[...] = wd_ref[...].astype(_BF16)

    def tile_body(t, carry):
        slot = t % 2

        @pl.when(t + 1 < n_live)
        def _():
            _unit_gather(xp_hbm, src_ref, (t + 1) * TILE_UNITS, TILE_UNITS,
                         xbuf.at[1 - slot], xsem.at[1 - slot])

        _unit_gather_wait(xp_hbm, TILE_UNITS, xbuf.at[slot], xsem.at[slot])
        gu = jnp.dot(xbuf[slot].astype(_BF16), wgu_bf[...],
                     preferred_element_type=_F32) + bgu_ref[...]
        gate = jnp.minimum(gu[:, :D_FF], SWIGLU_LIMIT)
        lin = jnp.clip(gu[:, D_FF:], -SWIGLU_LIMIT, SWIGLU_LIMIT)
        act = gate * jax.nn.sigmoid(SWIGLU_ALPHA * gate) * (lin + 1.0)
        y = jnp.dot(act.astype(_BF16), wd_bf[...], preferred_element_type=_F32) + bd_ref[...]

        @pl.when(t >= 2)
        def _():
            _tile_write(ybuf.at[slot], ys_hbm, t - 2, ysem.at[slot]).wait()

        ybuf[slot] = y
        _tile_write(ybuf.at[slot], ys_hbm, t, ysem.at[slot]).start()
        return carry

    lax.fori_loop(t_lo, t_hi, tile_body, 0)

    @pl.when(e == N_EXPERTS - 1)
    def _():
        for back in (2, 1):
            @pl.when(n_live >= back)
            def _():
                t = n_live - back
                _tile_write(ybuf.at[t % 2], ys_hbm, t, ysem.at[t % 2]).wait()

        ybuf[0] = jnp.zeros((TE, D_MODEL), _F32)

        def zero_start(t, carry):
            _tile_write(ybuf.at[0], ys_hbm, t, ysem.at[0]).start()
            return carry

        def zero_wait(t, carry):
            _tile_write(ybuf.at[0], ys_hbm, t, ysem.at[0]).wait()
            return carry

        lax.fori_loop(n_live, N_ETILES, zero_start, 0)
        lax.fori_loop(n_live, N_ETILES, zero_wait, 0)


def _experts(tile_first, n_tiles, src_unit, xp, w_gate_up, b_gate_up, w_down, b_down, layer):
    grid_spec = pltpu.PrefetchScalarGridSpec(
        num_scalar_prefetch=3,
        grid=(N_EXPERTS,),
        in_specs=[
            pl.BlockSpec(memory_space=pl.ANY),
            pl.BlockSpec((None, None, D_MODEL, 2 * D_FF), lambda e, tf, nt, su: (layer, e, 0, 0)),
            pl.BlockSpec((None, None, 1, 2 * D_FF), lambda e, tf, nt, su: (layer, e, 0, 0)),
            pl.BlockSpec((None, None, D_FF, D_MODEL), lambda e, tf, nt, su: (layer, e, 0, 0)),
            pl.BlockSpec((None, None, 1, D_MODEL), lambda e, tf, nt, su: (layer, e, 0, 0)),
        ],
        out_specs=pl.BlockSpec(memory_space=pl.ANY),
        scratch_shapes=[
            pltpu.VMEM((D_MODEL, 2 * D_FF), _BF16),
            pltpu.VMEM((D_FF, D_MODEL), _BF16),
            pltpu.VMEM((2, TE, D_MODEL), _F32),
            pltpu.VMEM((2, TE, D_MODEL), _F32),
            pltpu.SemaphoreType.DMA((2,)),
            pltpu.SemaphoreType.DMA((2,)),
        ],
    )
    return pl.pallas_call(
        _expert_kernel,
        grid_spec=grid_spec,
        out_shape=jax.ShapeDtypeStruct((N_ETILES * TE, D_MODEL), _F32),
        compiler_params=pltpu.CompilerParams(
            dimension_semantics=("arbitrary",), vmem_limit_bytes=VMEM_LIMIT),
        name="experts",
    )(tile_first, n_tiles, src_unit, xp, w_gate_up,
      b_gate_up.reshape(DEPTH, N_EXPERTS, 1, 2 * D_FF), w_down,
      b_down.reshape(DEPTH, N_EXPERTS, 1, D_MODEL))


def _combine_kernel(back_ref, ys_hbm, x_ref, mod_ref, qw_ref, g_ref, o_ref, ybuf, sem):
    b = pl.program_id(0)
    slot = b % 2

    @pl.when(b == 0)
    def _():
        _unit_gather(ys_hbm, back_ref, 0, BLOCK_UNITS, ybuf.at[0], sem.at[0])

    @pl.when(b + 1 < N_BLOCKS)
    def _():
        _unit_gather(ys_hbm, back_ref, (b + 1) * BLOCK_UNITS, BLOCK_UNITS,
                     ybuf.at[1 - slot], sem.at[1 - slot])

    _unit_gather_wait(ys_hbm, BLOCK_UNITS, ybuf.at[slot], sem.at[slot])
    moe = jnp.dot(qw_ref[...], ybuf[slot].astype(_BF16), preferred_element_type=_F32)
    gt2 = mod_ref[:, 5 * D_MODEL:6 * D_MODEL]
    o_ref[...] = x_ref[...] + gt2 * _rms(moe, g_ref[...])


def _combine(back_unit, ys, x1, mod3, qw, g_post, layer):
    grid_spec = pltpu.PrefetchScalarGridSpec(
        num_scalar_prefetch=1,
        grid=(N_BLOCKS,),
        in_specs=[
            pl.BlockSpec(memory_space=pl.ANY),
            pl.BlockSpec((TB, D_MODEL), lambda b, bu: (b, 0)),
            pl.BlockSpec((None, 1, 6 * D_MODEL),
                         lambda b, bu: (layer * COND_ROWS + _cond_index(b, TB), 0, 0)),
            pl.BlockSpec((TB, SLOTS), lambda b, bu: (b, 0)),
            pl.BlockSpec((1, D_MODEL), lambda b, bu: (0, 0)),
        ],
        out_specs=pl.BlockSpec((TB, D_MODEL), lambda b, bu: (b, 0)),
        scratch_shapes=[
            pltpu.VMEM((2, SLOTS, D_MODEL), _F32),
            pltpu.SemaphoreType.DMA((2,)),
        ],
    )
    return pl.pallas_call(
        _combine_kernel,
        grid_spec=grid_spec,
        out_shape=jax.ShapeDtypeStruct((N_TOK, D_MODEL), _F32),
        compiler_params=pltpu.CompilerParams(
            dimension_semantics=("arbitrary",), vmem_limit_bytes=VMEM_LIMIT),
        name="combine_residual",
    )(back_unit, ys, x1, mod3, qw, g_post)


def _rope_tables():
    pos = jnp.arange(DEC_SEQ)
    row = (pos // GRID_W).astype(_F32)
    col = (pos % GRID_W).astype(_F32)
    inv = ROPE_THETA ** (-jnp.arange(ROPE_FREQS, dtype=_F32) / ROPE_FREQS)
    ang_r = row[:, None] * inv[None, :]
    ang_c = col[:, None] * inv[None, :]
    cos = jnp.concatenate([jnp.cos(ang_r)] * 2 + [jnp.cos(ang_c)] * 2, axis=-1)
    sin = jnp.concatenate([-jnp.sin(ang_r), jnp.sin(ang_r), -jnp.sin(ang_c), jnp.sin(ang_c)], axis=-1)
    reps = LANE // HEAD_DIM
    return jnp.tile(cos, (1, reps)), jnp.tile(sin, (1, reps))


def kernel(x_prompt, x_sample, cache_k, cache_v, c, c_ctx, w_ada, b_ada, g_pre_mix, g_post_mix,
           g_pre_ffn, g_post_ffn, w_in, attn_sink, w_attn_o, conv_w, conv_b, conv_ln_g, conv_ln_b,
           w_conv_o, w_out, w_router, b_router, w_gate_up, b_gate_up, w_down, b_down):
    x = jnp.concatenate([x_prompt.reshape(N_CTX, D_MODEL), x_sample.reshape(N_LAT, D_MODEL)], axis=0)
    cond = jnp.concatenate([c_ctx[None, :], c, jnp.zeros((COND_ROWS - N_COND, D_MODEL), _F32)], axis=0)
    mod = _modulation(cond, w_ada, b_ada)
    mod3 = mod.reshape(DEPTH * COND_ROWS, 1, 6 * D_MODEL)
    cos_t, sin_t = _rope_tables()
    ck = cache_k.reshape(DEC_BATCH, DEPTH, PAST_LEN, KV_DIM)
    cv = cache_v.reshape(DEC_BATCH, DEPTH, PAST_LEN, KV_DIM)
    w_router_pad = jnp.pad(w_router, ((0, 0), (0, 0), (0, LANE - N_EXPERTS)))
    wr_hi = w_router_pad.astype(_BF16)
    wr_lo = (w_router_pad - wr_hi.astype(_F32)).astype(_BF16)
    b_router_pad = jnp.pad(b_router, ((0, 0), (0, LANE - N_EXPERTS)), constant_values=-jnp.inf)

    new_k, new_v = [], []
    for l in range(DEPTH):
        row = lambda a: a[l][None, :]
        q, kv, u, sg = _inproj(x, mod3, row(g_pre_mix), w_in[l].astype(_BF16), cos_t, sin_t, l)
        new_k.append(kv[:N_CTX, :KV_DIM].reshape(BATCH, SEQ, N_KV_HEADS, HEAD_DIM))
        new_v.append(kv[:N_CTX, KV_DIM:].reshape(BATCH, SEQ, N_KV_HEADS, HEAD_DIM))
        att = _attention(q, kv, ck, cv, attn_sink[l], l)
        cvn = _conv_branch(u, conv_w[l], row(conv_b), row(conv_ln_g), row(conv_ln_b))
        x1, xp, qw, meta = _mix(
            x, mod3, att, cvn, sg, w_attn_o[l].astype(_BF16), w_conv_o[l].astype(_BF16),
            w_out[l].astype(_BF16), row(g_post_mix), row(g_pre_ffn), wr_hi[l], wr_lo[l],
            row(b_router_pad), l)
        tile_first, n_tiles, src_unit, back_unit = _plan(meta)
        ys = _experts(tile_first, n_tiles, src_unit, xp, w_gate_up, b_gate_up, w_down, b_down, l)
        x = _combine(back_unit, ys, x1, mod3, qw, row(g_post_ffn), l)

    y_prompt = x[:N_CTX].reshape(BATCH, SEQ, D_MODEL)
    y_sample = x[N_CTX:].reshape(DEC_BATCH, DEC_SEQ, D_MODEL)
    return (y_prompt, y_sample, jnp.stack(new_k, axis=1), jnp.stack(new_v, axis=1))
```

```python
import jax
import jax.numpy as jnp
from jax import lax
from jax.experimental import pallas as pl
from jax.experimental.pallas import tpu as pltpu

D_MODEL = 1024
BATCH = 16
SEQ = 256
DEPTH = 2
DEC_BATCH = 2
DEC_SEQ = 2048
PAST_LEN = 256
GRID_W = 64
N_HEADS = 16
N_KV_HEADS = 4
GROUP = N_HEADS // N_KV_HEADS
HEAD_DIM = 64
Q_DIM = N_HEADS * HEAD_DIM
KV_DIM = N_KV_HEADS * HEAD_DIM
WINDOW = 128
ATTN_SCALE = HEAD_DIM ** -0.5
ROPE_THETA = 10000.0
ROPE_HALF = HEAD_DIM // 2
ROPE_FREQS = ROPE_HALF // 2
C_CONV = D_MODEL // 2
CONV_WIDTH = 31
CONV_PAD = (CONV_WIDTH - 1) // 2
N_EXPERTS = 32
TOP_K = 4
D_FF = D_MODEL
SWIGLU_LIMIT = 7.0
SWIGLU_ALPHA = 1.702
EPS = 1e-6
IN_COLS = Q_DIM + 2 * KV_DIM + 2 * C_CONV + 2 * D_MODEL

N_CTX = BATCH * SEQ
N_LAT = DEC_BATCH * DEC_SEQ
N_TOK = N_CTX + N_LAT
N_COND = 1 + DEC_BATCH
COND_ROWS = 8

LANE = 128
SUBLANE = 8
TM = 512
TQ = 256
KWIN = TQ + 2 * WINDOW
assert GROUP == 4 and 2 * HEAD_DIM == LANE
LOG2E = 1.4426950408889634
QK_SCALE = ATTN_SCALE * LOG2E
HALO = 16
CONV_SPAN = SEQ + ((HALO - CONV_PAD + CONV_WIDTH - 1) // SUBLANE) * SUBLANE
assert CONV_SPAN + SUBLANE - 1 <= SEQ + 2 * HALO

TB = 256
N_BLOCKS = N_TOK // TB
UNIT = SUBLANE
SLOTS = 1280
BLOCK_UNITS = SLOTS // UNIT
TE = 256
TILE_UNITS = TE // UNIT
N_ETILES = (N_BLOCKS * BLOCK_UNITS) // TILE_UNITS + N_EXPERTS
GATHER_PRIORITY = 1
VMEM_LIMIT = 56 * 1024 * 1024

assert SLOTS >= TB * TOP_K + N_EXPERTS * (UNIT - 1) and SLOTS % UNIT == 0

_F32 = jnp.float32
_BF16 = jnp.bfloat16
_I32 = jnp.int32


def _rms(x, g):
    return x * lax.rsqrt(jnp.mean(x * x, axis=-1, keepdims=True) + EPS) * g


def _cond_index(i, tile):
    n_ctx_tiles = N_CTX // tile
    return jnp.where(i < n_ctx_tiles, 0, 1 + (i - n_ctx_tiles) // (DEC_SEQ // tile))


def _mod_kernel(cond_ref, w_ref, b_ref, out_ref):
    cnd = cond_ref[...]
    s = cnd * jax.nn.sigmoid(cnd)
    out_ref[...] = jnp.dot(s, w_ref[...], precision=lax.Precision.HIGHEST,
                           preferred_element_type=_F32) + b_ref[...]


def _modulation(cond, w_ada, b_ada):
    tn = 1536
    nt = 6 * D_MODEL // tn
    return pl.pallas_call(
        _mod_kernel,
        grid=(DEPTH, nt),
        in_specs=[
            pl.BlockSpec((COND_ROWS, D_MODEL), lambda l, n: (0, 0)),
            pl.BlockSpec((None, D_MODEL, tn), lambda l, n: (l, 0, n)),
            pl.BlockSpec((None, 1, tn), lambda l, n: (l, 0, n)),
        ],
        out_specs=pl.BlockSpec((None, COND_ROWS, tn), lambda l, n: (l, 0, n)),
        out_shape=jax.ShapeDtypeStruct((DEPTH, COND_ROWS, 6 * D_MODEL), _F32),
        compiler_params=pltpu.CompilerParams(
            dimension_semantics=("arbitrary", "arbitrary"), vmem_limit_bytes=VMEM_LIMIT),
        name="modulation",
    )(cond, w_ada, b_ada.reshape(DEPTH, 1, 6 * D_MODEL))


def _rope_chunk(x, cos, sin):
    lane = lax.broadcasted_iota(_I32, x.shape, 1)
    partner = jnp.where((lane & ROPE_FREQS) == 0,
                        pltpu.roll(x, LANE - ROPE_FREQS, 1), pltpu.roll(x, ROPE_FREQS, 1))
    return x * cos + partner * sin


def _inproj_kernel(x_ref, mod_ref, g_ref, w_ref, cos_ref, sin_ref,
                   q_ref, kv_ref, u_ref, sg_ref):
    i = pl.program_id(0)
    x = x_ref[...]
    sh = mod_ref[:, 0:D_MODEL]
    sc = mod_ref[:, D_MODEL:2 * D_MODEL]
    h = (_rms(x, g_ref[...]) * (1.0 + sc) + sh).astype(_BF16)

    c0 = 0
    q = jnp.dot(h, w_ref[:, c0:c0 + Q_DIM], preferred_element_type=_F32) * QK_SCALE
    c0 += Q_DIM
    kv = jnp.dot(h, w_ref[:, c0:c0 + 2 * KV_DIM], preferred_element_type=_F32)
    c0 += 2 * KV_DIM
    ua = jnp.dot(h, w_ref[:, c0:c0 + C_CONV], preferred_element_type=_F32)
    c0 += C_CONV
    ub = jnp.dot(h, w_ref[:, c0:c0 + C_CONV], preferred_element_type=_F32)
    c0 += C_CONV
    g = jnp.dot(h, w_ref[:, c0:c0 + 2 * D_MODEL], preferred_element_type=_F32)

    u_ref[...] = ua * jax.nn.sigmoid(ub)
    sg_ref[...] = jax.nn.sigmoid(g).astype(_BF16)

    is_latent = i >= N_CTX // TM

    @pl.when(jnp.logical_not(is_latent))
    def _():
        q_ref[...] = q.astype(_BF16)
        kv_ref[...] = kv

    @pl.when(is_latent)
    def _():
        cos = cos_ref[...]
        sin = sin_ref[...]
        for j in range(Q_DIM // LANE):
            sl = slice(j * LANE, (j + 1) * LANE)
            q_ref[:, sl] = _rope_chunk(q[:, sl], cos, sin).astype(_BF16)
        for j in range(KV_DIM // LANE):
            sl = slice(j * LANE, (j + 1) * LANE)
            kv_ref[:, sl] = _rope_chunk(kv[:, sl], cos, sin)
        kv_ref[:, KV_DIM:] = kv[:, KV_DIM:]


def _inproj(x, mod3, g_pre, w_in_bf, cos_t, sin_t, layer):
    n_ctx_tiles = N_CTX // TM
    lat_tiles = DEC_SEQ // TM

    def rope_map(i):
        return (jnp.where(i < n_ctx_tiles, 0, (i - n_ctx_tiles) % lat_tiles), 0)

    return pl.pallas_call(
        _inproj_kernel,
        grid=(N_TOK // TM,),
        in_specs=[
            pl.BlockSpec((TM, D_MODEL), lambda i: (i, 0)),
            pl.BlockSpec((None, 1, 6 * D_MODEL),
                         lambda i: (layer * COND_ROWS + _cond_index(i, TM), 0, 0)),
            pl.BlockSpec((1, D_MODEL), lambda i: (0, 0)),
            pl.BlockSpec((D_MODEL, IN_COLS), lambda i: (0, 0)),
            pl.BlockSpec((TM, LANE), rope_map),
            pl.BlockSpec((TM, LANE), rope_map),
        ],
        out_specs=[
            pl.BlockSpec((TM, Q_DIM), lambda i: (i, 0)),
            pl.BlockSpec((TM, 2 * KV_DIM), lambda i: (i, 0)),
            pl.BlockSpec((TM, C_CONV), lambda i: (i, 0)),
            pl.BlockSpec((TM, 2 * D_MODEL), lambda i: (i, 0)),
        ],
        out_shape=[
            jax.ShapeDtypeStruct((N_TOK, Q_DIM), _BF16),
            jax.ShapeDtypeStruct((N_TOK, 2 * KV_DIM), _F32),
            jax.ShapeDtypeStruct((N_TOK, C_CONV), _F32),
            jax.ShapeDtypeStruct((N_TOK, 2 * D_MODEL), _BF16),
        ],
        compiler_params=pltpu.CompilerParams(
            dimension_semantics=("arbitrary",), vmem_limit_bytes=VMEM_LIMIT),
        name="inproj",
    )(x, mod3, g_pre, w_in_bf, cos_t, sin_t)


def _pair_operands(k, v):
    zero = jnp.zeros_like(k)
    one = jnp.ones_like(v)
    ka = jnp.concatenate([k, zero], axis=1).astype(_BF16)
    kb = jnp.concatenate([zero, k], axis=1).astype(_BF16)
    va = jnp.concatenate([v, zero, one, zero], axis=1).astype(_BF16)
    vb = jnp.concatenate([zero, v, zero, one], axis=1).astype(_BF16)
    return ka, kb, va, vb


def _pair_attend(qq, operands, masks, sink_a, sink_b):
    def scores(which):
        out = []
        for ops, mask in zip(operands, masks):
            s = lax.dot_general(qq, ops[which], (((1,), (1,)), ((), ())),
                                preferred_element_type=_F32)
            out.append(s if mask is None else jnp.where(mask, s, -jnp.inf))
        return out

    acc = jnp.zeros((qq.shape[0], 2 * LANE), _F32)
    sink_terms = []
    for which, sink in ((0, sink_a), (1, sink_b)):
        ss = scores(which)
        m = sink
        for s in ss:
            m = jnp.maximum(m, jnp.max(s, axis=-1, keepdims=True))
        for s, ops in zip(ss, operands):
            acc = acc + jnp.dot(jnp.exp2(s - m).astype(_BF16), ops[2 + which],
                                preferred_element_type=_F32)
        sink_terms.append(jnp.exp2(sink - m))
    lane = lax.broadcasted_iota(_I32, (qq.shape[0], LANE), 1)
    sink_term = jnp.where(lane < HEAD_DIM, sink_terms[0], sink_terms[1])
    return acc[:, :LANE] / (acc[:, LANE:] + sink_term)


def _group_attend(q_ref, o_ref, hk, operands, masks, sink_ref):
    pairs = [slice((2 * hk + j) * LANE, (2 * hk + j + 1) * LANE) for j in range(GROUP // 2)]
    qq = jnp.concatenate([q_ref[:, sl] for sl in pairs], axis=0)
    row = lax.broadcasted_iota(_I32, (qq.shape[0], 1), 0)
    first = row < TQ
    sink_a = jnp.where(first, sink_ref[GROUP * hk], sink_ref[GROUP * hk + 2]) * LOG2E
    sink_b = jnp.where(first, sink_ref[GROUP * hk + 1], sink_ref[GROUP * hk + 3]) * LOG2E
    out = _pair_attend(qq, operands, masks, sink_a, sink_b)
    for j, sl in enumerate(pairs):
        o_ref[:, sl] = out[j * TQ:(j + 1) * TQ].astype(_BF16)


def _attn_kernel(sink_ref, q_ref, kv_own_ref, kv_seq_ref, ck_ref, cv_ref, o_ref):
    i = pl.program_id(0)
    n_ctx_steps = N_CTX // TQ

    @pl.when(i < n_ctx_steps)
    def _():
        for hk in range(N_KV_HEADS):
            ks = slice(hk * HEAD_DIM, (hk + 1) * HEAD_DIM)
            vs = slice(KV_DIM + hk * HEAD_DIM, KV_DIM + (hk + 1) * HEAD_DIM)
            own = _pair_operands(kv_own_ref[:, ks], kv_own_ref[:, vs])
            _group_attend(q_ref, o_ref, hk, [own], [None], sink_ref)

    @pl.when(i >= n_ctx_steps)
    def _():
        qb = (i - n_ctx_steps) % (DEC_SEQ // TQ)
        q_start = qb * TQ
        k_start = pl.multiple_of(jnp.clip(q_start - WINDOW, 0, DEC_SEQ - KWIN), WINDOW)
        stacked = (GROUP // 2) * TQ
        qpos = q_start + lax.broadcasted_iota(_I32, (stacked, KWIN), 0) % TQ
        kpos = k_start + lax.broadcasted_iota(_I32, (stacked, KWIN), 1)
        mask = jnp.abs(kpos - qpos) <= WINDOW
        for hk in range(N_KV_HEADS):
            ks = slice(hk * HEAD_DIM, (hk + 1) * HEAD_DIM)
            vs = slice(KV_DIM + hk * HEAD_DIM, KV_DIM + (hk + 1) * HEAD_DIM)
            local = _pair_operands(kv_seq_ref[pl.ds(k_start, KWIN), ks],
                                   kv_seq_ref[pl.ds(k_start, KWIN), vs])
            cached = _pair_operands(ck_ref[:, ks], cv_ref[:, ks])
            _group_attend(q_ref, o_ref, hk, [local, cached], [mask, None], sink_ref)


def _attention(q, kv, cache_k, cache_v, sink, layer):
    n_ctx_steps = N_CTX // TQ
    nq = DEC_SEQ // TQ
    kv_off = N_CTX // DEC_SEQ

    def lat_batch(i):
        return jnp.maximum(i - n_ctx_steps, 0) // nq

    return pl.pallas_call(
        _attn_kernel,
        grid=(N_TOK // TQ,),
        in_specs=[
            pl.BlockSpec(memory_space=pltpu.SMEM),
            pl.BlockSpec((TQ, Q_DIM), lambda i: (i, 0)),
            pl.BlockSpec((TQ, 2 * KV_DIM), lambda i: (i, 0)),
            pl.BlockSpec((DEC_SEQ, 2 * KV_DIM), lambda i: (kv_off + lat_batch(i), 0)),
            pl.BlockSpec((None, None, PAST_LEN, KV_DIM), lambda i: (lat_batch(i), layer, 0, 0)),
            pl.BlockSpec((None, None, PAST_LEN, KV_DIM), lambda i: (lat_batch(i), layer, 0, 0)),
        ],
        out_specs=pl.BlockSpec((TQ, Q_DIM), lambda i: (i, 0)),
        out_shape=jax.ShapeDtypeStruct((N_TOK, Q_DIM), _BF16),
        compiler_params=pltpu.CompilerParams(
            dimension_semantics=("arbitrary",), vmem_limit_bytes=VMEM_LIMIT),
        name="attention",
    )(sink, q, kv, kv, cache_k, cache_v)


def _conv_kernel(prev_ref, cur_ref, next_ref, w_ref, b_ref, lg_ref, lb_ref, y_ref, pad_ref, sh_ref):
    i = pl.program_id(0)
    n_ctx_tiles = N_CTX // SEQ
    tiles_per_seq = jnp.where(i < n_ctx_tiles, 1, DEC_SEQ // SEQ)
    j = jnp.where(i < n_ctx_tiles, 0, (i - n_ctx_tiles) % (DEC_SEQ // SEQ))
    pad_ref[0:HALO, :] = jnp.where(j > 0, prev_ref[...], 0.0)
    pad_ref[HALO:HALO + SEQ, :] = cur_ref[...]
    pad_ref[HALO + SEQ:HALO + SEQ + HALO, :] = jnp.where(j < tiles_per_seq - 1, next_ref[...], 0.0)

    for r in range(SUBLANE):
        sh_ref[r] = pad_ref[r:r + CONV_SPAN, :]

    rows = 64
    for r0 in range(0, SEQ, rows):
        acc = jnp.zeros((rows, C_CONV), _F32) + b_ref[...]
        for t in range(CONV_WIDTH):
            off = HALO - CONV_PAD + t
            start = (off // SUBLANE) * SUBLANE + r0
            acc = acc + sh_ref[off % SUBLANE, start:start + rows, :] * w_ref[t:t + 1, :]
        mu = jnp.mean(acc, axis=-1, keepdims=True)
        d = acc - mu
        var = jnp.mean(d * d, axis=-1, keepdims=True)
        y = d * lax.rsqrt(var + EPS) * lg_ref[...] + lb_ref[...]
        y_ref[r0:r0 + rows, :] = (y * jax.nn.sigmoid(y)).astype(_BF16)


def _conv_branch(u, conv_w, conv_b, ln_g, ln_b):
    n_tiles = N_TOK // SEQ
    hb = SEQ // HALO
    last = N_TOK // HALO - 1
    return pl.pallas_call(
        _conv_kernel,
        grid=(n_tiles,),
        in_specs=[
            pl.BlockSpec((HALO, C_CONV), lambda i: (jnp.maximum(i * hb - 1, 0), 0)),
            pl.BlockSpec((SEQ, C_CONV), lambda i: (i, 0)),
            pl.BlockSpec((HALO, C_CONV), lambda i: (jnp.minimum((i + 1) * hb, last), 0)),
            pl.BlockSpec((CONV_WIDTH, C_CONV), lambda i: (0, 0)),
            pl.BlockSpec((1, C_CONV), lambda i: (0, 0)),
            pl.BlockSpec((1, C_CONV), lambda i: (0, 0)),
            pl.BlockSpec((1, C_CONV), lambda i: (0, 0)),
        ],
        out_specs=pl.BlockSpec((SEQ, C_CONV), lambda i: (i, 0)),
        out_shape=jax.ShapeDtypeStruct((N_TOK, C_CONV), _BF16),
        scratch_shapes=[pltpu.VMEM((SEQ + 2 * HALO, C_CONV), _F32),
                        pltpu.VMEM((SUBLANE, CONV_SPAN, C_CONV), _F32)],
        compiler_params=pltpu.CompilerParams(
            dimension_semantics=("arbitrary",), vmem_limit_bytes=VMEM_LIMIT),
        name="conv_branch",
    )(u, u, u, conv_w, conv_b, ln_g, ln_b)


def _mix_kernel(x_ref, mod_ref, att_ref, cv_ref, sg_ref, wa_ref, wc_ref, wo_ref,
                gpost_ref, gffn_ref, wr_ref, wrlo_ref, br_ref,
                x1_ref, xp_ref, qw_ref, meta_ref):
    a = jnp.dot(att_ref[...], wa_ref[...], preferred_element_type=_F32)
    cv = jnp.dot(cv_ref[...], wc_ref[...], preferred_element_type=_F32)
    m = sg_ref[:, 0:D_MODEL].astype(_F32) * a + sg_ref[:, D_MODEL:].astype(_F32) * cv
    mix = jnp.dot(m.astype(_BF16), wo_ref[...], preferred_element_type=_F32)
    gt1 = mod_ref[:, 2 * D_MODEL:3 * D_MODEL]
    sh2 = mod_ref[:, 3 * D_MODEL:4 * D_MODEL]
    sc2 = mod_ref[:, 4 * D_MODEL:5 * D_MODEL]
    x1 = x_ref[...] + gt1 * _rms(mix, gpost_ref[...])
    x1_ref[...] = x1
    h2 = _rms(x1, gffn_ref[...]) * (1.0 + sc2) + sh2
    h2b = h2.astype(_BF16)

    h2_lo = (h2 - h2b.astype(_F32)).astype(_BF16)
    logits = (jnp.dot(h2b, wr_ref[...], preferred_element_type=_F32)
              + jnp.dot(h2_lo, wr_ref[...], preferred_element_type=_F32)
              + jnp.dot(h2b, wrlo_ref[...], preferred_element_type=_F32) + br_ref[...])
    lane = lax.broadcasted_iota(_I32, (TB, LANE), 1).astype(_F32)
    member = jnp.zeros((TB, LANE), _F32)
    hots, exps = [], []
    top = None
    total = jnp.zeros((TB, 1), _F32)
    for k in range(TOP_K):
        mval = jnp.max(logits, axis=-1, keepdims=True)
        sel = jnp.min(jnp.where(logits == mval, lane, float(LANE)), axis=-1, keepdims=True)
        if top is None:
            top = mval
        e = jnp.exp(mval - top)
        total = total + e
        hot = lane == sel
        hots.append(hot)
        exps.append(e)
        member = member + jnp.where(hot, 1.0, 0.0)
        logits = jnp.where(hot, -jnp.inf, logits)

    r_i = lax.broadcasted_iota(_I32, (TB, TB), 0)
    c_i = lax.broadcasted_iota(_I32, (TB, TB), 1)
    lower = jnp.where(r_i > c_i, 1.0, 0.0).astype(_BF16)
    rank = jnp.dot(lower, member.astype(_BF16), preferred_element_type=_F32)
    count = jnp.sum(member, axis=0, keepdims=True)
    units = jnp.floor((count + float(UNIT - 1)) * (1.0 / UNIT))
    r_l = lax.broadcasted_iota(_I32, (LANE, LANE), 0)
    c_l = lax.broadcasted_iota(_I32, (LANE, LANE), 1)
    upper = jnp.where(r_l < c_l, 1.0, 0.0).astype(_BF16)
    unit_off = jnp.dot(jnp.broadcast_to(units, (SUBLANE, LANE)).astype(_BF16), upper,
                       preferred_element_type=_F32)[0:1, :]
    base = unit_off * float(UNIT) + rank

    slot_lane = lax.broadcasted_iota(_I32, (TB, SLOTS), 1).astype(_F32)
    qw = jnp.zeros((TB, SLOTS), _F32)
    slot_cols = jnp.zeros((TB, LANE), _F32)
    for k in range(TOP_K):
        slot = jnp.sum(jnp.where(hots[k], base, 0.0), axis=-1, keepdims=True)
        qw = qw + jnp.where(slot_lane == slot, exps[k] / total, 0.0)
        slot_cols = jnp.where(lane == float(k), slot, slot_cols)
    qw_ref[...] = qw.astype(_BF16)

    slot_rows = slot_cols.T
    slot_sub = lax.broadcasted_iota(_I32, (SLOTS, TB), 0).astype(_F32)
    perm = jnp.zeros((SLOTS, TB), _F32)
    for k in range(TOP_K):
        perm = perm + jnp.where(slot_sub == slot_rows[k:k + 1, :], 1.0, 0.0)
    perm = perm.astype(_BF16)
    xp_ref[...] = jnp.dot(perm, h2b, preferred_element_type=_F32)

    sub = lax.broadcasted_iota(_I32, (SUBLANE, LANE), 0)
    meta = jnp.where(sub == 0, units, jnp.where(sub == 1, unit_off, 0.0))
    meta_ref[...] = meta.astype(_I32)


def _mix(x, mod3, att, cvn, sg, wa, wc, wo, g_post, g_ffn, wr_hi, wr_lo, b_router_pad, layer):
    full = lambda shape: pl.BlockSpec(shape, lambda i: (0,) * len(shape))
    return pl.pallas_call(
        _mix_kernel,
        grid=(N_BLOCKS,),
        in_specs=[
            pl.BlockSpec((TB, D_MODEL), lambda i: (i, 0)),
            pl.BlockSpec((None, 1, 6 * D_MODEL),
                         lambda i: (layer * COND_ROWS + _cond_index(i, TB), 0, 0)),
            pl.BlockSpec((TB, Q_DIM), lambda i: (i, 0)),
            pl.BlockSpec((TB, C_CONV), lambda i: (i, 0)),
            pl.BlockSpec((TB, 2 * D_MODEL), lambda i: (i, 0)),
            full((Q_DIM, D_MODEL)),
            full((C_CONV, D_MODEL)),
            full((D_MODEL, D_MODEL)),
            full((1, D_MODEL)),
            full((1, D_MODEL)),
            full((D_MODEL, LANE)),
            full((D_MODEL, LANE)),
            full((1, LANE)),
        ],
        out_specs=[
            pl.BlockSpec((TB, D_MODEL), lambda i: (i, 0)),
            pl.BlockSpec((SLOTS, D_MODEL), lambda i: (i, 0)),
            pl.BlockSpec((TB, SLOTS), lambda i: (i, 0)),
            pl.BlockSpec((None, SUBLANE, LANE), lambda i: (i, 0, 0)),
        ],
        out_shape=[
            jax.ShapeDtypeStruct((N_TOK, D_MODEL), _F32),
            jax.ShapeDtypeStruct((N_BLOCKS * SLOTS, D_MODEL), _F32),
            jax.ShapeDtypeStruct((N_TOK, SLOTS), _BF16),
            jax.ShapeDtypeStruct((N_BLOCKS, SUBLANE, LANE), _I32),
        ],
        compiler_params=pltpu.CompilerParams(
            dimension_semantics=("arbitrary",), vmem_limit_bytes=VMEM_LIMIT),
        name="mix_router",
    )(x, mod3, att, cvn, sg, wa, wc, wo, g_post, g_ffn, wr_hi, wr_lo, b_router_pad)


def _plan(meta):
    units = meta[:, 0, :N_EXPERTS]
    seg_off = meta[:, 1, :N_EXPERTS]
    tiles = (jnp.sum(units, axis=0) + TILE_UNITS - 1) // TILE_UNITS
    tile_end = jnp.cumsum(tiles)
    n_tiles = tile_end[-1]
    region = (tile_end - tiles) * TILE_UNITS
    dst = region[None, :] + jnp.cumsum(units, axis=0) - units
    src = jnp.arange(N_BLOCKS, dtype=_I32)[:, None] * BLOCK_UNITS + seg_off

    tile_first = jnp.concatenate([jnp.zeros((1,), _I32), tile_end.astype(_I32)])

    dst_f, len_f, src_f = dst.reshape(1, -1), units.reshape(1, -1), src.reshape(1, -1)
    d = jnp.arange(N_ETILES * TILE_UNITS, dtype=_I32)[:, None]
    in_seg = (dst_f <= d) & (d < dst_f + len_f)
    src_unit = jnp.sum(jnp.where(in_seg, src_f + d - dst_f, 0), axis=1)

    u = jnp.arange(BLOCK_UNITS, dtype=_I32)[None, :, None]
    so, un = seg_off[:, None, :], units[:, None, :]
    in_blk = (so <= u) & (u < so + un)
    back_unit = jnp.sum(jnp.where(in_blk, dst[:, None, :] + u - so, 0), axis=2)
    return (tile_first, n_tiles.reshape(1).astype(_I32), src_unit.astype(_I32),
            back_unit.reshape(-1).astype(_I32))


def _unit_gather(src_hbm, unit_ref, first, n_units, dst_buf, sem):
    for i in range(n_units):
        row = pl.multiple_of(unit_ref[first + i] * UNIT, UNIT)
        pltpu.make_async_copy(src_hbm.at[pl.ds(row, UNIT), :],
                              dst_buf.at[pl.ds(i * UNIT, UNIT), :], sem).start(priority=GATHER_PRIORITY)


def _unit_gather_wait(src_hbm, n_units, dst_buf, sem):
    pltpu.make_async_copy(src_hbm.at[pl.ds(0, n_units * UNIT), :], dst_buf, sem).wait()


def _tile_write(ybuf_slot, ys_hbm, tile, sem):
    row = pl.multiple_of(tile * TE, TE)
    return pltpu.make_async_copy(ybuf_slot, ys_hbm.at[pl.ds(row, TE), :], sem)


def _expert_kernel(first_ref, nt_ref, src_ref, xp_hbm, wgu_ref, bgu_ref, wd_ref, bd_ref,
                   ys_hbm, wgu_bf, wd_bf, xbuf, ybuf, xsem, ysem):
    e = pl.program_id(0)
    n_live = nt_ref[0]
    t_lo = first_ref[e]
    t_hi = first_ref[e + 1]

    @pl.when(e == 0)
    def _():
        _unit_gather(xp_hbm, src_ref, 0, TILE_UNITS, xbuf.at[0], xsem.at[0])

    @pl.when(t_hi > t_lo)
    def _():
        wgu_bf[...] = wgu_ref[...].astype(_BF16)
        wd_bf[...] = wd_ref[...].astype(_BF16)

    def tile_body(t, carry):
        slot = t % 2

        @pl.when(t + 1 < n_live)
        def _():
            _unit_gather(xp_hbm, src_ref, (t + 1) * TILE_UNITS, TILE_UNITS,
                         xbuf.at[1 - slot], xsem.at[1 - slot])

        _unit_gather_wait(xp_hbm, TILE_UNITS, xbuf.at[slot], xsem.at[slot])
        gu = jnp.dot(xbuf[slot].astype(_BF16), wgu_bf[...],
                     preferred_element_type=_F32) + bgu_ref[...]
        gate = jnp.minimum(gu[:, :D_FF], SWIGLU_LIMIT)
        lin = jnp.clip(gu[:, D_FF:], -SWIGLU_LIMIT, SWIGLU_LIMIT)
        act = gate * jax.nn.sigmoid(SWIGLU_ALPHA * gate) * (lin + 1.0)
        y = jnp.dot(act.astype(_BF16), wd_bf[...], preferred_element_type=_F32) + bd_ref[...]

        @pl.when(t >= 2)
        def _():
            _tile_write(ybuf.at[slot], ys_hbm, t - 2, ysem.at[slot]).wait()

        ybuf[slot] = y
        _tile_write(ybuf.at[slot], ys_hbm, t, ysem.at[slot]).start(priority=GATHER_PRIORITY)
        return carry

    lax.fori_loop(t_lo, t_hi, tile_body, 0)

    @pl.when(e == N_EXPERTS - 1)
    def _():
        for back in (2, 1):
            @pl.when(n_live >= back)
            def _():
                t = n_live - back
                _tile_write(ybuf.at[t % 2], ys_hbm, t, ysem.at[t % 2]).wait()

        ybuf[0] = jnp.zeros((TE, D_MODEL), _F32)

        def zero_start(t, carry):
            _tile_write(ybuf.at[0], ys_hbm, t, ysem.at[0]).start()
            return carry

        def zero_wait(t, carry):
            _tile_write(ybuf.at[0], ys_hbm, t, ysem.at[0]).wait()
            return carry

        lax.fori_loop(n_live, N_ETILES, zero_start, 0)
        lax.fori_loop(n_live, N_ETILES, zero_wait, 0)


def _experts(tile_first, n_tiles, src_unit, xp, w_gate_up, b_gate_up, w_down, b_down, layer):
    grid_spec = pltpu.PrefetchScalarGridSpec(
        num_scalar_prefetch=3,
        grid=(N_EXPERTS,),
        in_specs=[
            pl.BlockSpec(memory_space=pl.ANY),
            pl.BlockSpec((None, None, D_MODEL, 2 * D_FF), lambda e, tf, nt, su: (layer, e, 0, 0)),
            pl.BlockSpec((None, None, 1, 2 * D_FF), lambda e, tf, nt, su: (layer, e, 0, 0)),
            pl.BlockSpec((None, None, D_FF, D_MODEL), lambda e, tf, nt, su: (layer, e, 0, 0)),
            pl.BlockSpec((None, None, 1, D_MODEL), lambda e, tf, nt, su: (layer, e, 0, 0)),
        ],
        out_specs=pl.BlockSpec(memory_space=pl.ANY),
        scratch_shapes=[
            pltpu.VMEM((D_MODEL, 2 * D_FF), _BF16),
            pltpu.VMEM((D_FF, D_MODEL), _BF16),
            pltpu.VMEM((2, TE, D_MODEL), _F32),
            pltpu.VMEM((2, TE, D_MODEL), _F32),
            pltpu.SemaphoreType.DMA((2,)),
            pltpu.SemaphoreType.DMA((2,)),
        ],
    )
    return pl.pallas_call(
        _expert_kernel,
        grid_spec=grid_spec,
        out_shape=jax.ShapeDtypeStruct((N_ETILES * TE, D_MODEL), _F32),
        compiler_params=pltpu.CompilerParams(
            dimension_semantics=("arbitrary",), vmem_limit_bytes=VMEM_LIMIT),
        name="experts",
    )(tile_first, n_tiles, src_unit, xp, w_gate_up,
      b_gate_up.reshape(DEPTH, N_EXPERTS, 1, 2 * D_FF), w_down,
      b_down.reshape(DEPTH, N_EXPERTS, 1, D_MODEL))


def _combine_kernel(back_ref, ys_hbm, x_ref, mod_ref, qw_ref, g_ref, o_ref, ybuf, sem):
    b = pl.program_id(0)
    slot = b % 2

    @pl.when(b == 0)
    def _():
        _unit_gather(ys_hbm, back_ref, 0, BLOCK_UNITS, ybuf.at[0], sem.at[0])

    @pl.when(b + 1 < N_BLOCKS)
    def _():
        _unit_gather(ys_hbm, back_ref, (b + 1) * BLOCK_UNITS, BLOCK_UNITS,
                     ybuf.at[1 - slot], sem.at[1 - slot])

    _unit_gather_wait(ys_hbm, BLOCK_UNITS, ybuf.at[slot], sem.at[slot])
    moe = jnp.dot(qw_ref[...], ybuf[slot].astype(_BF16), preferred_element_type=_F32)
    gt2 = mod_ref[:, 5 * D_MODEL:6 * D_MODEL]
    o_ref[...] = x_ref[...] + gt2 * _rms(moe, g_ref[...])


def _combine(back_unit, ys, x1, mod3, qw, g_post, layer):
    grid_spec = pltpu.PrefetchScalarGridSpec(
        num_scalar_prefetch=1,
        grid=(N_BLOCKS,),
        in_specs=[
            pl.BlockSpec(memory_space=pl.ANY),
            pl.BlockSpec((TB, D_MODEL), lambda b, bu: (b, 0)),
            pl.BlockSpec((None, 1, 6 * D_MODEL),
                         lambda b, bu: (layer * COND_ROWS + _cond_index(b, TB), 0, 0)),
            pl.BlockSpec((TB, SLOTS), lambda b, bu: (b, 0)),
            pl.BlockSpec((1, D_MODEL), lambda b, bu: (0, 0)),
        ],
        out_specs=pl.BlockSpec((TB, D_MODEL), lambda b, bu: (b, 0)),
        scratch_shapes=[
            pltpu.VMEM((2, SLOTS, D_MODEL), _F32),
            pltpu.SemaphoreType.DMA((2,)),
        ],
    )
    return pl.pallas_call(
        _combine_kernel,
        grid_spec=grid_spec,
        out_shape=jax.ShapeDtypeStruct((N_TOK, D_MODEL), _F32),
        compiler_params=pltpu.CompilerParams(
            dimension_semantics=("arbitrary",), vmem_limit_bytes=VMEM_LIMIT),
        name="combine_residual",
    )(back_unit, ys, x1, mod3, qw, g_post)


def _rope_tables():
    pos = jnp.arange(DEC_SEQ)
    row = (pos // GRID_W).astype(_F32)
    col = (pos % GRID_W).astype(_F32)
    inv = ROPE_THETA ** (-jnp.arange(ROPE_FREQS, dtype=_F32) / ROPE_FREQS)
    ang_r = row[:, None] * inv[None, :]
    ang_c = col[:, None] * inv[None, :]
    cos = jnp.concatenate([jnp.cos(ang_r)] * 2 + [jnp.cos(ang_c)] * 2, axis=-1)
    sin = jnp.concatenate([-jnp.sin(ang_r), jnp.sin(ang_r), -jnp.sin(ang_c), jnp.sin(ang_c)], axis=-1)
    reps = LANE // HEAD_DIM
    return jnp.tile(cos, (1, reps)), jnp.tile(sin, (1, reps))


def kernel(x_prompt, x_sample, cache_k, cache_v, c, c_ctx, w_ada, b_ada, g_pre_mix, g_post_mix,
           g_pre_ffn, g_post_ffn, w_in, attn_sink, w_attn_o, conv_w, conv_b, conv_ln_g, conv_ln_b,
           w_conv_o, w_out, w_router, b_router, w_gate_up, b_gate_up, w_down, b_down):
    x = jnp.concatenate([x_prompt.reshape(N_CTX, D_MODEL), x_sample.reshape(N_LAT, D_MODEL)], axis=0)
    cond = jnp.concatenate([c_ctx[None, :], c, jnp.zeros((COND_ROWS - N_COND, D_MODEL), _F32)], axis=0)
    mod = _modulation(cond, w_ada, b_ada)
    mod3 = mod.reshape(DEPTH * COND_ROWS, 1, 6 * D_MODEL)
    cos_t, sin_t = _rope_tables()
    ck = cache_k.reshape(DEC_BATCH, DEPTH, PAST_LEN, KV_DIM)
    cv = cache_v.reshape(DEC_BATCH, DEPTH, PAST_LEN, KV_DIM)
    w_router_pad = jnp.pad(w_router, ((0, 0), (0, 0), (0, LANE - N_EXPERTS)))
    wr_hi = w_router_pad.astype(_BF16)
    wr_lo = (w_router_pad - wr_hi.astype(_F32)).astype(_BF16)
    b_router_pad = jnp.pad(b_router, ((0, 0), (0, LANE - N_EXPERTS)), constant_values=-jnp.inf)

    new_k, new_v = [], []
    for l in range(DEPTH):
        row = lambda a: a[l][None, :]
        q, kv, u, sg = _inproj(x, mod3, row(g_pre_mix), w_in[l].astype(_BF16), cos_t, sin_t, l)
        new_k.append(kv[:N_CTX, :KV_DIM].reshape(BATCH, SEQ, N_KV_HEADS, HEAD_DIM))
        new_v.append(kv[:N_CTX, KV_DIM:].reshape(BATCH, SEQ, N_KV_HEADS, HEAD_DIM))
        att = _attention(q, kv, ck, cv, attn_sink[l], l)
        cvn = _conv_branch(u, conv_w[l], row(conv_b), row(conv_ln_g), row(conv_ln_b))
        x1, xp, qw, meta = _mix(
            x, mod3, att, cvn, sg, w_attn_o[l].astype(_BF16), w_conv_o[l].astype(_BF16),
            w_out[l].astype(_BF16), row(g_post_mix), row(g_pre_ffn), wr_hi[l], wr_lo[l],
            row(b_router_pad), l)
        tile_first, n_tiles, src_unit, back_unit = _plan(meta)
        ys = _experts(tile_first, n_tiles, src_unit, xp, w_gate_up, b_gate_up, w_down, b_down, l)
        x = _combine(back_unit, ys, x1, mod3, qw, row(g_post_ffn), l)

    y_prompt = x[:N_CTX].reshape(BATCH, SEQ, D_MODEL)
    y_sample = x[N_CTX:].reshape(DEC_BATCH, DEC_SEQ, D_MODEL)
    return (y_prompt, y_sample, jnp.stack(new_k, axis=1), jnp.stack(new_v, axis=1))
```

```python
import functools

import jax
import jax.numpy as jnp
from jax import lax
from jax.experimental import pallas as pl
from jax.experimental.pallas import tpu as pltpu

D_MODEL = 1024
BATCH = 16
SEQ = 256
DEPTH = 2
DEC_BATCH = 2
DEC_SEQ = 2048
PAST_LEN = 256
GRID_W = 64
N_HEADS = 16
N_KV_HEADS = 4
GROUP = N_HEADS // N_KV_HEADS
HEAD_DIM = 64
Q_DIM = N_HEADS * HEAD_DIM
KV_DIM = N_KV_HEADS * HEAD_DIM
WINDOW = 128
ATTN_SCALE = HEAD_DIM ** -0.5
ROPE_THETA = 10000.0
ROPE_HALF = HEAD_DIM // 2
ROPE_FREQS = ROPE_HALF // 2
C_CONV = D_MODEL // 2
CONV_WIDTH = 31
CONV_PAD = (CONV_WIDTH - 1) // 2
N_EXPERTS = 32
TOP_K = 4
D_FF = D_MODEL
SWIGLU_LIMIT = 7.0
SWIGLU_ALPHA = 1.702
EPS = 1e-6
IN_COLS = Q_DIM + 2 * KV_DIM + 2 * C_CONV + 2 * D_MODEL

N_CTX = BATCH * SEQ
N_LAT = DEC_BATCH * DEC_SEQ
N_TOK = N_CTX + N_LAT
N_COND = 1 + DEC_BATCH
COND_ROWS = 8

LANE = 128
SUBLANE = 8
TM = 512
TQ = 256
KWIN = TQ + 2 * WINDOW
assert GROUP == 4 and 2 * HEAD_DIM == LANE
LOG2E = 1.4426950408889634
QK_SCALE = ATTN_SCALE * LOG2E
HALO = 16
CONV_SPAN = SEQ + ((HALO - CONV_PAD + CONV_WIDTH - 1) // SUBLANE) * SUBLANE
assert CONV_SPAN + SUBLANE - 1 <= SEQ + 2 * HALO

TB = 256
N_BLOCKS = N_TOK // TB
UNIT = SUBLANE
SLOTS = 1280
BLOCK_UNITS = SLOTS // UNIT
TE = 256
TILE_UNITS = TE // UNIT
N_ETILES = (N_BLOCKS * BLOCK_UNITS) // TILE_UNITS + N_EXPERTS
W_CHUNKS = 4
W_ROWS = D_MODEL // W_CHUNKS
assert D_FF == D_MODEL
GATHER_PRIORITY = 1
VMEM_LIMIT = 56 * 1024 * 1024

assert SLOTS >= TB * TOP_K + N_EXPERTS * (UNIT - 1) and SLOTS % UNIT == 0

_F32 = jnp.float32
_BF16 = jnp.bfloat16
_I32 = jnp.int32


def _rms(x, g):
    return x * lax.rsqrt(jnp.mean(x * x, axis=-1, keepdims=True) + EPS) * g


def _cond_index(i, tile):
    n_ctx_tiles = N_CTX // tile
    return jnp.where(i < n_ctx_tiles, 0, 1 + (i - n_ctx_tiles) // (DEC_SEQ // tile))


def _mod_kernel(cond_ref, w_ref, b_ref, out_ref):
    cnd = cond_ref[...]
    s = cnd * jax.nn.sigmoid(cnd)
    out_ref[...] = jnp.dot(s, w_ref[...], precision=lax.Precision.HIGHEST,
                           preferred_element_type=_F32) + b_ref[...]


def _modulation(cond, w_ada, b_ada):
    tn = 1536
    nt = 6 * D_MODEL // tn
    return pl.pallas_call(
        _mod_kernel,
        grid=(DEPTH, nt),
        in_specs=[
            pl.BlockSpec((COND_ROWS, D_MODEL), lambda l, n: (0, 0)),
            pl.BlockSpec((None, D_MODEL, tn), lambda l, n: (l, 0, n)),
            pl.BlockSpec((None, 1, tn), lambda l, n: (l, 0, n)),
        ],
        out_specs=pl.BlockSpec((None, COND_ROWS, tn), lambda l, n: (l, 0, n)),
        out_shape=jax.ShapeDtypeStruct((DEPTH, COND_ROWS, 6 * D_MODEL), _F32),
        compiler_params=pltpu.CompilerParams(
            dimension_semantics=("arbitrary", "arbitrary"), vmem_limit_bytes=VMEM_LIMIT),
        name="modulation",
    )(cond, w_ada, b_ada.reshape(DEPTH, 1, 6 * D_MODEL))


def _rope_chunk(x, cos, sin):
    lane = lax.broadcasted_iota(_I32, x.shape, 1)
    partner = jnp.where((lane & ROPE_FREQS) == 0,
                        pltpu.roll(x, LANE - ROPE_FREQS, 1), pltpu.roll(x, ROPE_FREQS, 1))
    return x * cos + partner * sin


def _inproj_kernel(x_ref, mod_ref, g_ref, w_ref, cos_ref, sin_ref,
                   q_ref, kv_ref, u_ref, sg_ref):
    i = pl.program_id(0)
    x = x_ref[...]
    sh = mod_ref[:, 0:D_MODEL]
    sc = mod_ref[:, D_MODEL:2 * D_MODEL]
    h = (_rms(x, g_ref[...]) * (1.0 + sc) + sh).astype(_BF16)

    c0 = 0
    q = jnp.dot(h, w_ref[:, c0:c0 + Q_DIM], preferred_element_type=_F32) * QK_SCALE
    c0 += Q_DIM
    kv = jnp.dot(h, w_ref[:, c0:c0 + 2 * KV_DIM], preferred_element_type=_F32)
    c0 += 2 * KV_DIM
    ua = jnp.dot(h, w_ref[:, c0:c0 + C_CONV], preferred_element_type=_F32)
    c0 += C_CONV
    ub = jnp.dot(h, w_ref[:, c0:c0 + C_CONV], preferred_element_type=_F32)
    c0 += C_CONV
    g = jnp.dot(h, w_ref[:, c0:c0 + 2 * D_MODEL], preferred_element_type=_F32)

    u_ref[...] = ua * jax.nn.sigmoid(ub)
    sg_ref[...] = jax.nn.sigmoid(g).astype(_BF16)

    is_latent = i >= N_CTX // TM

    @pl.when(jnp.logical_not(is_latent))
    def _():
        q_ref[...] = q.astype(_BF16)
        kv_ref[...] = kv

    @pl.when(is_latent)
    def _():
        cos = cos_ref[...]
        sin = sin_ref[...]
        for j in range(Q_DIM // LANE):
            sl = slice(j * LANE, (j + 1) * LANE)
            q_ref[:, sl] = _rope_chunk(q[:, sl], cos, sin).astype(_BF16)
        for j in range(KV_DIM // LANE):
            sl = slice(j * LANE, (j + 1) * LANE)
            kv_ref[:, sl] = _rope_chunk(kv[:, sl], cos, sin)
        kv_ref[:, KV_DIM:] = kv[:, KV_DIM:]


def _inproj(x, mod3, g_pre, w_in_bf, cos_t, sin_t, layer):
    n_ctx_tiles = N_CTX // TM
    lat_tiles = DEC_SEQ // TM

    def rope_map(i):
        return (jnp.where(i < n_ctx_tiles, 0, (i - n_ctx_tiles) % lat_tiles), 0)

    return pl.pallas_call(
        _inproj_kernel,
        grid=(N_TOK // TM,),
        in_specs=[
            pl.BlockSpec((TM, D_MODEL), lambda i: (i, 0)),
            pl.BlockSpec((None, 1, 6 * D_MODEL),
                         lambda i: (layer * COND_ROWS + _cond_index(i, TM), 0, 0)),
            pl.BlockSpec((1, D_MODEL), lambda i: (0, 0)),
            pl.BlockSpec((D_MODEL, IN_COLS), lambda i: (0, 0)),
            pl.BlockSpec((TM, LANE), rope_map),
            pl.BlockSpec((TM, LANE), rope_map),
        ],
        out_specs=[
            pl.BlockSpec((TM, Q_DIM), lambda i: (i, 0)),
            pl.BlockSpec((TM, 2 * KV_DIM), lambda i: (i, 0)),
            pl.BlockSpec((TM, C_CONV), lambda i: (i, 0)),
            pl.BlockSpec((TM, 2 * D_MODEL), lambda i: (i, 0)),
        ],
        out_shape=[
            jax.ShapeDtypeStruct((N_TOK, Q_DIM), _BF16),
            jax.ShapeDtypeStruct((N_TOK, 2 * KV_DIM), _F32),
            jax.ShapeDtypeStruct((N_TOK, C_CONV), _F32),
            jax.ShapeDtypeStruct((N_TOK, 2 * D_MODEL), _BF16),
        ],
        compiler_params=pltpu.CompilerParams(
            dimension_semantics=("arbitrary",), vmem_limit_bytes=VMEM_LIMIT),
        name="inproj",
    )(x, mod3, g_pre, w_in_bf, cos_t, sin_t)


def _pair_operands(k, v):
    zero = jnp.zeros_like(k)
    one = jnp.ones_like(v)
    ka = jnp.concatenate([k, zero], axis=1).astype(_BF16)
    kb = jnp.concatenate([zero, k], axis=1).astype(_BF16)
    va = jnp.concatenate([v, zero, one, zero], axis=1).astype(_BF16)
    vb = jnp.concatenate([zero, v, zero, one], axis=1).astype(_BF16)
    return ka, kb, va, vb


def _pair_attend(qq, operands, masks, sink_a, sink_b):
    def scores(which):
        out = []
        for ops, mask in zip(operands, masks):
            s = lax.dot_general(qq, ops[which], (((1,), (1,)), ((), ())),
                                preferred_element_type=_F32)
            out.append(s if mask is None else jnp.where(mask, s, -jnp.inf))
        return out

    acc = jnp.zeros((qq.shape[0], 2 * LANE), _F32)
    sink_terms = []
    for which, sink in ((0, sink_a), (1, sink_b)):
        ss = scores(which)
        m = sink
        for s in ss:
            m = jnp.maximum(m, jnp.max(s, axis=-1, keepdims=True))
        for s, ops in zip(ss, operands):
            acc = acc + jnp.dot(jnp.exp2(s - m).astype(_BF16), ops[2 + which],
                                preferred_element_type=_F32)
        sink_terms.append(jnp.exp2(sink - m))
    lane = lax.broadcasted_iota(_I32, (qq.shape[0], LANE), 1)
    sink_term = jnp.where(lane < HEAD_DIM, sink_terms[0], sink_terms[1])
    return acc[:, :LANE] / (acc[:, LANE:] + sink_term)


def _group_attend(q_ref, o_ref, hk, operands, masks, sink_ref):
    pairs = [slice((2 * hk + j) * LANE, (2 * hk + j + 1) * LANE) for j in range(GROUP // 2)]
    qq = jnp.concatenate([q_ref[:, sl] for sl in pairs], axis=0)
    row = lax.broadcasted_iota(_I32, (qq.shape[0], 1), 0)
    first = row < TQ
    sink_a = jnp.where(first, sink_ref[GROUP * hk], sink_ref[GROUP * hk + 2]) * LOG2E
    sink_b = jnp.where(first, sink_ref[GROUP * hk + 1], sink_ref[GROUP * hk + 3]) * LOG2E
    out = _pair_attend(qq, operands, masks, sink_a, sink_b)
    for j, sl in enumerate(pairs):
        o_ref[:, sl] = out[j * TQ:(j + 1) * TQ].astype(_BF16)


def _attn_kernel(sink_ref, q_ref, kv_own_ref, kv_seq_ref, ck_ref, cv_ref, o_ref):
    i = pl.program_id(0)
    n_ctx_steps = N_CTX // TQ

    @pl.when(i < n_ctx_steps)
    def _():
        for hk in range(N_KV_HEADS):
            ks = slice(hk * HEAD_DIM, (hk + 1) * HEAD_DIM)
            vs = slice(KV_DIM + hk * HEAD_DIM, KV_DIM + (hk + 1) * HEAD_DIM)
            own = _pair_operands(kv_own_ref[:, ks], kv_own_ref[:, vs])
            _group_attend(q_ref, o_ref, hk, [own], [None], sink_ref)

    @pl.when(i >= n_ctx_steps)
    def _():
        qb = (i - n_ctx_steps) % (DEC_SEQ // TQ)
        q_start = qb * TQ
        k_start = pl.multiple_of(jnp.clip(q_start - WINDOW, 0, DEC_SEQ - KWIN), WINDOW)
        stacked = (GROUP // 2) * TQ
        qpos = q_start + lax.broadcasted_iota(_I32, (stacked, KWIN), 0) % TQ
        kpos = k_start + lax.broadcasted_iota(_I32, (stacked, KWIN), 1)
        mask = jnp.abs(kpos - qpos) <= WINDOW
        for hk in range(N_KV_HEADS):
            ks = slice(hk * HEAD_DIM, (hk + 1) * HEAD_DIM)
            vs = slice(KV_DIM + hk * HEAD_DIM, KV_DIM + (hk + 1) * HEAD_DIM)
            local = _pair_operands(kv_seq_ref[pl.ds(k_start, KWIN), ks],
                                   kv_seq_ref[pl.ds(k_start, KWIN), vs])
            cached = _pair_operands(ck_ref[:, ks], cv_ref[:, ks])
            _group_attend(q_ref, o_ref, hk, [local, cached], [mask, None], sink_ref)


def _attention(q, kv, cache_k, cache_v, sink, layer):
    n_ctx_steps = N_CTX // TQ
    nq = DEC_SEQ // TQ
    kv_off = N_CTX // DEC_SEQ

    def lat_batch(i):
        return jnp.maximum(i - n_ctx_steps, 0) // nq

    return pl.pallas_call(
        _attn_kernel,
        grid=(N_TOK // TQ,),
        in_specs=[
            pl.BlockSpec(memory_space=pltpu.SMEM),
            pl.BlockSpec((TQ, Q_DIM), lambda i: (i, 0)),
            pl.BlockSpec((TQ, 2 * KV_DIM), lambda i: (i, 0)),
            pl.BlockSpec((DEC_SEQ, 2 * KV_DIM), lambda i: (kv_off + lat_batch(i), 0)),
            pl.BlockSpec((None, None, PAST_LEN, KV_DIM), lambda i: (lat_batch(i), layer, 0, 0)),
            pl.BlockSpec((None, None, PAST_LEN, KV_DIM), lambda i: (lat_batch(i), layer, 0, 0)),
        ],
        out_specs=pl.BlockSpec((TQ, Q_DIM), lambda i: (i, 0)),
        out_shape=jax.ShapeDtypeStruct((N_TOK, Q_DIM), _BF16),
        compiler_params=pltpu.CompilerParams(
            dimension_semantics=("arbitrary",), vmem_limit_bytes=VMEM_LIMIT),
        name="attention",
    )(sink, q, kv, kv, cache_k, cache_v)


def _conv_kernel(prev_ref, cur_ref, next_ref, w_ref, b_ref, lg_ref, lb_ref, y_ref, pad_ref, sh_ref):
    i = pl.program_id(0)
    n_ctx_tiles = N_CTX // SEQ
    tiles_per_seq = jnp.where(i < n_ctx_tiles, 1, DEC_SEQ // SEQ)
    j = jnp.where(i < n_ctx_tiles, 0, (i - n_ctx_tiles) % (DEC_SEQ // SEQ))
    pad_ref[0:HALO, :] = jnp.where(j > 0, prev_ref[...], 0.0)
    pad_ref[HALO:HALO + SEQ, :] = cur_ref[...]
    pad_ref[HALO + SEQ:HALO + SEQ + HALO, :] = jnp.where(j < tiles_per_seq - 1, next_ref[...], 0.0)

    for r in range(SUBLANE):
        sh_ref[r] = pad_ref[r:r + CONV_SPAN, :]

    rows = 64
    for r0 in range(0, SEQ, rows):
        acc = jnp.zeros((rows, C_CONV), _F32) + b_ref[...]
        for t in range(CONV_WIDTH):
            off = HALO - CONV_PAD + t
            start = (off // SUBLANE) * SUBLANE + r0
            acc = acc + sh_ref[off % SUBLANE, start:start + rows, :] * w_ref[t:t + 1, :]
        mu = jnp.mean(acc, axis=-1, keepdims=True)
        d = acc - mu
        var = jnp.mean(d * d, axis=-1, keepdims=True)
        y = d * lax.rsqrt(var + EPS) * lg_ref[...] + lb_ref[...]
        y_ref[r0:r0 + rows, :] = (y * jax.nn.sigmoid(y)).astype(_BF16)


def _conv_branch(u, conv_w, conv_b, ln_g, ln_b):
    n_tiles = N_TOK // SEQ
    hb = SEQ // HALO
    last = N_TOK // HALO - 1
    return pl.pallas_call(
        _conv_kernel,
        grid=(n_tiles,),
        in_specs=[
            pl.BlockSpec((HALO, C_CONV), lambda i: (jnp.maximum(i * hb - 1, 0), 0)),
            pl.BlockSpec((SEQ, C_CONV), lambda i: (i, 0)),
            pl.BlockSpec((HALO, C_CONV), lambda i: (jnp.minimum((i + 1) * hb, last), 0)),
            pl.BlockSpec((CONV_WIDTH, C_CONV), lambda i: (0, 0)),
            pl.BlockSpec((1, C_CONV), lambda i: (0, 0)),
            pl.BlockSpec((1, C_CONV), lambda i: (0, 0)),
            pl.BlockSpec((1, C_CONV), lambda i: (0, 0)),
        ],
        out_specs=pl.BlockSpec((SEQ, C_CONV), lambda i: (i, 0)),
        out_shape=jax.ShapeDtypeStruct((N_TOK, C_CONV), _BF16),
        scratch_shapes=[pltpu.VMEM((SEQ + 2 * HALO, C_CONV), _F32),
                        pltpu.VMEM((SUBLANE, CONV_SPAN, C_CONV), _F32)],
        compiler_params=pltpu.CompilerParams(
            dimension_semantics=("arbitrary",), vmem_limit_bytes=VMEM_LIMIT),
        name="conv_branch",
    )(u, u, u, conv_w, conv_b, ln_g, ln_b)


def _mix_kernel(x_ref, mod_ref, att_ref, cv_ref, sg_ref, wa_ref, wc_ref, wo_ref,
                gpost_ref, gffn_ref, wr_ref, wrlo_ref, br_ref,
                x1_ref, xp_ref, qw_ref, meta_ref):
    a = jnp.dot(att_ref[...], wa_ref[...], preferred_element_type=_F32)
    cv = jnp.dot(cv_ref[...], wc_ref[...], preferred_element_type=_F32)
    m = sg_ref[:, 0:D_MODEL].astype(_F32) * a + sg_ref[:, D_MODEL:].astype(_F32) * cv
    mix = jnp.dot(m.astype(_BF16), wo_ref[...], preferred_element_type=_F32)
    gt1 = mod_ref[:, 2 * D_MODEL:3 * D_MODEL]
    sh2 = mod_ref[:, 3 * D_MODEL:4 * D_MODEL]
    sc2 = mod_ref[:, 4 * D_MODEL:5 * D_MODEL]
    x1 = x_ref[...] + gt1 * _rms(mix, gpost_ref[...])
    x1_ref[...] = x1
    h2 = _rms(x1, gffn_ref[...]) * (1.0 + sc2) + sh2
    h2b = h2.astype(_BF16)

    h2_lo = (h2 - h2b.astype(_F32)).astype(_BF16)
    logits = (jnp.dot(h2b, wr_ref[...], preferred_element_type=_F32)
              + jnp.dot(h2_lo, wr_ref[...], preferred_element_type=_F32)
              + jnp.dot(h2b, wrlo_ref[...], preferred_element_type=_F32) + br_ref[...])
    lane = lax.broadcasted_iota(_I32, (TB, LANE), 1).astype(_F32)
    member = jnp.zeros((TB, LANE), _F32)
    hots, exps = [], []
    top = None
    total = jnp.zeros((TB, 1), _F32)
    for k in range(TOP_K):
        mval = jnp.max(logits, axis=-1, keepdims=True)
        sel = jnp.min(jnp.where(logits == mval, lane, float(LANE)), axis=-1, keepdims=True)
        if top is None:
            top = mval
        e = jnp.exp(mval - top)
        total = total + e
        hot = lane == sel
        hots.append(hot)
        exps.append(e)
        member = member + jnp.where(hot, 1.0, 0.0)
        logits = jnp.where(hot, -jnp.inf, logits)

    r_i = lax.broadcasted_iota(_I32, (TB, TB), 0)
    c_i = lax.broadcasted_iota(_I32, (TB, TB), 1)
    lower = jnp.where(r_i > c_i, 1.0, 0.0).astype(_BF16)
    rank = jnp.dot(lower, member.astype(_BF16), preferred_element_type=_F32)
    count = jnp.sum(member, axis=0, keepdims=True)
    units = jnp.floor((count + float(UNIT - 1)) * (1.0 / UNIT))
    r_l = lax.broadcasted_iota(_I32, (LANE, LANE), 0)
    c_l = lax.broadcasted_iota(_I32, (LANE, LANE), 1)
    upper = jnp.where(r_l < c_l, 1.0, 0.0).astype(_BF16)
    unit_off = jnp.dot(jnp.broadcast_to(units, (SUBLANE, LANE)).astype(_BF16), upper,
                       preferred_element_type=_F32)[0:1, :]
    base = unit_off * float(UNIT) + rank

    slot_lane = lax.broadcasted_iota(_I32, (TB, SLOTS), 1).astype(_F32)
    qw = jnp.zeros((TB, SLOTS), _F32)
    slot_cols = jnp.zeros((TB, LANE), _F32)
    for k in range(TOP_K):
        slot = jnp.sum(jnp.where(hots[k], base, 0.0), axis=-1, keepdims=True)
        qw = qw + jnp.where(slot_lane == slot, exps[k] / total, 0.0)
        slot_cols = jnp.where(lane == float(k), slot, slot_cols)
    qw_ref[...] = qw.astype(_BF16)

    slot_rows = slot_cols.T
    slot_sub = lax.broadcasted_iota(_I32, (SLOTS, TB), 0).astype(_F32)
    perm = jnp.zeros((SLOTS, TB), _F32)
    for k in range(TOP_K):
        perm = perm + jnp.where(slot_sub == slot_rows[k:k + 1, :], 1.0, 0.0)
    perm = perm.astype(_BF16)
    xp_ref[...] = jnp.dot(perm, h2b, preferred_element_type=_F32)

    sub = lax.broadcasted_iota(_I32, (SUBLANE, LANE), 0)
    meta = jnp.where(sub == 0, units, jnp.where(sub == 1, unit_off, 0.0))
    meta_ref[...] = meta.astype(_I32)


def _mix(x, mod3, att, cvn, sg, wa, wc, wo, g_post, g_ffn, wr_hi, wr_lo, b_router_pad, layer):
    full = lambda shape: pl.BlockSpec(shape, lambda i: (0,) * len(shape))
    return pl.pallas_call(
        _mix_kernel,
        grid=(N_BLOCKS,),
        in_specs=[
            pl.BlockSpec((TB, D_MODEL), lambda i: (i, 0)),
            pl.BlockSpec((None, 1, 6 * D_MODEL),
                         lambda i: (layer * COND_ROWS + _cond_index(i, TB), 0, 0)),
            pl.BlockSpec((TB, Q_DIM), lambda i: (i, 0)),
            pl.BlockSpec((TB, C_CONV), lambda i: (i, 0)),
            pl.BlockSpec((TB, 2 * D_MODEL), lambda i: (i, 0)),
            full((Q_DIM, D_MODEL)),
            full((C_CONV, D_MODEL)),
            full((D_MODEL, D_MODEL)),
            full((1, D_MODEL)),
            full((1, D_MODEL)),
            full((D_MODEL, LANE)),
            full((D_MODEL, LANE)),
            full((1, LANE)),
        ],
        out_specs=[
            pl.BlockSpec((TB, D_MODEL), lambda i: (i, 0)),
            pl.BlockSpec((SLOTS, D_MODEL), lambda i: (i, 0)),
            pl.BlockSpec((TB, SLOTS), lambda i: (i, 0)),
            pl.BlockSpec((None, SUBLANE, LANE), lambda i: (i, 0, 0)),
        ],
        out_shape=[
            jax.ShapeDtypeStruct((N_TOK, D_MODEL), _F32),
            jax.ShapeDtypeStruct((N_BLOCKS * SLOTS, D_MODEL), _F32),
            jax.ShapeDtypeStruct((N_TOK, SLOTS), _BF16),
            jax.ShapeDtypeStruct((N_BLOCKS, SUBLANE, LANE), _I32),
        ],
        compiler_params=pltpu.CompilerParams(
            dimension_semantics=("arbitrary",), vmem_limit_bytes=VMEM_LIMIT),
        name="mix_router",
    )(x, mod3, att, cvn, sg, wa, wc, wo, g_post, g_ffn, wr_hi, wr_lo, b_router_pad)


def _plan(meta):
    units = meta[:, 0, :N_EXPERTS]
    seg_off = meta[:, 1, :N_EXPERTS]
    tiles = (jnp.sum(units, axis=0) + TILE_UNITS - 1) // TILE_UNITS
    tile_end = jnp.cumsum(tiles)
    n_tiles = tile_end[-1]
    region = (tile_end - tiles) * TILE_UNITS
    dst = region[None, :] + jnp.cumsum(units, axis=0) - units
    src = jnp.arange(N_BLOCKS, dtype=_I32)[:, None] * BLOCK_UNITS + seg_off

    tile_first = jnp.concatenate([jnp.zeros((1,), _I32), tile_end.astype(_I32)])

    dst_f, len_f, src_f = dst.reshape(1, -1), units.reshape(1, -1), src.reshape(1, -1)
    d = jnp.arange(N_ETILES * TILE_UNITS, dtype=_I32)[:, None]
    in_seg = (dst_f <= d) & (d < dst_f + len_f)
    src_unit = jnp.sum(jnp.where(in_seg, src_f + d - dst_f, 0), axis=1)

    u = jnp.arange(BLOCK_UNITS, dtype=_I32)[None, :, None]
    so, un = seg_off[:, None, :], units[:, None, :]
    in_blk = (so <= u) & (u < so + un)
    back_unit = jnp.sum(jnp.where(in_blk, dst[:, None, :] + u - so, 0), axis=2)
    return (tile_first, n_tiles.reshape(1).astype(_I32), src_unit.astype(_I32),
            back_unit.reshape(-1).astype(_I32))


def _unit_gather(src_hbm, unit_ref, first, n_units, dst_buf, sem):
    for i in range(n_units):
        row = pl.multiple_of(unit_ref[first + i] * UNIT, UNIT)
        pltpu.make_async_copy(src_hbm.at[pl.ds(row, UNIT), :],
                              dst_buf.at[pl.ds(i * UNIT, UNIT), :], sem).start(priority=GATHER_PRIORITY)


def _unit_gather_wait(src_hbm, n_units, dst_buf, sem):
    pltpu.make_async_copy(src_hbm.at[pl.ds(0, n_units * UNIT), :], dst_buf, sem).wait()


def _tile_write(ybuf_slot, ys_hbm, tile, sem):
    row = pl.multiple_of(tile * TE, TE)
    return pltpu.make_async_copy(ybuf_slot, ys_hbm.at[pl.ds(row, TE), :], sem)


def _weight_chunk(wgu_hbm, wd_hbm, layer, expert, chunk, wgu_f32, wd_f32, buf, sem):
    r = pl.multiple_of(chunk * W_ROWS, W_ROWS)
    return (pltpu.make_async_copy(wgu_hbm.at[layer, expert, pl.ds(r, W_ROWS), :],
                                  wgu_f32.at[buf, pl.ds(r, W_ROWS), :], sem),
            pltpu.make_async_copy(wd_hbm.at[layer, expert, pl.ds(r, W_ROWS), :],
                                  wd_f32.at[buf, pl.ds(r, W_ROWS), :], sem))


def _expert_kernel(layer, first_ref, nt_ref, src_ref, xp_hbm, wgu_hbm, bgu_ref, wd_hbm, bd_ref,
                   ys_hbm, wgu_f32, wd_f32, wgu_bf, wd_bf, xbuf, ybuf, wsem, xsem, ysem):
    e = pl.program_id(0)
    n_live = nt_ref[0]
    t_lo = first_ref[e]
    t_hi = first_ref[e + 1]
    buf = e % 2
    has_next = e + 1 < N_EXPERTS

    def start_chunk(expert, chunk, into):
        for cp in _weight_chunk(wgu_hbm, wd_hbm, layer, expert, chunk, wgu_f32, wd_f32,
                                into, wsem.at[into]):
            cp.start()

    @pl.when(e == 0)
    def _():
        _unit_gather(xp_hbm, src_ref, 0, TILE_UNITS, xbuf.at[0], xsem.at[0])
        for c in range(W_CHUNKS):
            start_chunk(0, c, 0)

    pltpu.make_async_copy(wgu_hbm.at[layer, e], wgu_f32.at[buf], wsem.at[buf]).wait()
    pltpu.make_async_copy(wd_hbm.at[layer, e], wd_f32.at[buf], wsem.at[buf]).wait()

    @pl.when(t_hi > t_lo)
    def _():
        wgu_bf[...] = wgu_f32[buf].astype(_BF16)
        wd_bf[...] = wd_f32[buf].astype(_BF16)

    def tile_body(t, carry):
        slot = t % 2

        @pl.when(jnp.logical_and(has_next, t - t_lo < W_CHUNKS))
        def _():
            start_chunk(e + 1, t - t_lo, 1 - buf)

        @pl.when(t + 1 < n_live)
        def _():
            _unit_gather(xp_hbm, src_ref, (t + 1) * TILE_UNITS, TILE_UNITS,
                         xbuf.at[1 - slot], xsem.at[1 - slot])

        _unit_gather_wait(xp_hbm, TILE_UNITS, xbuf.at[slot], xsem.at[slot])
        gu = jnp.dot(xbuf[slot].astype(_BF16), wgu_bf[...],
                     preferred_element_type=_F32) + bgu_ref[...]
        gate = jnp.minimum(gu[:, :D_FF], SWIGLU_LIMIT)
        lin = jnp.clip(gu[:, D_FF:], -SWIGLU_LIMIT, SWIGLU_LIMIT)
        act = gate * jax.nn.sigmoid(SWIGLU_ALPHA * gate) * (lin + 1.0)
        y = jnp.dot(act.astype(_BF16), wd_bf[...], preferred_element_type=_F32) + bd_ref[...]

        @pl.when(t >= 2)
        def _():
            _tile_write(ybuf.at[slot], ys_hbm, t - 2, ysem.at[slot]).wait()

        ybuf[slot] = y
        _tile_write(ybuf.at[slot], ys_hbm, t, ysem.at[slot]).start(priority=GATHER_PRIORITY)
        return carry

    lax.fori_loop(t_lo, t_hi, tile_body, 0)

    for c in range(W_CHUNKS):
        @pl.when(jnp.logical_and(has_next, c >= t_hi - t_lo))
        def _():
            start_chunk(e + 1, c, 1 - buf)

    @pl.when(e == N_EXPERTS - 1)
    def _():
        for back in (2, 1):
            @pl.when(n_live >= back)
            def _():
                t = n_live - back
                _tile_write(ybuf.at[t % 2], ys_hbm, t, ysem.at[t % 2]).wait()

        ybuf[0] = jnp.zeros((TE, D_MODEL), _F32)

        def zero_start(t, carry):
            _tile_write(ybuf.at[0], ys_hbm, t, ysem.at[0]).start()
            return carry

        def zero_wait(t, carry):
            _tile_write(ybuf.at[0], ys_hbm, t, ysem.at[0]).wait()
            return carry

        lax.fori_loop(n_live, N_ETILES, zero_start, 0)
        lax.fori_loop(n_live, N_ETILES, zero_wait, 0)


def _experts(tile_first, n_tiles, src_unit, xp, w_gate_up, b_gate_up, w_down, b_down, layer):
    grid_spec = pltpu.PrefetchScalarGridSpec(
        num_scalar_prefetch=3,
        grid=(N_EXPERTS,),
        in_specs=[
            pl.BlockSpec(memory_space=pl.ANY),
            pl.BlockSpec(memory_space=pl.ANY),
            pl.BlockSpec((None, None, 1, 2 * D_FF), lambda e, tf, nt, su: (layer, e, 0, 0)),
            pl.BlockSpec(memory_space=pl.ANY),
            pl.BlockSpec((None, None, 1, D_MODEL), lambda e, tf, nt, su: (layer, e, 0, 0)),
        ],
        out_specs=pl.BlockSpec(memory_space=pl.ANY),
        scratch_shapes=[
            pltpu.VMEM((2, D_MODEL, 2 * D_FF), _F32),
            pltpu.VMEM((2, D_FF, D_MODEL), _F32),
            pltpu.VMEM((D_MODEL, 2 * D_FF), _BF16),
            pltpu.VMEM((D_FF, D_MODEL), _BF16),
            pltpu.VMEM((2, TE, D_MODEL), _F32),
            pltpu.VMEM((2, TE, D_MODEL), _F32),
            pltpu.SemaphoreType.DMA((2,)),
            pltpu.SemaphoreType.DMA((2,)),
            pltpu.SemaphoreType.DMA((2,)),
        ],
    )
    return pl.pallas_call(
        functools.partial(_expert_kernel, layer),
        grid_spec=grid_spec,
        out_shape=jax.ShapeDtypeStruct((N_ETILES * TE, D_MODEL), _F32),
        compiler_params=pltpu.CompilerParams(
            dimension_semantics=("arbitrary",), vmem_limit_bytes=VMEM_LIMIT),
        name="experts",
    )(tile_first, n_tiles, src_unit, xp, w_gate_up,
      b_gate_up.reshape(DEPTH, N_EXPERTS, 1, 2 * D_FF), w_down,
      b_down.reshape(DEPTH, N_EXPERTS, 1, D_MODEL))


def _combine_kernel(back_ref, ys_hbm, x_ref, mod_ref, qw_ref, g_ref, o_ref, ybuf, sem):
    b = pl.program_id(0)
    slot = b % 2

    @pl.when(b == 0)
    def _():
        _unit_gather(ys_hbm, back_ref, 0, BLOCK_UNITS, ybuf.at[0], sem.at[0])

    @pl.when(b + 1 < N_BLOCKS)
    def _():
        _unit_gather(ys_hbm, back_ref, (b + 1) * BLOCK_UNITS, BLOCK_UNITS,
                     ybuf.at[1 - slot], sem.at[1 - slot])

    _unit_gather_wait(ys_hbm, BLOCK_UNITS, ybuf.at[slot], sem.at[slot])
    moe = jnp.dot(qw_ref[...], ybuf[slot].astype(_BF16), preferred_element_type=_F32)
    gt2 = mod_ref[:, 5 * D_MODEL:6 * D_MODEL]
    o_ref[...] = x_ref[...] + gt2 * _rms(moe, g_ref[...])


def _combine(back_unit, ys, x1, mod3, qw, g_post, layer):
    grid_spec = pltpu.PrefetchScalarGridSpec(
        num_scalar_prefetch=1,
        grid=(N_BLOCKS,),
        in_specs=[
            pl.BlockSpec(memory_space=pl.ANY),
            pl.BlockSpec((TB, D_MODEL), lambda b, bu: (b, 0)),
            pl.BlockSpec((None, 1, 6 * D_MODEL),
                         lambda b, bu: (layer * COND_ROWS + _cond_index(b, TB), 0, 0)),
            pl.BlockSpec((TB, SLOTS), lambda b, bu: (b, 0)),
            pl.BlockSpec((1, D_MODEL), lambda b, bu: (0, 0)),
        ],
        out_specs=pl.BlockSpec((TB, D_MODEL), lambda b, bu: (b, 0)),
        scratch_shapes=[
            pltpu.VMEM((2, SLOTS, D_MODEL), _F32),
            pltpu.SemaphoreType.DMA((2,)),
        ],
    )
    return pl.pallas_call(
        _combine_kernel,
        grid_spec=grid_spec,
        out_shape=jax.ShapeDtypeStruct((N_TOK, D_MODEL), _F32),
        compiler_params=pltpu.CompilerParams(
            dimension_semantics=("arbitrary",), vmem_limit_bytes=VMEM_LIMIT),
        name="combine_residual",
    )(back_unit, ys, x1, mod3, qw, g_post)


def _rope_tables():
    pos = jnp.arange(DEC_SEQ)
    row = (pos // GRID_W).astype(_F32)
    col = (pos % GRID_W).astype(_F32)
    inv = ROPE_THETA ** (-jnp.arange(ROPE_FREQS, dtype=_F32) / ROPE_FREQS)
    ang_r = row[:, None] * inv[None, :]
    ang_c = col[:, None] * inv[None, :]
    cos = jnp.concatenate([jnp.cos(ang_r)] * 2 + [jnp.cos(ang_c)] * 2, axis=-1)
    sin = jnp.concatenate([-jnp.sin(ang_r), jnp.sin(ang_r), -jnp.sin(ang_c), jnp.sin(ang_c)], axis=-1)
    reps = LANE // HEAD_DIM
    return jnp.tile(cos, (1, reps)), jnp.tile(sin, (1, reps))


def kernel(x_prompt, x_sample, cache_k, cache_v, c, c_ctx, w_ada, b_ada, g_pre_mix, g_post_mix,
           g_pre_ffn, g_post_ffn, w_in, attn_sink, w_attn_o, conv_w, conv_b, conv_ln_g, conv_ln_b,
           w_conv_o, w_out, w_router, b_router, w_gate_up, b_gate_up, w_down, b_down):
    x = jnp.concatenate([x_prompt.reshape(N_CTX, D_MODEL), x_sample.reshape(N_LAT, D_MODEL)], axis=0)
    cond = jnp.concatenate([c_ctx[None, :], c, jnp.zeros((COND_ROWS - N_COND, D_MODEL), _F32)], axis=0)
    mod = _modulation(cond, w_ada, b_ada)
    mod3 = mod.reshape(DEPTH * COND_ROWS, 1, 6 * D_MODEL)
    cos_t, sin_t = _rope_tables()
    ck = cache_k.reshape(DEC_BATCH, DEPTH, PAST_LEN, KV_DIM)
    cv = cache_v.reshape(DEC_BATCH, DEPTH, PAST_LEN, KV_DIM)
    w_router_pad = jnp.pad(w_router, ((0, 0), (0, 0), (0, LANE - N_EXPERTS)))
    wr_hi = w_router_pad.astype(_BF16)
    wr_lo = (w_router_pad - wr_hi.astype(_F32)).astype(_BF16)
    b_router_pad = jnp.pad(b_router, ((0, 0), (0, LANE - N_EXPERTS)), constant_values=-jnp.inf)

    new_k, new_v = [], []
    for l in range(DEPTH):
        row = lambda a: a[l][None, :]
        q, kv, u, sg = _inproj(x, mod3, row(g_pre_mix), w_in[l].astype(_BF16), cos_t, sin_t, l)
        new_k.append(kv[:N_CTX, :KV_DIM].reshape(BATCH, SEQ, N_KV_HEADS, HEAD_DIM))
        new_v.append(kv[:N_CTX, KV_DIM:].reshape(BATCH, SEQ, N_KV_HEADS, HEAD_DIM))
        att = _attention(q, kv, ck, cv, attn_sink[l], l)
        cvn = _conv_branch(u, conv_w[l], row(conv_b), row(conv_ln_g), row(conv_ln_b))
        x1, xp, qw, meta = _mix(
            x, mod3, att, cvn, sg, w_attn_o[l].astype(_BF16), w_conv_o[l].astype(_BF16),
            w_out[l].astype(_BF16), row(g_post_mix), row(g_pre_ffn), wr_hi[l], wr_lo[l],
            row(b_router_pad), l)
        tile_first, n_tiles, src_unit, back_unit = _plan(meta)
        ys = _experts(tile_first, n_tiles, src_unit, xp, w_gate_up, b_gate_up, w_down, b_down, l)
        x = _combine(back_unit, ys, x1, mod3, qw, row(g_post_ffn), l)

    y_prompt = x[:N_CTX].reshape(BATCH, SEQ, D_MODEL)
    y_sample = x[N_CTX:].reshape(DEC_BATCH, DEC_SEQ, D_MODEL)
    return (y_prompt, y_sample, jnp.stack(new_k, axis=1), jnp.stack(new_v, axis=1))
```

```python
import functools

import jax
import jax.numpy as jnp
from jax import lax
from jax.experimental import pallas as pl
from jax.experimental.pallas import tpu as pltpu

D_MODEL = 1024
BATCH = 16
SEQ = 256
DEPTH = 2
DEC_BATCH = 2
DEC_SEQ = 2048
PAST_LEN = 256
GRID_W = 64
N_HEADS = 16
N_KV_HEADS = 4
GROUP = N_HEADS // N_KV_HEADS
HEAD_DIM = 64
Q_DIM = N_HEADS * HEAD_DIM
KV_DIM = N_KV_HEADS * HEAD_DIM
WINDOW = 128
ATTN_SCALE = HEAD_DIM ** -0.5
ROPE_THETA = 10000.0
ROPE_HALF = HEAD_DIM // 2
ROPE_FREQS = ROPE_HALF // 2
C_CONV = D_MODEL // 2
CONV_WIDTH = 31
CONV_PAD = (CONV_WIDTH - 1) // 2
N_EXPERTS = 32
TOP_K = 4
D_FF = D_MODEL
SWIGLU_LIMIT = 7.0
SWIGLU_ALPHA = 1.702
EPS = 1e-6
IN_COLS = Q_DIM + 2 * KV_DIM + 2 * C_CONV + 2 * D_MODEL

N_CTX = BATCH * SEQ
N_LAT = DEC_BATCH * DEC_SEQ
N_TOK = N_CTX + N_LAT
N_COND = 1 + DEC_BATCH
COND_ROWS = 8

LANE = 128
SUBLANE = 8
TM = 512
TQ = 256
TQ_SUB = 256
KWIN = TQ_SUB + 2 * WINDOW
assert GROUP == 4 and 2 * HEAD_DIM == LANE
LOG2E = 1.4426950408889634
QK_SCALE = ATTN_SCALE * LOG2E
HALO = 16
CONV_SPAN = SEQ + ((HALO - CONV_PAD + CONV_WIDTH - 1) // SUBLANE) * SUBLANE
assert CONV_SPAN + SUBLANE - 1 <= SEQ + 2 * HALO

TB = 256
N_BLOCKS = N_TOK // TB
UNIT = SUBLANE
SLOTS = 1280
BLOCK_UNITS = SLOTS // UNIT
TE = 256
TILE_UNITS = TE // UNIT
N_ETILES = (N_BLOCKS * BLOCK_UNITS) // TILE_UNITS + N_EXPERTS
W_CHUNKS = 4
W_ROWS = D_MODEL // W_CHUNKS
assert D_FF == D_MODEL
GATHER_PRIORITY = 1
VMEM_LIMIT = 56 * 1024 * 1024

assert SLOTS >= TB * TOP_K + N_EXPERTS * (UNIT - 1) and SLOTS % UNIT == 0

_F32 = jnp.float32
_BF16 = jnp.bfloat16
_I32 = jnp.int32


def _rms(x, g):
    return x * lax.rsqrt(jnp.mean(x * x, axis=-1, keepdims=True) + EPS) * g


def _cond_index(i, tile):
    n_ctx_tiles = N_CTX // tile
    return jnp.where(i < n_ctx_tiles, 0, 1 + (i - n_ctx_tiles) // (DEC_SEQ // tile))


def _mod_kernel(cond_ref, w_ref, b_ref, out_ref):
    cnd = cond_ref[...]
    s = cnd * jax.nn.sigmoid(cnd)
    out_ref[...] = jnp.dot(s, w_ref[...], precision=lax.Precision.HIGHEST,
                           preferred_element_type=_F32) + b_ref[...]


def _modulation(cond, w_ada, b_ada):
    tn = 1536
    nt = 6 * D_MODEL // tn
    return pl.pallas_call(
        _mod_kernel,
        grid=(DEPTH, nt),
        in_specs=[
            pl.BlockSpec((COND_ROWS, D_MODEL), lambda l, n: (0, 0)),
            pl.BlockSpec((None, D_MODEL, tn), lambda l, n: (l, 0, n)),
            pl.BlockSpec((None, 1, tn), lambda l, n: (l, 0, n)),
        ],
        out_specs=pl.BlockSpec((None, COND_ROWS, tn), lambda l, n: (l, 0, n)),
        out_shape=jax.ShapeDtypeStruct((DEPTH, COND_ROWS, 6 * D_MODEL), _F32),
        compiler_params=pltpu.CompilerParams(
            dimension_semantics=("arbitrary", "arbitrary"), vmem_limit_bytes=VMEM_LIMIT),
        name="modulation",
    )(cond, w_ada, b_ada.reshape(DEPTH, 1, 6 * D_MODEL))


def _rope_chunk(x, cos, sin):
    lane = lax.broadcasted_iota(_I32, x.shape, 1)
    partner = jnp.where((lane & ROPE_FREQS) == 0,
                        pltpu.roll(x, LANE - ROPE_FREQS, 1), pltpu.roll(x, ROPE_FREQS, 1))
    return x * cos + partner * sin


def _inproj_kernel(stack_x, xa_ref, xb_ref, mod_ref, g_ref, wf_ref, cos_ref, sin_ref,
                   q_ref, kv_ref, u_ref, sg_ref, *rest):
    w_ref = rest[-1]
    i = pl.program_id(0)

    @pl.when(i == 0)
    def _():
        w_ref[...] = wf_ref[...].astype(_BF16)

    x = jnp.where(i < N_CTX // TM, xa_ref[...], xb_ref[...])
    if stack_x:
        rest[0][...] = x
    sh = mod_ref[:, 0:D_MODEL]
    sc = mod_ref[:, D_MODEL:2 * D_MODEL]
    h = (_rms(x, g_ref[...]) * (1.0 + sc) + sh).astype(_BF16)

    c0 = 0
    q = jnp.dot(h, w_ref[:, c0:c0 + Q_DIM], preferred_element_type=_F32) * QK_SCALE
    c0 += Q_DIM
    kv = jnp.dot(h, w_ref[:, c0:c0 + 2 * KV_DIM], preferred_element_type=_F32)
    c0 += 2 * KV_DIM
    ua = jnp.dot(h, w_ref[:, c0:c0 + C_CONV], preferred_element_type=_F32)
    c0 += C_CONV
    ub = jnp.dot(h, w_ref[:, c0:c0 + C_CONV], preferred_element_type=_F32)
    c0 += C_CONV
    g = jnp.dot(h, w_ref[:, c0:c0 + 2 * D_MODEL], preferred_element_type=_F32)

    u_ref[...] = ua * jax.nn.sigmoid(ub)
    sg_ref[...] = jax.nn.sigmoid(g).astype(_BF16)

    is_latent = i >= N_CTX // TM

    @pl.when(jnp.logical_not(is_latent))
    def _():
        q_ref[...] = q.astype(_BF16)
        kv_ref[...] = kv

    @pl.when(is_latent)
    def _():
        cos = cos_ref[...]
        sin = sin_ref[...]
        for j in range(Q_DIM // LANE):
            sl = slice(j * LANE, (j + 1) * LANE)
            q_ref[:, sl] = _rope_chunk(q[:, sl], cos, sin).astype(_BF16)
        for j in range(KV_DIM // LANE):
            sl = slice(j * LANE, (j + 1) * LANE)
            kv_ref[:, sl] = _rope_chunk(kv[:, sl], cos, sin)
        kv_ref[:, KV_DIM:] = kv[:, KV_DIM:]


def _inproj(x_ctx, x_lat, lat_off, mod3, g_pre, w_in, cos_t, sin_t, layer):
    n_ctx_tiles = N_CTX // TM
    lat_tiles = DEC_SEQ // TM
    stack_x = lat_off == 0

    def rope_map(i):
        return (jnp.where(i < n_ctx_tiles, 0, (i - n_ctx_tiles) % lat_tiles), 0)

    row_tile = lambda width: pl.BlockSpec((TM, width), lambda i: (i, 0))
    out_specs = [row_tile(Q_DIM), row_tile(2 * KV_DIM), row_tile(C_CONV), row_tile(2 * D_MODEL)]
    out_shape = [
        jax.ShapeDtypeStruct((N_TOK, Q_DIM), _BF16),
        jax.ShapeDtypeStruct((N_TOK, 2 * KV_DIM), _F32),
        jax.ShapeDtypeStruct((N_TOK, C_CONV), _F32),
        jax.ShapeDtypeStruct((N_TOK, 2 * D_MODEL), _BF16),
    ]
    if stack_x:
        out_specs.append(row_tile(D_MODEL))
        out_shape.append(jax.ShapeDtypeStruct((N_TOK, D_MODEL), _F32))
    return pl.pallas_call(
        functools.partial(_inproj_kernel, stack_x),
        grid=(N_TOK // TM,),
        in_specs=[
            pl.BlockSpec((TM, D_MODEL), lambda i: (jnp.minimum(i, n_ctx_tiles - 1), 0)),
            pl.BlockSpec((TM, D_MODEL), lambda i: (lat_off + jnp.maximum(i - n_ctx_tiles, 0), 0)),
            pl.BlockSpec((None, 1, 6 * D_MODEL),
                         lambda i: (layer * COND_ROWS + _cond_index(i, TM), 0, 0)),
            pl.BlockSpec((1, D_MODEL), lambda i: (0, 0)),
            pl.BlockSpec((None, D_MODEL, IN_COLS), lambda i: (layer, 0, 0),
                         pipeline_mode=pl.Buffered(1)),
            pl.BlockSpec((TM, LANE), rope_map),
            pl.BlockSpec((TM, LANE), rope_map),
        ],
        out_specs=out_specs,
        out_shape=out_shape,
        scratch_shapes=[pltpu.VMEM((D_MODEL, IN_COLS), _BF16)],
        compiler_params=pltpu.CompilerParams(
            dimension_semantics=("arbitrary",), vmem_limit_bytes=VMEM_LIMIT),
        name="inproj",
    )(x_ctx, x_lat, mod3, g_pre, w_in, cos_t, sin_t)


def _pair_operands(k, v):
    zero = jnp.zeros_like(k)
    one = jnp.ones_like(v)
    ka = jnp.concatenate([k, zero], axis=1).astype(_BF16)
    kb = jnp.concatenate([zero, k], axis=1).astype(_BF16)
    va = jnp.concatenate([v, zero, one, zero], axis=1).astype(_BF16)
    vb = jnp.concatenate([zero, v, zero, one], axis=1).astype(_BF16)
    return ka, kb, va, vb


def _pair_attend(qq, operands, masks, sink_a, sink_b):
    def scores(which):
        out = []
        for ops, mask in zip(operands, masks):
            s = lax.dot_general(qq, ops[which], (((1,), (1,)), ((), ())),
                                preferred_element_type=_F32)
            out.append(s if mask is None else jnp.where(mask, s, -jnp.inf))
        return out

    acc = jnp.zeros((qq.shape[0], 2 * LANE), _F32)
    sink_terms = []
    for which, sink in ((0, sink_a), (1, sink_b)):
        ss = scores(which)
        m = sink
        for s in ss:
            m = jnp.maximum(m, jnp.max(s, axis=-1, keepdims=True))
        for s, ops in zip(ss, operands):
            acc = acc + jnp.dot(jnp.exp2(s - m).astype(_BF16), ops[2 + which],
                                preferred_element_type=_F32)
        sink_terms.append(jnp.exp2(sink - m))
    lane = lax.broadcasted_iota(_I32, (qq.shape[0], LANE), 1)
    sink_term = jnp.where(lane < HEAD_DIM, sink_terms[0], sink_terms[1])
    return acc[:, :LANE] / (acc[:, LANE:] + sink_term)


def _group_attend(q_ref, o_ref, hk, rows, operands, masks, sink_ref):
    n_rows = rows.stop - rows.start
    pairs = [slice((2 * hk + j) * LANE, (2 * hk + j + 1) * LANE) for j in range(GROUP // 2)]
    qq = jnp.concatenate([q_ref[rows, sl] for sl in pairs], axis=0)
    first = lax.broadcasted_iota(_I32, (qq.shape[0], 1), 0) < n_rows
    sink_a = jnp.where(first, sink_ref[GROUP * hk], sink_ref[GROUP * hk + 2]) * LOG2E
    sink_b = jnp.where(first, sink_ref[GROUP * hk + 1], sink_ref[GROUP * hk + 3]) * LOG2E
    out = _pair_attend(qq, operands, masks, sink_a, sink_b)
    for j, sl in enumerate(pairs):
        o_ref[rows, sl] = out[j * n_rows:(j + 1) * n_rows].astype(_BF16)


def _attn_kernel(sink_ref, q_ref, kv_own_ref, kv_seq_ref, ck_ref, cv_ref, o_ref):
    i = pl.program_id(0)
    n_ctx_steps = N_CTX // TQ

    @pl.when(i < n_ctx_steps)
    def _():
        for hk in range(N_KV_HEADS):
            ks = slice(hk * HEAD_DIM, (hk + 1) * HEAD_DIM)
            vs = slice(KV_DIM + hk * HEAD_DIM, KV_DIM + (hk + 1) * HEAD_DIM)
            own = _pair_operands(kv_own_ref[:, ks], kv_own_ref[:, vs])
            _group_attend(q_ref, o_ref, hk, slice(0, TQ), [own], [None], sink_ref)

    @pl.when(i >= n_ctx_steps)
    def _():
        qb = (i - n_ctx_steps) % (DEC_SEQ // TQ)
        stacked = (GROUP // 2) * TQ_SUB
        starts, masks = [], []
        for sb in range(TQ // TQ_SUB):
            q_start = qb * TQ + sb * TQ_SUB
            k_start = pl.multiple_of(jnp.clip(q_start - WINDOW, 0, DEC_SEQ - KWIN), WINDOW)
            qpos = q_start + lax.broadcasted_iota(_I32, (stacked, KWIN), 0) % TQ_SUB
            kpos = k_start + lax.broadcasted_iota(_I32, (stacked, KWIN), 1)
            starts.append(k_start)
            masks.append(jnp.abs(kpos - qpos) <= WINDOW)
        for hk in range(N_KV_HEADS):
            ks = slice(hk * HEAD_DIM, (hk + 1) * HEAD_DIM)
            vs = slice(KV_DIM + hk * HEAD_DIM, KV_DIM + (hk + 1) * HEAD_DIM)
            cached = _pair_operands(ck_ref[:, ks], cv_ref[:, ks])
            for sb in range(TQ // TQ_SUB):
                local = _pair_operands(kv_seq_ref[pl.ds(starts[sb], KWIN), ks],
                                       kv_seq_ref[pl.ds(starts[sb], KWIN), vs])
                _group_attend(q_ref, o_ref, hk, slice(sb * TQ_SUB, (sb + 1) * TQ_SUB),
                              [local, cached], [masks[sb], None], sink_ref)


def _attention(q, kv, cache_k, cache_v, sink, layer):
    n_ctx_steps = N_CTX // TQ
    nq = DEC_SEQ // TQ
    kv_off = N_CTX // DEC_SEQ

    def lat_batch(i):
        return jnp.maximum(i - n_ctx_steps, 0) // nq

    return pl.pallas_call(
        _attn_kernel,
        grid=(N_TOK // TQ,),
        in_specs=[
            pl.BlockSpec(memory_space=pltpu.SMEM),
            pl.BlockSpec((TQ, Q_DIM), lambda i: (i, 0)),
            pl.BlockSpec((TQ, 2 * KV_DIM), lambda i: (i, 0)),
            pl.BlockSpec((DEC_SEQ, 2 * KV_DIM), lambda i: (kv_off + lat_batch(i), 0)),
            pl.BlockSpec((None, None, PAST_LEN, KV_DIM), lambda i: (lat_batch(i), layer, 0, 0)),
            pl.BlockSpec((None, None, PAST_LEN, KV_DIM), lambda i: (lat_batch(i), layer, 0, 0)),
        ],
        out_specs=pl.BlockSpec((TQ, Q_DIM), lambda i: (i, 0)),
        out_shape=jax.ShapeDtypeStruct((N_TOK, Q_DIM), _BF16),
        compiler_params=pltpu.CompilerParams(
            dimension_semantics=("arbitrary",), vmem_limit_bytes=VMEM_LIMIT),
        name="attention",
    )(sink, q, kv, kv, cache_k, cache_v)


def _conv_kernel(prev_ref, cur_ref, next_ref, w_ref, b_ref, lg_ref, lb_ref, y_ref, pad_ref, sh_ref):
    i = pl.program_id(0)
    n_ctx_tiles = N_CTX // SEQ
    tiles_per_seq = jnp.where(i < n_ctx_tiles, 1, DEC_SEQ // SEQ)
    j = jnp.where(i < n_ctx_tiles, 0, (i - n_ctx_tiles) % (DEC_SEQ // SEQ))
    pad_ref[0:HALO, :] = jnp.where(j > 0, prev_ref[...], 0.0)
    pad_ref[HALO:HALO + SEQ, :] = cur_ref[...]
    pad_ref[HALO + SEQ:HALO + SEQ + HALO, :] = jnp.where(j < tiles_per_seq - 1, next_ref[...], 0.0)

    for r in range(SUBLANE):
        sh_ref[r] = pad_ref[r:r + CONV_SPAN, :]

    rows = 64
    for r0 in range(0, SEQ, rows):
        acc = jnp.zeros((rows, C_CONV), _F32) + b_ref[...]
        for t in range(CONV_WIDTH):
            off = HALO - CONV_PAD + t
            start = (off // SUBLANE) * SUBLANE + r0
            acc = acc + sh_ref[off % SUBLANE, start:start + rows, :] * w_ref[t:t + 1, :]
        mu = jnp.mean(acc, axis=-1, keepdims=True)
        d = acc - mu
        var = jnp.mean(d * d, axis=-1, keepdims=True)
        y = d * lax.rsqrt(var + EPS) * lg_ref[...] + lb_ref[...]
        y_ref[r0:r0 + rows, :] = (y * jax.nn.sigmoid(y)).astype(_BF16)


def _conv_branch(u, conv_w, conv_b, ln_g, ln_b):
    n_tiles = N_TOK // SEQ
    hb = SEQ // HALO
    last = N_TOK // HALO - 1
    return pl.pallas_call(
        _conv_kernel,
        grid=(n_tiles,),
        in_specs=[
            pl.BlockSpec((HALO, C_CONV), lambda i: (jnp.maximum(i * hb - 1, 0), 0)),
            pl.BlockSpec((SEQ, C_CONV), lambda i: (i, 0)),
            pl.BlockSpec((HALO, C_CONV), lambda i: (jnp.minimum((i + 1) * hb, last), 0)),
            pl.BlockSpec((CONV_WIDTH, C_CONV), lambda i: (0, 0)),
            pl.BlockSpec((1, C_CONV), lambda i: (0, 0)),
            pl.BlockSpec((1, C_CONV), lambda i: (0, 0)),
            pl.BlockSpec((1, C_CONV), lambda i: (0, 0)),
        ],
        out_specs=pl.BlockSpec((SEQ, C_CONV), lambda i: (i, 0)),
        out_shape=jax.ShapeDtypeStruct((N_TOK, C_CONV), _BF16),
        scratch_shapes=[pltpu.VMEM((SEQ + 2 * HALO, C_CONV), _F32),
                        pltpu.VMEM((SUBLANE, CONV_SPAN, C_CONV), _F32)],
        compiler_params=pltpu.CompilerParams(
            dimension_semantics=("arbitrary",), vmem_limit_bytes=VMEM_LIMIT),
        name="conv_branch",
    )(u, u, u, conv_w, conv_b, ln_g, ln_b)


def _mix_kernel(x_ref, mod_ref, att_ref, cv_ref, sg_ref, waf_ref, wcf_ref, wof_ref,
                gpost_ref, gffn_ref, wr_ref, wrlo_ref, br_ref,
                x1_ref, xp_ref, qw_ref, meta_ref, wa_ref, wc_ref, wo_ref):
    @pl.when(pl.program_id(0) == 0)
    def _():
        wa_ref[...] = waf_ref[...].astype(_BF16)
        wc_ref[...] = wcf_ref[...].astype(_BF16)
        wo_ref[...] = wof_ref[...].astype(_BF16)

    a = jnp.dot(att_ref[...], wa_ref[...], preferred_element_type=_F32)
    cv = jnp.dot(cv_ref[...], wc_ref[...], preferred_element_type=_F32)
    m = sg_ref[:, 0:D_MODEL].astype(_F32) * a + sg_ref[:, D_MODEL:].astype(_F32) * cv
    mix = jnp.dot(m.astype(_BF16), wo_ref[...], preferred_element_type=_F32)
    gt1 = mod_ref[:, 2 * D_MODEL:3 * D_MODEL]
    sh2 = mod_ref[:, 3 * D_MODEL:4 * D_MODEL]
    sc2 = mod_ref[:, 4 * D_MODEL:5 * D_MODEL]
    x1 = x_ref[...] + gt1 * _rms(mix, gpost_ref[...])
    x1_ref[...] = x1
    h2 = _rms(x1, gffn_ref[...]) * (1.0 + sc2) + sh2
    h2b = h2.astype(_BF16)

    h2_lo = (h2 - h2b.astype(_F32)).astype(_BF16)
    logits = (jnp.dot(h2b, wr_ref[...], preferred_element_type=_F32)
              + jnp.dot(h2_lo, wr_ref[...], preferred_element_type=_F32)
              + jnp.dot(h2b, wrlo_ref[...], preferred_element_type=_F32) + br_ref[...])
    lane = lax.broadcasted_iota(_I32, (TB, LANE), 1).astype(_F32)
    member = jnp.zeros((TB, LANE), _F32)
    hots, exps = [], []
    top = None
    total = jnp.zeros((TB, 1), _F32)
    for k in range(TOP_K):
        mval = jnp.max(logits, axis=-1, keepdims=True)
        sel = jnp.min(jnp.where(logits == mval, lane, float(LANE)), axis=-1, keepdims=True)
        if top is None:
            top = mval
        e = jnp.exp(mval - top)
        total = total + e
        hot = lane == sel
        hots.append(hot)
        exps.append(e)
        member = member + jnp.where(hot, 1.0, 0.0)
        logits = jnp.where(hot, -jnp.inf, logits)

    r_i = lax.broadcasted_iota(_I32, (TB, TB), 0)
    c_i = lax.broadcasted_iota(_I32, (TB, TB), 1)
    lower = jnp.where(r_i > c_i, 1.0, 0.0).astype(_BF16)
    rank = jnp.dot(lower, member.astype(_BF16), preferred_element_type=_F32)
    count = jnp.sum(member, axis=0, keepdims=True)
    units = jnp.floor((count + float(UNIT - 1)) * (1.0 / UNIT))
    r_l = lax.broadcasted_iota(_I32, (LANE, LANE), 0)
    c_l = lax.broadcasted_iota(_I32, (LANE, LANE), 1)
    upper = jnp.where(r_l < c_l, 1.0, 0.0).astype(_BF16)
    unit_off = jnp.dot(jnp.broadcast_to(units, (SUBLANE, LANE)).astype(_BF16), upper,
                       preferred_element_type=_F32)[0:1, :]
    base = unit_off * float(UNIT) + rank

    slot_lane = lax.broadcasted_iota(_I32, (TB, SLOTS), 1).astype(_F32)
    qw = jnp.zeros((TB, SLOTS), _F32)
    hit = jnp.zeros((TB, SLOTS), _F32)
    for k in range(TOP_K):
        slot = jnp.sum(jnp.where(hots[k], base, 0.0), axis=-1, keepdims=True)
        here = slot_lane == slot
        qw = jnp.where(here, exps[k] / total, qw)
        hit = jnp.where(here, 1.0, hit)
    qw_ref[...] = qw.astype(_BF16)

    xp_ref[...] = lax.dot_general(hit.astype(_BF16), h2b, (((0,), (0,)), ((), ())),
                                  preferred_element_type=_F32)

    sub = lax.broadcasted_iota(_I32, (SUBLANE, LANE), 0)
    meta = jnp.where(sub == 0, units, jnp.where(sub == 1, unit_off, 0.0))
    meta_ref[...] = meta.astype(_I32)


def _mix(x, mod3, att, cvn, sg, wa, wc, wo, g_post, g_ffn, wr_hi, wr_lo, b_router_pad, layer):
    full = lambda shape: pl.BlockSpec(shape, lambda i: (0,) * len(shape))
    layer_weight = lambda rows: pl.BlockSpec((None, rows, D_MODEL), lambda i: (layer, 0, 0),
                                             pipeline_mode=pl.Buffered(1))
    return pl.pallas_call(
        _mix_kernel,
        grid=(N_BLOCKS,),
        in_specs=[
            pl.BlockSpec((TB, D_MODEL), lambda i: (i, 0)),
            pl.BlockSpec((None, 1, 6 * D_MODEL),
                         lambda i: (layer * COND_ROWS + _cond_index(i, TB), 0, 0)),
            pl.BlockSpec((TB, Q_DIM), lambda i: (i, 0)),
            pl.BlockSpec((TB, C_CONV), lambda i: (i, 0)),
            pl.BlockSpec((TB, 2 * D_MODEL), lambda i: (i, 0)),
            layer_weight(Q_DIM),
            layer_weight(C_CONV),
            layer_weight(D_MODEL),
            full((1, D_MODEL)),
            full((1, D_MODEL)),
            full((D_MODEL, LANE)),
            full((D_MODEL, LANE)),
            full((1, LANE)),
        ],
        out_specs=[
            pl.BlockSpec((TB, D_MODEL), lambda i: (i, 0)),
            pl.BlockSpec((SLOTS, D_MODEL), lambda i: (i, 0)),
            pl.BlockSpec((TB, SLOTS), lambda i: (i, 0)),
            pl.BlockSpec((None, SUBLANE, LANE), lambda i: (i, 0, 0)),
        ],
        out_shape=[
            jax.ShapeDtypeStruct((N_TOK, D_MODEL), _F32),
            jax.ShapeDtypeStruct((N_BLOCKS * SLOTS, D_MODEL), _F32),
            jax.ShapeDtypeStruct((N_TOK, SLOTS), _BF16),
            jax.ShapeDtypeStruct((N_BLOCKS, SUBLANE, LANE), _I32),
        ],
        scratch_shapes=[pltpu.VMEM((Q_DIM, D_MODEL), _BF16), pltpu.VMEM((C_CONV, D_MODEL), _BF16),
                        pltpu.VMEM((D_MODEL, D_MODEL), _BF16)],
        compiler_params=pltpu.CompilerParams(
            dimension_semantics=("arbitrary",), vmem_limit_bytes=VMEM_LIMIT),
        name="mix_router",
    )(x, mod3, att, cvn, sg, wa, wc, wo, g_post, g_ffn, wr_hi, wr_lo, b_router_pad)


def _plan(meta):
    units = meta[:, 0, :N_EXPERTS]
    seg_off = meta[:, 1, :N_EXPERTS]
    tiles = (jnp.sum(units, axis=0) + TILE_UNITS - 1) // TILE_UNITS
    tile_end = jnp.cumsum(tiles)
    n_tiles = tile_end[-1]
    region = (tile_end - tiles) * TILE_UNITS
    dst = region[None, :] + jnp.cumsum(units, axis=0) - units
    src = jnp.arange(N_BLOCKS, dtype=_I32)[:, None] * BLOCK_UNITS + seg_off

    tile_first = jnp.concatenate([jnp.zeros((1,), _I32), tile_end.astype(_I32)])

    dst_f, len_f, src_f = dst.reshape(1, -1), units.reshape(1, -1), src.reshape(1, -1)
    d = jnp.arange(N_ETILES * TILE_UNITS, dtype=_I32)[:, None]
    in_seg = (dst_f <= d) & (d < dst_f + len_f)
    src_unit = jnp.sum(jnp.where(in_seg, src_f + d - dst_f, 0), axis=1)

    u = jnp.arange(BLOCK_UNITS, dtype=_I32)[None, :, None]
    so, un = seg_off[:, None, :], units[:, None, :]
    in_blk = (so <= u) & (u < so + un)
    back_unit = jnp.sum(jnp.where(in_blk, dst[:, None, :] + u - so, 0), axis=2)
    return (tile_first, n_tiles.reshape(1).astype(_I32), src_unit.astype(_I32),
            back_unit.reshape(-1).astype(_I32))


def _unit_gather(src_hbm, unit_ref, first, n_units, dst_buf, sem):
    for i in range(n_units):
        row = pl.multiple_of(unit_ref[first + i] * UNIT, UNIT)
        pltpu.make_async_copy(src_hbm.at[pl.ds(row, UNIT), :],
                              dst_buf.at[pl.ds(i * UNIT, UNIT), :], sem).start(priority=GATHER_PRIORITY)


def _unit_gather_wait(src_hbm, n_units, dst_buf, sem):
    pltpu.make_async_copy(src_hbm.at[pl.ds(0, n_units * UNIT), :], dst_buf, sem).wait()


def _tile_write(ybuf_slot, ys_hbm, tile, sem):
    row = pl.multiple_of(tile * TE, TE)
    return pltpu.make_async_copy(ybuf_slot, ys_hbm.at[pl.ds(row, TE), :], sem)


def _weight_chunk(wgu_hbm, wd_hbm, layer, expert, chunk, wgu_f32, wd_f32, buf, sem):
    r = pl.multiple_of(chunk * W_ROWS, W_ROWS)
    return (pltpu.make_async_copy(wgu_hbm.at[layer, expert, pl.ds(r, W_ROWS), :],
                                  wgu_f32.at[buf, pl.ds(r, W_ROWS), :], sem),
            pltpu.make_async_copy(wd_hbm.at[layer, expert, pl.ds(r, W_ROWS), :],
                                  wd_f32.at[buf, pl.ds(r, W_ROWS), :], sem))


def _expert_kernel(layer, first_ref, nt_ref, src_ref, xp_hbm, wgu_hbm, bgu_ref, wd_hbm, bd_ref,
                   ys_hbm, wgu_f32, wd_f32, wgu_bf, wd_bf, xbuf, ybuf, wsem, xsem, ysem):
    e = pl.program_id(0)
    n_live = nt_ref[0]
    t_lo = first_ref[e]
    t_hi = first_ref[e + 1]
    buf = e % 2
    has_next = e + 1 < N_EXPERTS

    def start_chunk(expert, chunk, into):
        for cp in _weight_chunk(wgu_hbm, wd_hbm, layer, expert, chunk, wgu_f32, wd_f32,
                                into, wsem.at[into]):
            cp.start()

    @pl.when(e == 0)
    def _():
        _unit_gather(xp_hbm, src_ref, 0, TILE_UNITS, xbuf.at[0], xsem.at[0])
        for c in range(W_CHUNKS):
            start_chunk(0, c, 0)

    pltpu.make_async_copy(wgu_hbm.at[layer, e], wgu_f32.at[buf], wsem.at[buf]).wait()
    pltpu.make_async_copy(wd_hbm.at[layer, e], wd_f32.at[buf], wsem.at[buf]).wait()

    @pl.when(t_hi > t_lo)
    def _():
        wgu_bf[...] = wgu_f32[buf].astype(_BF16)
        wd_bf[...] = wd_f32[buf].astype(_BF16)

    def tile_body(t, carry):
        slot = t % 2

        @pl.when(jnp.logical_and(has_next, t - t_lo < W_CHUNKS))
        def _():
            start_chunk(e + 1, t - t_lo, 1 - buf)

        @pl.when(t + 1 < n_live)
        def _():
            _unit_gather(xp_hbm, src_ref, (t + 1) * TILE_UNITS, TILE_UNITS,
                         xbuf.at[1 - slot], xsem.at[1 - slot])

        _unit_gather_wait(xp_hbm, TILE_UNITS, xbuf.at[slot], xsem.at[slot])
        gu = jnp.dot(xbuf[slot].astype(_BF16), wgu_bf[...],
                     preferred_element_type=_F32) + bgu_ref[...]
        gate = jnp.minimum(gu[:, :D_FF], SWIGLU_LIMIT)
        lin = jnp.clip(gu[:, D_FF:], -SWIGLU_LIMIT, SWIGLU_LIMIT)
        act = gate * jax.nn.sigmoid(SWIGLU_ALPHA * gate) * (lin + 1.0)
        y = jnp.dot(act.astype(_BF16), wd_bf[...], preferred_element_type=_F32) + bd_ref[...]

        @pl.when(t >= 2)
        def _():
            _tile_write(ybuf.at[slot], ys_hbm, t - 2, ysem.at[slot]).wait()

        ybuf[slot] = y
        _tile_write(ybuf.at[slot], ys_hbm, t, ysem.at[slot]).start(priority=GATHER_PRIORITY)
        return carry

    lax.fori_loop(t_lo, t_hi, tile_body, 0)

    for c in range(W_CHUNKS):
        @pl.when(jnp.logical_and(has_next, c >= t_hi - t_lo))
        def _():
            start_chunk(e + 1, c, 1 - buf)

    @pl.when(e == N_EXPERTS - 1)
    def _():
        for back in (2, 1):
            @pl.when(n_live >= back)
            def _():
                t = n_live - back
                _tile_write(ybuf.at[t % 2], ys_hbm, t, ysem.at[t % 2]).wait()

        ybuf[0] = jnp.zeros((TE, D_MODEL), _F32)

        def zero_start(t, carry):
            _tile_write(ybuf.at[0], ys_hbm, t, ysem.at[0]).start()
            return carry

        def zero_wait(t, carry):
            _tile_write(ybuf.at[0], ys_hbm, t, ysem.at[0]).wait()
            return carry

        lax.fori_loop(n_live, N_ETILES, zero_start, 0)
        lax.fori_loop(n_live, N_ETILES, zero_wait, 0)


def _experts(tile_first, n_tiles, src_unit, xp, w_gate_up, b_gate_up, w_down, b_down, layer):
    grid_spec = pltpu.PrefetchScalarGridSpec(
        num_scalar_prefetch=3,
        grid=(N_EXPERTS,),
        in_specs=[
            pl.BlockSpec(memory_space=pl.ANY),
            pl.BlockSpec(memory_space=pl.ANY),
            pl.BlockSpec((None, None, 1, 2 * D_FF), lambda e, tf, nt, su: (layer, e, 0, 0)),
            pl.BlockSpec(memory_space=pl.ANY),
            pl.BlockSpec((None, None, 1, D_MODEL), lambda e, tf, nt, su: (layer, e, 0, 0)),
        ],
        out_specs=pl.BlockSpec(memory_space=pl.ANY),
        scratch_shapes=[
            pltpu.VMEM((2, D_MODEL, 2 * D_FF), _F32),
            pltpu.VMEM((2, D_FF, D_MODEL), _F32),
            pltpu.VMEM((D_MODEL, 2 * D_FF), _BF16),
            pltpu.VMEM((D_FF, D_MODEL), _BF16),
            pltpu.VMEM((2, TE, D_MODEL), _F32),
            pltpu.VMEM((2, TE, D_MODEL), _F32),
            pltpu.SemaphoreType.DMA((2,)),
            pltpu.SemaphoreType.DMA((2,)),
            pltpu.SemaphoreType.DMA((2,)),
        ],
    )
    return pl.pallas_call(
        functools.partial(_expert_kernel, layer),
        grid_spec=grid_spec,
        out_shape=jax.ShapeDtypeStruct((N_ETILES * TE, D_MODEL), _F32),
        compiler_params=pltpu.CompilerParams(
            dimension_semantics=("arbitrary",), vmem_limit_bytes=VMEM_LIMIT),
        name="experts",
    )(tile_first, n_tiles, src_unit, xp, w_gate_up,
      b_gate_up.reshape(DEPTH, N_EXPERTS, 1, 2 * D_FF), w_down,
      b_down.reshape(DEPTH, N_EXPERTS, 1, D_MODEL))


def _combine_kernel(split_out, back_ref, ys_hbm, x_ref, mod_ref, qw_ref, g_ref, *rest):
    outs, ybuf, sem = rest[:-2], rest[-2], rest[-1]
    b = pl.program_id(0)
    slot = b % 2

    @pl.when(b == 0)
    def _():
        _unit_gather(ys_hbm, back_ref, 0, BLOCK_UNITS, ybuf.at[0], sem.at[0])

    @pl.when(b + 1 < N_BLOCKS)
    def _():
        _unit_gather(ys_hbm, back_ref, (b + 1) * BLOCK_UNITS, BLOCK_UNITS,
                     ybuf.at[1 - slot], sem.at[1 - slot])

    _unit_gather_wait(ys_hbm, BLOCK_UNITS, ybuf.at[slot], sem.at[slot])
    moe = jnp.dot(qw_ref[...], ybuf[slot].astype(_BF16), preferred_element_type=_F32)
    gt2 = mod_ref[:, 5 * D_MODEL:6 * D_MODEL]
    res = x_ref[...] + gt2 * _rms(moe, g_ref[...])
    if split_out:
        @pl.when(b < N_CTX // TB)
        def _():
            outs[0][...] = res

        @pl.when(b >= N_CTX // TB)
        def _():
            outs[1][...] = res
    else:
        outs[0][...] = res


def _combine(back_unit, ys, x1, mod3, qw, g_post, layer, split_out):
    n_ctx_blocks = N_CTX // TB
    if split_out:
        out_specs = [
            pl.BlockSpec((TB, D_MODEL), lambda b, bu: (jnp.minimum(b, n_ctx_blocks - 1), 0)),
            pl.BlockSpec((TB, D_MODEL), lambda b, bu: (jnp.maximum(b - n_ctx_blocks, 0), 0)),
        ]
        out_shape = [jax.ShapeDtypeStruct((N_CTX, D_MODEL), _F32),
                     jax.ShapeDtypeStruct((N_LAT, D_MODEL), _F32)]
    else:
        out_specs = [pl.BlockSpec((TB, D_MODEL), lambda b, bu: (b, 0))]
        out_shape = [jax.ShapeDtypeStruct((N_TOK, D_MODEL), _F32)]
    grid_spec = pltpu.PrefetchScalarGridSpec(
        num_scalar_prefetch=1,
        grid=(N_BLOCKS,),
        in_specs=[
            pl.BlockSpec(memory_space=pl.ANY),
            pl.BlockSpec((TB, D_MODEL), lambda b, bu: (b, 0)),
            pl.BlockSpec((None, 1, 6 * D_MODEL),
                         lambda b, bu: (layer * COND_ROWS + _cond_index(b, TB), 0, 0)),
            pl.BlockSpec((TB, SLOTS), lambda b, bu: (b, 0)),
            pl.BlockSpec((1, D_MODEL), lambda b, bu: (0, 0)),
        ],
        out_specs=out_specs,
        scratch_shapes=[
            pltpu.VMEM((2, SLOTS, D_MODEL), _F32),
            pltpu.SemaphoreType.DMA((2,)),
        ],
    )
    return pl.pallas_call(
        functools.partial(_combine_kernel, split_out),
        grid_spec=grid_spec,
        out_shape=out_shape,
        compiler_params=pltpu.CompilerParams(
            dimension_semantics=("arbitrary",), vmem_limit_bytes=VMEM_LIMIT),
        name="combine_residual",
    )(back_unit, ys, x1, mod3, qw, g_post)


def _rope_tables():
    pos = jnp.arange(DEC_SEQ)
    row = (pos // GRID_W).astype(_F32)
    col = (pos % GRID_W).astype(_F32)
    inv = ROPE_THETA ** (-jnp.arange(ROPE_FREQS, dtype=_F32) / ROPE_FREQS)
    ang_r = row[:, None] * inv[None, :]
    ang_c = col[:, None] * inv[None, :]
    cos = jnp.concatenate([jnp.cos(ang_r)] * 2 + [jnp.cos(ang_c)] * 2, axis=-1)
    sin = jnp.concatenate([-jnp.sin(ang_r), jnp.sin(ang_r), -jnp.sin(ang_c), jnp.sin(ang_c)], axis=-1)
    reps = LANE // HEAD_DIM
    return jnp.tile(cos, (1, reps)), jnp.tile(sin, (1, reps))


def kernel(x_prompt, x_sample, cache_k, cache_v, c, c_ctx, w_ada, b_ada, g_pre_mix, g_post_mix,
           g_pre_ffn, g_post_ffn, w_in, attn_sink, w_attn_o, conv_w, conv_b, conv_ln_g, conv_ln_b,
           w_conv_o, w_out, w_router, b_router, w_gate_up, b_gate_up, w_down, b_down):
    cond =jnp.concatenate([c_ctx[None, :], c, jnp.zeros((COND_ROWS - N_COND, D_MODEL), _F32)], axis=0)
    mod = _modulation(cond, w_ada, b_ada)
    mod3 = mod.reshape(DEPTH * COND_ROWS, 1, 6 * D_MODEL)
    cos_t, sin_t = _rope_tables()
    ck = cache_k.reshape(DEC_BATCH, DEPTH, PAST_LEN, KV_DIM)
    cv = cache_v.reshape(DEC_BATCH, DEPTH, PAST_LEN, KV_DIM)
    w_router_pad = jnp.pad(w_router, ((0, 0), (0, 0), (0, LANE - N_EXPERTS)))
    wr_hi = w_router_pad.astype(_BF16)
    wr_lo = (w_router_pad - wr_hi.astype(_F32)).astype(_BF16)
    b_router_pad = jnp.pad(b_router, ((0, 0), (0, LANE - N_EXPERTS)), constant_values=-jnp.inf)

    new_k, new_v = [], []
    x = None
    for l in range(DEPTH):
        row = lambda a: a[l][None, :]
        if l == 0:
            q, kv, u, sg, x = _inproj(x_prompt.reshape(N_CTX, D_MODEL), x_sample.reshape(N_LAT, D_MODEL),
                                      0, mod3, row(g_pre_mix), w_in, cos_t, sin_t, l)
        else:
            q, kv, u, sg = _inproj(x, x, N_CTX // TM, mod3, row(g_pre_mix), w_in, cos_t, sin_t, l)
        new_k.append(kv[:N_CTX, :KV_DIM].reshape(BATCH, SEQ, N_KV_HEADS, HEAD_DIM))
        new_v.append(kv[:N_CTX, KV_DIM:].reshape(BATCH, SEQ, N_KV_HEADS, HEAD_DIM))
        att = _attention(q, kv, ck, cv, attn_sink[l], l)
        cvn = _conv_branch(u, conv_w[l], row(conv_b), row(conv_ln_g), row(conv_ln_b))
        x1, xp, qw, meta = _mix(
            x, mod3, att, cvn, sg, w_attn_o, w_conv_o, w_out, row(g_post_mix), row(g_pre_ffn),
            wr_hi[l], wr_lo[l], row(b_router_pad), l)
        tile_first, n_tiles, src_unit, back_unit = _plan(meta)
        ys = _experts(tile_first, n_tiles, src_unit, xp, w_gate_up, b_gate_up, w_down, b_down, l)
        outs = _combine(back_unit, ys, x1, mod3, qw, row(g_post_ffn), l, l == DEPTH - 1)
        x = outs[0]

    y_prompt = outs[0].reshape(BATCH, SEQ, D_MODEL)
    y_sample = outs[1].reshape(DEC_BATCH, DEC_SEQ, D_MODEL)
    return (y_prompt, y_sample, jnp.stack(new_k, axis=1), jnp.stack(new_v, axis=1))
```

```python
import functools

import jax
import jax.numpy as jnp
from jax import lax
from jax.experimental import pallas as pl
from jax.experimental.pallas import tpu as pltpu

D_MODEL = 1024
BATCH = 16
SEQ = 256
DEPTH = 2
DEC_BATCH = 2
DEC_SEQ = 2048
PAST_LEN = 256
GRID_W = 64
N_HEADS = 16
N_KV_HEADS = 4
GROUP = N_HEADS // N_KV_HEADS
HEAD_DIM = 64
Q_DIM = N_HEADS * HEAD_DIM
KV_DIM = N_KV_HEADS * HEAD_DIM
WINDOW = 128
ATTN_SCALE = HEAD_DIM ** -0.5
ROPE_THETA = 10000.0
ROPE_HALF = HEAD_DIM // 2
ROPE_FREQS = ROPE_HALF // 2
C_CONV = D_MODEL // 2
CONV_WIDTH = 31
CONV_PAD = (CONV_WIDTH - 1) // 2
N_EXPERTS = 32
TOP_K = 4
D_FF = D_MODEL
SWIGLU_LIMIT = 7.0
SWIGLU_ALPHA = 1.702
EPS = 1e-6
IN_COLS = Q_DIM + 2 * KV_DIM + 2 * C_CONV + 2 * D_MODEL

N_CTX = BATCH * SEQ
N_LAT = DEC_BATCH * DEC_SEQ
N_TOK = N_CTX + N_LAT
N_COND = 1 + DEC_BATCH
COND_ROWS = 8

LANE = 128
SUBLANE = 8
TM = 512
TQ = 256
TQ_SUB = 256
KWIN = TQ_SUB + 2 * WINDOW
assert GROUP == 4 and 2 * HEAD_DIM == LANE
LOG2E = 1.4426950408889634
QK_SCALE = ATTN_SCALE * LOG2E
HALO = 16
CONV_SPAN = SEQ + ((HALO - CONV_PAD + CONV_WIDTH - 1) // SUBLANE) * SUBLANE
assert CONV_SPAN + SUBLANE - 1 <= SEQ + 2 * HALO

TB = 256
N_BLOCKS = N_TOK // TB
UNIT = 2 * SUBLANE
SLOTS = 1536
BLOCK_UNITS = SLOTS // UNIT
TE = 256
TILE_UNITS = TE // UNIT
N_ETILES = (N_BLOCKS * BLOCK_UNITS) // TILE_UNITS + N_EXPERTS
COMBINE_BUFS = 3
X_BUFS = 3
MAX_LIVE_TILES = (N_BLOCKS * ((TB * TOP_K + N_EXPERTS * (UNIT - 1)) // UNIT)) // TILE_UNITS + N_EXPERTS
assert MAX_LIVE_TILES + X_BUFS - 1 <= N_ETILES
W_CHUNKS = 4
W_ROWS = D_MODEL // W_CHUNKS
assert D_FF == D_MODEL
GATHER_PRIORITY = 1
VMEM_LIMIT = 56 * 1024 * 1024

assert SLOTS >= TB * TOP_K + N_EXPERTS * (UNIT - 1) and SLOTS % UNIT == 0

_F32 = jnp.float32
_BF16 = jnp.bfloat16
_I32 = jnp.int32


def _rms(x, g):
    return x * lax.rsqrt(jnp.mean(x * x, axis=-1, keepdims=True) + EPS) * g


def _cond_index(i, tile):
    n_ctx_tiles = N_CTX // tile
    return jnp.where(i < n_ctx_tiles, 0, 1 + (i - n_ctx_tiles) // (DEC_SEQ // tile))


def _mod_kernel(cond_ref, w_ref, b_ref, out_ref):
    cnd = cond_ref[...]
    s = cnd * jax.nn.sigmoid(cnd)
    out_ref[...] = jnp.dot(s, w_ref[...], precision=lax.Precision.HIGHEST,
                           preferred_element_type=_F32) + b_ref[...]


def _modulation(cond, w_ada, b_ada):
    tn = 1536
    nt = 6 * D_MODEL // tn
    return pl.pallas_call(
        _mod_kernel,
        grid=(DEPTH, nt),
        in_specs=[
            pl.BlockSpec((COND_ROWS, D_MODEL), lambda l, n: (0, 0)),
            pl.BlockSpec((None, D_MODEL, tn), lambda l, n: (l, 0, n)),
            pl.BlockSpec((None, 1, tn), lambda l, n: (l, 0, n)),
        ],
        out_specs=pl.BlockSpec((None, COND_ROWS, tn), lambda l, n: (l, 0, n)),
        out_shape=jax.ShapeDtypeStruct((DEPTH, COND_ROWS, 6 * D_MODEL), _F32),
        compiler_params=pltpu.CompilerParams(
            dimension_semantics=("arbitrary", "arbitrary"), vmem_limit_bytes=VMEM_LIMIT),
        name="modulation",
    )(cond, w_ada, b_ada.reshape(DEPTH, 1, 6 * D_MODEL))


def _rope_chunk(x, cos, sin):
    lane = lax.broadcasted_iota(_I32, x.shape, 1)
    partner = jnp.where((lane & ROPE_FREQS) == 0,
                        pltpu.roll(x, LANE - ROPE_FREQS, 1), pltpu.roll(x, ROPE_FREQS, 1))
    return x * cos + partner * sin


def _inproj_kernel(stack_x, xa_ref, xb_ref, mod_ref, g_ref, wf_ref, cos_ref, sin_ref,
                   q_ref, kv_ref, u_ref, sg_ref, *rest):
    w_ref = rest[-1]
    i = pl.program_id(0)

    @pl.when(i == 0)
    def _():
        w_ref[...] = wf_ref[...].astype(_BF16)

    x = jnp.where(i < N_CTX // TM, xa_ref[...], xb_ref[...])
    if stack_x:
        rest[0][...] = x
    sh = mod_ref[:, 0:D_MODEL]
    sc = mod_ref[:, D_MODEL:2 * D_MODEL]
    h = (_rms(x, g_ref[...]) * (1.0 + sc) + sh).astype(_BF16)

    c0 = 0
    q = jnp.dot(h, w_ref[:, c0:c0 + Q_DIM], preferred_element_type=_F32) * QK_SCALE
    c0 += Q_DIM
    kv = jnp.dot(h, w_ref[:, c0:c0 + 2 * KV_DIM], preferred_element_type=_F32)
    c0 += 2 * KV_DIM
    ua = jnp.dot(h, w_ref[:, c0:c0 + C_CONV], preferred_element_type=_F32)
    c0 += C_CONV
    ub = jnp.dot(h, w_ref[:, c0:c0 + C_CONV], preferred_element_type=_F32)
    c0 += C_CONV
    g = jnp.dot(h, w_ref[:, c0:c0 + 2 * D_MODEL], preferred_element_type=_F32)

    u_ref[...] = ua * jax.nn.sigmoid(ub)
    sg_ref[...] = jax.nn.sigmoid(g).astype(_BF16)

    is_latent = i >= N_CTX // TM

    @pl.when(jnp.logical_not(is_latent))
    def _():
        q_ref[...] = q.astype(_BF16)
        kv_ref[...] = kv

    @pl.when(is_latent)
    def _():
        cos = cos_ref[...]
        sin = sin_ref[...]
        for j in range(Q_DIM // LANE):
            sl = slice(j * LANE, (j + 1) * LANE)
            q_ref[:, sl] = _rope_chunk(q[:, sl], cos, sin).astype(_BF16)
        for j in range(KV_DIM // LANE):
            sl = slice(j * LANE, (j + 1) * LANE)
            kv_ref[:, sl] = _rope_chunk(kv[:, sl], cos, sin)
        kv_ref[:, KV_DIM:] = kv[:, KV_DIM:]


def _inproj(x_ctx, x_lat, lat_off, mod3, g_pre, w_in, cos_t, sin_t, layer):
    n_ctx_tiles = N_CTX // TM
    lat_tiles = DEC_SEQ // TM
    stack_x = lat_off == 0

    def rope_map(i):
        return (jnp.where(i < n_ctx_tiles, 0, (i - n_ctx_tiles) % lat_tiles), 0)

    row_tile = lambda width: pl.BlockSpec((TM, width), lambda i: (i, 0))
    out_specs = [row_tile(Q_DIM), row_tile(2 * KV_DIM), row_tile(C_CONV), row_tile(2 * D_MODEL)]
    out_shape = [
        jax.ShapeDtypeStruct((N_TOK, Q_DIM), _BF16),
        jax.ShapeDtypeStruct((N_TOK, 2 * KV_DIM), _F32),
        jax.ShapeDtypeStruct((N_TOK, C_CONV), _F32),
        jax.ShapeDtypeStruct((N_TOK, 2 * D_MODEL), _BF16),
    ]
    if stack_x:
        out_specs.append(row_tile(D_MODEL))
        out_shape.append(jax.ShapeDtypeStruct((N_TOK, D_MODEL), _F32))
    return pl.pallas_call(
        functools.partial(_inproj_kernel, stack_x),
        grid=(N_TOK // TM,),
        in_specs=[
            pl.BlockSpec((TM, D_MODEL), lambda i: (jnp.minimum(i, n_ctx_tiles - 1), 0)),
            pl.BlockSpec((TM, D_MODEL), lambda i: (lat_off + jnp.maximum(i - n_ctx_tiles, 0), 0)),
            pl.BlockSpec((None, 1, 6 * D_MODEL),
                         lambda i: (layer * COND_ROWS + _cond_index(i, TM), 0, 0)),
            pl.BlockSpec((1, D_MODEL), lambda i: (0, 0)),
            pl.BlockSpec((None, D_MODEL, IN_COLS), lambda i: (layer, 0, 0),
                         pipeline_mode=pl.Buffered(1)),
            pl.BlockSpec((TM, LANE), rope_map),
            pl.BlockSpec((TM, LANE), rope_map),
        ],
        out_specs=out_specs,
        out_shape=out_shape,
        scratch_shapes=[pltpu.VMEM((D_MODEL, IN_COLS), _BF16)],
        compiler_params=pltpu.CompilerParams(
            dimension_semantics=("arbitrary",), vmem_limit_bytes=VMEM_LIMIT),
        name="inproj",
    )(x_ctx, x_lat, mod3, g_pre, w_in, cos_t, sin_t)


def _pair_operands(k, v):
    zero = jnp.zeros_like(k)
    one = jnp.ones_like(v)
    ka = jnp.concatenate([k, zero], axis=1).astype(_BF16)
    kb = jnp.concatenate([zero, k], axis=1).astype(_BF16)
    va = jnp.concatenate([v, zero, one, zero], axis=1).astype(_BF16)
    vb = jnp.concatenate([zero, v, zero, one], axis=1).astype(_BF16)
    return ka, kb, va, vb


def _pair_attend(qq, operands, masks, sink_a, sink_b):
    def scores(which):
        out = []
        for ops, mask in zip(operands, masks):
            s = lax.dot_general(qq, ops[which], (((1,), (1,)), ((), ())),
                                preferred_element_type=_F32)
            out.append(s if mask is None else jnp.where(mask, s, -jnp.inf))
        return out

    acc = jnp.zeros((qq.shape[0], 2 * LANE), _F32)
    sink_terms = []
    for which, sink in ((0, sink_a), (1, sink_b)):
        ss = scores(which)
        m = sink
        for s in ss:
            m = jnp.maximum(m, jnp.max(s, axis=-1, keepdims=True))
        for s, ops in zip(ss, operands):
            acc = acc + jnp.dot(jnp.exp2(s - m).astype(_BF16), ops[2 + which],
                                preferred_element_type=_F32)
        sink_terms.append(jnp.exp2(sink - m))
    lane = lax.broadcasted_iota(_I32, (qq.shape[0], LANE), 1)
    sink_term = jnp.where(lane < HEAD_DIM, sink_terms[0], sink_terms[1])
    return acc[:, :LANE] / (acc[:, LANE:] + sink_term)


def _group_attend(q_ref, o_ref, hk, rows, operands, masks, sink_ref):
    n_rows = rows.stop - rows.start
    pairs = [slice((2 * hk + j) * LANE, (2 * hk + j + 1) * LANE) for j in range(GROUP // 2)]
    qq = jnp.concatenate([q_ref[rows, sl] for sl in pairs], axis=0)
    first = lax.broadcasted_iota(_I32, (qq.shape[0], 1), 0) < n_rows
    sink_a = jnp.where(first, sink_ref[GROUP * hk], sink_ref[GROUP * hk + 2]) * LOG2E
    sink_b = jnp.where(first, sink_ref[GROUP * hk + 1], sink_ref[GROUP * hk + 3]) * LOG2E
    out = _pair_attend(qq, operands, masks, sink_a, sink_b)
    for j, sl in enumerate(pairs):
        o_ref[rows, sl] = out[j * n_rows:(j + 1) * n_rows].astype(_BF16)


def _attn_kernel(sink_ref, q_ref, kv_own_ref, kv_seq_ref, ck_ref, cv_ref, o_ref):
    i = pl.program_id(0)
    n_ctx_steps = N_CTX // TQ

    @pl.when(i < n_ctx_steps)
    def _():
        for hk in range(N_KV_HEADS):
            ks = slice(hk * HEAD_DIM, (hk + 1) * HEAD_DIM)
            vs = slice(KV_DIM + hk * HEAD_DIM, KV_DIM + (hk + 1) * HEAD_DIM)
            own = _pair_operands(kv_own_ref[:, ks], kv_own_ref[:, vs])
            _group_attend(q_ref, o_ref, hk, slice(0, TQ), [own], [None], sink_ref)

    @pl.when(i >= n_ctx_steps)
    def _():
        qb = (i - n_ctx_steps) % (DEC_SEQ // TQ)
        stacked = (GROUP // 2) * TQ_SUB
        starts, masks = [], []
        for sb in range(TQ // TQ_SUB):
            q_start = qb * TQ + sb * TQ_SUB
            k_start = pl.multiple_of(jnp.clip(q_start - WINDOW, 0, DEC_SEQ - KWIN), WINDOW)
            qpos = q_start + lax.broadcasted_iota(_I32, (stacked, KWIN), 0) % TQ_SUB
            kpos = k_start + lax.broadcasted_iota(_I32, (stacked, KWIN), 1)
            starts.append(k_start)
            masks.append(jnp.abs(kpos - qpos) <= WINDOW)
        for hk in range(N_KV_HEADS):
            ks = slice(hk * HEAD_DIM, (hk + 1) * HEAD_DIM)
            vs = slice(KV_DIM + hk * HEAD_DIM, KV_DIM + (hk + 1) * HEAD_DIM)
            cached = _pair_operands(ck_ref[:, ks], cv_ref[:, ks])
            for sb in range(TQ // TQ_SUB):
                local = _pair_operands(kv_seq_ref[pl.ds(starts[sb], KWIN), ks],
                                       kv_seq_ref[pl.ds(starts[sb], KWIN), vs])
                _group_attend(q_ref, o_ref, hk, slice(sb * TQ_SUB, (sb + 1) * TQ_SUB),
                              [local, cached], [masks[sb], None], sink_ref)


def _attention(q, kv, cache_k, cache_v, sink, layer):
    n_ctx_steps = N_CTX // TQ
    nq = DEC_SEQ // TQ
    kv_off = N_CTX // DEC_SEQ

    def lat_batch(i):
        return jnp.maximum(i - n_ctx_steps, 0) // nq

    return pl.pallas_call(
        _attn_kernel,
        grid=(N_TOK // TQ,),
        in_specs=[
            pl.BlockSpec(memory_space=pltpu.SMEM),
            pl.BlockSpec((TQ, Q_DIM), lambda i: (i, 0)),
            pl.BlockSpec((TQ, 2 * KV_DIM), lambda i: (i, 0)),
            pl.BlockSpec((DEC_SEQ, 2 * KV_DIM), lambda i: (kv_off + lat_batch(i), 0)),
            pl.BlockSpec((None, None, PAST_LEN, KV_DIM), lambda i: (lat_batch(i), layer, 0, 0)),
            pl.BlockSpec((None, None, PAST_LEN, KV_DIM), lambda i: (lat_batch(i), layer, 0, 0)),
        ],
        out_specs=pl.BlockSpec((TQ, Q_DIM), lambda i: (i, 0)),
        out_shape=jax.ShapeDtypeStruct((N_TOK, Q_DIM), _BF16),
        compiler_params=pltpu.CompilerParams(
            dimension_semantics=("arbitrary",), vmem_limit_bytes=VMEM_LIMIT),
        name="attention",
    )(sink, q, kv, kv, cache_k, cache_v)


def _conv_kernel(prev_ref, cur_ref, next_ref, w_ref, b_ref, lg_ref, lb_ref, y_ref, pad_ref, sh_ref):
    i = pl.program_id(0)
    n_ctx_tiles = N_CTX // SEQ
    tiles_per_seq = jnp.where(i < n_ctx_tiles, 1, DEC_SEQ // SEQ)
    j = jnp.where(i < n_ctx_tiles, 0, (i - n_ctx_tiles) % (DEC_SEQ // SEQ))
    pad_ref[0:HALO, :] = jnp.where(j > 0, prev_ref[...], 0.0)
    pad_ref[HALO:HALO + SEQ, :] = cur_ref[...]
    pad_ref[HALO + SEQ:HALO + SEQ + HALO, :] = jnp.where(j < tiles_per_seq - 1, next_ref[...], 0.0)

    for r in range(SUBLANE):
        sh_ref[r] = pad_ref[r:r + CONV_SPAN, :]

    rows = 64
    for r0 in range(0, SEQ, rows):
        acc = jnp.zeros((rows, C_CONV), _F32) + b_ref[...]
        for t in range(CONV_WIDTH):
            off = HALO - CONV_PAD + t
            start = (off // SUBLANE) * SUBLANE + r0
            acc = acc + sh_ref[off % SUBLANE, start:start + rows, :] * w_ref[t:t + 1, :]
        mu = jnp.mean(acc, axis=-1, keepdims=True)
        d = acc - mu
        var = jnp.mean(d * d, axis=-1, keepdims=True)
        y = d * lax.rsqrt(var + EPS) * lg_ref[...] + lb_ref[...]
        y_ref[r0:r0 + rows, :] = (y * jax.nn.sigmoid(y)).astype(_BF16)


def _conv_branch(u, conv_w, conv_b, ln_g, ln_b):
    n_tiles = N_TOK // SEQ
    hb = SEQ // HALO
    last = N_TOK // HALO - 1
    return pl.pallas_call(
        _conv_kernel,
        grid=(n_tiles,),
        in_specs=[
            pl.BlockSpec((HALO, C_CONV), lambda i: (jnp.maximum(i * hb - 1, 0), 0)),
            pl.BlockSpec((SEQ, C_CONV), lambda i: (i, 0)),
            pl.BlockSpec((HALO, C_CONV), lambda i: (jnp.minimum((i + 1) * hb, last), 0)),
            pl.BlockSpec((CONV_WIDTH, C_CONV), lambda i: (0, 0)),
            pl.BlockSpec((1, C_CONV), lambda i: (0, 0)),
            pl.BlockSpec((1, C_CONV), lambda i: (0, 0)),
            pl.BlockSpec((1, C_CONV), lambda i: (0, 0)),
        ],
        out_specs=pl.BlockSpec((SEQ, C_CONV), lambda i: (i, 0)),
        out_shape=jax.ShapeDtypeStruct((N_TOK, C_CONV), _BF16),
        scratch_shapes=[pltpu.VMEM((SEQ + 2 * HALO, C_CONV), _F32),
                        pltpu.VMEM((SUBLANE, CONV_SPAN, C_CONV), _F32)],
        compiler_params=pltpu.CompilerParams(
            dimension_semantics=("arbitrary",), vmem_limit_bytes=VMEM_LIMIT),
        name="conv_branch",
    )(u, u, u, conv_w, conv_b, ln_g, ln_b)


def _mix_kernel(x_ref, mod_ref, att_ref, cv_ref, sg_ref, waf_ref, wcf_ref, wof_ref,
                gpost_ref, gffn_ref, wr_ref, wrlo_ref, br_ref,
                x1_ref, xp_ref, qw_ref, meta_ref, wa_ref, wc_ref, wo_ref):
    @pl.when(pl.program_id(0) == 0)
    def _():
        wa_ref[...] = waf_ref[...].astype(_BF16)
        wc_ref[...] = wcf_ref[...].astype(_BF16)
        wo_ref[...] = wof_ref[...].astype(_BF16)

    a = jnp.dot(att_ref[...], wa_ref[...], preferred_element_type=_F32)
    cv = jnp.dot(cv_ref[...], wc_ref[...], preferred_element_type=_F32)
    m = sg_ref[:, 0:D_MODEL].astype(_F32) * a + sg_ref[:, D_MODEL:].astype(_F32) * cv
    mix = jnp.dot(m.astype(_BF16), wo_ref[...], preferred_element_type=_F32)
    gt1 = mod_ref[:, 2 * D_MODEL:3 * D_MODEL]
    sh2 = mod_ref[:, 3 * D_MODEL:4 * D_MODEL]
    sc2 = mod_ref[:, 4 * D_MODEL:5 * D_MODEL]
    x1 = x_ref[...] + gt1 * _rms(mix, gpost_ref[...])
    x1_ref[...] = x1
    h2 = _rms(x1, gffn_ref[...]) * (1.0 + sc2) + sh2
    h2b = h2.astype(_BF16)

    h2_lo = (h2 - h2b.astype(_F32)).astype(_BF16)
    logits = (jnp.dot(h2b, wr_ref[...], preferred_element_type=_F32)
              + jnp.dot(h2_lo, wr_ref[...], preferred_element_type=_F32)
              + jnp.dot(h2b, wrlo_ref[...], preferred_element_type=_F32) + br_ref[...])
    lane = lax.broadcasted_iota(_I32, (TB, LANE), 1).astype(_F32)
    member = jnp.zeros((TB, LANE), _F32)
    hots, exps = [], []
    top = None
    total = jnp.zeros((TB, 1), _F32)
    for k in range(TOP_K):
        mval = jnp.max(logits, axis=-1, keepdims=True)
        sel = jnp.min(jnp.where(logits == mval, lane, float(LANE)), axis=-1, keepdims=True)
        if top is None:
            top = mval
        e = jnp.exp(mval - top)
        total = total + e
        hot = lane == sel
        hots.append(hot)
        exps.append(e)
        member = member + jnp.where(hot, 1.0, 0.0)
        logits = jnp.where(hot, -jnp.inf, logits)

    r_i = lax.broadcasted_iota(_I32, (TB, TB), 0)
    c_i = lax.broadcasted_iota(_I32, (TB, TB), 1)
    lower = jnp.where(r_i > c_i, 1.0, 0.0).astype(_BF16)
    rank = jnp.dot(lower, member.astype(_BF16), preferred_element_type=_F32)
    count = jnp.sum(member, axis=0, keepdims=True)
    units = jnp.floor((count + float(UNIT - 1)) * (1.0 / UNIT))
    r_l = lax.broadcasted_iota(_I32, (LANE, LANE), 0)
    c_l = lax.broadcasted_iota(_I32, (LANE, LANE), 1)
    upper = jnp.where(r_l < c_l, 1.0, 0.0).astype(_BF16)
    unit_off = jnp.dot(jnp.broadcast_to(units, (SUBLANE, LANE)).astype(_BF16), upper,
                       preferred_element_type=_F32)[0:1, :]
    base = unit_off * float(UNIT) + rank

    slot_lane = lax.broadcasted_iota(_I32, (TB, SLOTS), 1).astype(_F32)
    qw = jnp.zeros((TB, SLOTS), _F32)
    hit = jnp.zeros((TB, SLOTS), _F32)
    for k in range(TOP_K):
        slot = jnp.sum(jnp.where(hots[k], base, 0.0), axis=-1, keepdims=True)
        here = slot_lane == slot
        qw = jnp.where(here, exps[k] / total, qw)
        hit = jnp.where(here, 1.0, hit)
    qw_ref[...] = qw.astype(_BF16)

    xp_ref[...] = lax.dot_general(hit.astype(_BF16), h2b, (((0,), (0,)), ((), ())),
                                  preferred_element_type=_F32).astype(_BF16)

    sub = lax.broadcasted_iota(_I32, (SUBLANE, LANE), 0)
    meta = jnp.where(sub == 0, units, jnp.where(sub == 1, unit_off, 0.0))
    meta_ref[...] = meta.astype(_I32)


def _mix(x, mod3, att, cvn, sg, wa, wc, wo, g_post, g_ffn, wr_hi, wr_lo, b_router_pad, layer):
    full = lambda shape: pl.BlockSpec(shape, lambda i: (0,) * len(shape))
    layer_weight = lambda rows: pl.BlockSpec((None, rows, D_MODEL), lambda i: (layer, 0, 0),
                                             pipeline_mode=pl.Buffered(1))
    return pl.pallas_call(
        _mix_kernel,
        grid=(N_BLOCKS,),
        in_specs=[
            pl.BlockSpec((TB, D_MODEL), lambda i: (i, 0)),
            pl.BlockSpec((None, 1, 6 * D_MODEL),
                         lambda i: (layer * COND_ROWS + _cond_index(i, TB), 0, 0)),
            pl.BlockSpec((TB, Q_DIM), lambda i: (i, 0)),
            pl.BlockSpec((TB, C_CONV), lambda i: (i, 0)),
            pl.BlockSpec((TB, 2 * D_MODEL), lambda i: (i, 0)),
            layer_weight(Q_DIM),
            layer_weight(C_CONV),
            layer_weight(D_MODEL),
            full((1, D_MODEL)),
            full((1, D_MODEL)),
            full((D_MODEL, LANE)),
            full((D_MODEL, LANE)),
            full((1, LANE)),
        ],
        out_specs=[
            pl.BlockSpec((TB, D_MODEL), lambda i: (i, 0)),
            pl.BlockSpec((SLOTS, D_MODEL), lambda i: (i, 0)),
            pl.BlockSpec((TB, SLOTS), lambda i: (i, 0)),
            pl.BlockSpec((None, SUBLANE, LANE), lambda i: (i, 0, 0)),
        ],
        out_shape=[
            jax.ShapeDtypeStruct((N_TOK, D_MODEL), _F32),
            jax.ShapeDtypeStruct((N_BLOCKS * SLOTS, D_MODEL), _BF16),
            jax.ShapeDtypeStruct((N_TOK, SLOTS), _BF16),
            jax.ShapeDtypeStruct((N_BLOCKS, SUBLANE, LANE), _I32),
        ],
        scratch_shapes=[pltpu.VMEM((Q_DIM, D_MODEL), _BF16), pltpu.VMEM((C_CONV, D_MODEL), _BF16),
                        pltpu.VMEM((D_MODEL, D_MODEL), _BF16)],
        compiler_params=pltpu.CompilerParams(
            dimension_semantics=("arbitrary",), vmem_limit_bytes=VMEM_LIMIT),
        name="mix_router",
    )(x, mod3, att, cvn, sg, wa, wc, wo, g_post, g_ffn, wr_hi, wr_lo, b_router_pad)


def _plan(meta):
    units = meta[:, 0, :N_EXPERTS]
    seg_off = meta[:, 1, :N_EXPERTS]
    tiles = (jnp.sum(units, axis=0) + TILE_UNITS - 1) // TILE_UNITS
    tile_end = jnp.cumsum(tiles)
    n_tiles = tile_end[-1]
    region = (tile_end - tiles) * TILE_UNITS
    dst = region[None, :] + jnp.cumsum(units, axis=0) - units
    src = jnp.arange(N_BLOCKS, dtype=_I32)[:, None] * BLOCK_UNITS + seg_off

    tile_first = jnp.concatenate([jnp.zeros((1,), _I32), tile_end.astype(_I32)])

    dst_f, len_f, src_f = dst.reshape(1, -1), units.reshape(1, -1), src.reshape(1, -1)
    d = jnp.arange(N_ETILES * TILE_UNITS, dtype=_I32)[:, None]
    in_seg = (dst_f <= d) & (d < dst_f + len_f)
    src_unit = jnp.sum(jnp.where(in_seg, src_f + d - dst_f, 0), axis=1)

    u = jnp.arange(BLOCK_UNITS, dtype=_I32)[None, :, None]
    so, un = seg_off[:, None, :], units[:, None, :]
    in_blk = (so <= u) & (u < so + un)
    back_unit = jnp.sum(jnp.where(in_blk, dst[:, None, :] + u - so, 0), axis=2)
    back_unit = jnp.concatenate([back_unit.reshape(-1).astype(_I32),
                                 jnp.zeros(((COMBINE_BUFS - 1) * BLOCK_UNITS,), _I32)])
    return tile_first, n_tiles.reshape(1).astype(_I32), src_unit.astype(_I32), back_unit


def _unit_gather(src_hbm, unit_ref, first, n_units, dst_buf, sem):
    for i in range(n_units):
        row = pl.multiple_of(unit_ref[first + i] * UNIT, UNIT)
        pltpu.make_async_copy(src_hbm.at[pl.ds(row, UNIT), :],
                              dst_buf.at[pl.ds(i * UNIT, UNIT), :], sem).start(priority=GATHER_PRIORITY)


def _unit_gather_wait(src_hbm, n_units, dst_buf, sem):
    pltpu.make_async_copy(src_hbm.at[pl.ds(0, n_units * UNIT), :], dst_buf, sem).wait()


def _tile_write(ybuf_slot, ys_hbm, tile, sem):
    row = pl.multiple_of(tile * TE, TE)
    return pltpu.make_async_copy(ybuf_slot, ys_hbm.at[pl.ds(row, TE), :], sem)


def _weight_chunk(wgu_hbm, wd_hbm, layer, expert, chunk, wgu_f32, wd_f32, buf, sem):
    r = pl.multiple_of(chunk * W_ROWS, W_ROWS)
    return (pltpu.make_async_copy(wgu_hbm.at[layer, expert, pl.ds(r, W_ROWS), :],
                                  wgu_f32.at[buf, pl.ds(r, W_ROWS), :], sem),
            pltpu.make_async_copy(wd_hbm.at[layer, expert, pl.ds(r, W_ROWS), :],
                                  wd_f32.at[buf, pl.ds(r, W_ROWS), :], sem))


def _expert_kernel(layer, first_ref, nt_ref, src_ref, xp_hbm, wgu_hbm, bgu_ref, wd_hbm, bd_ref,
                   ys_hbm, wgu_f32, wd_f32, wgu_bf, wd_bf, xbuf, ybuf, wsem, xsem, ysem):
    e = pl.program_id(0)
    n_live = nt_ref[0]
    t_lo = first_ref[e]
    t_hi = first_ref[e + 1]
    buf = e % 2
    has_next = e + 1 < N_EXPERTS

    def x_gather(tile):
        _unit_gather(xp_hbm, src_ref, tile * TILE_UNITS, TILE_UNITS,
                     xbuf.at[tile % X_BUFS], xsem.at[tile % X_BUFS])

    def start_chunk(expert, chunk, into):
        for cp in _weight_chunk(wgu_hbm, wd_hbm, layer, expert, chunk, wgu_f32, wd_f32,
                                into, wsem.at[into]):
            cp.start()

    @pl.when(e == 0)
    def _():
        for ahead in range(X_BUFS - 1):
            x_gather(ahead)
        for c in range(W_CHUNKS):
            start_chunk(0, c, 0)

    pltpu.make_async_copy(wgu_hbm.at[layer, e], wgu_f32.at[buf], wsem.at[buf]).wait()
    pltpu.make_async_copy(wd_hbm.at[layer, e], wd_f32.at[buf], wsem.at[buf]).wait()

    @pl.when(t_hi > t_lo)
    def _():
        wgu_bf[...] = wgu_f32[buf].astype(_BF16)
        wd_bf[...] = wd_f32[buf].astype(_BF16)

    def tile_body(t, carry):
        slot = t % 2

        x_gather(t + X_BUFS - 1)

        @pl.when(jnp.logical_and(has_next, t - t_lo < W_CHUNKS))
        def _():
            start_chunk(e + 1, t - t_lo, 1 - buf)

        _unit_gather_wait(xp_hbm, TILE_UNITS, xbuf.at[t % X_BUFS], xsem.at[t % X_BUFS])
        gu = jnp.dot(xbuf[t % X_BUFS], wgu_bf[...],
                     preferred_element_type=_F32) + bgu_ref[...]
        gate = jnp.minimum(gu[:, :D_FF], SWIGLU_LIMIT)
        lin = jnp.clip(gu[:, D_FF:], -SWIGLU_LIMIT, SWIGLU_LIMIT)
        act = gate * jax.nn.sigmoid(SWIGLU_ALPHA * gate) * (lin + 1.0)
        y = jnp.dot(act.astype(_BF16), wd_bf[...], preferred_element_type=_F32) + bd_ref[...]

        @pl.when(t >= 2)
        def _():
            _tile_write(ybuf.at[slot], ys_hbm, t - 2, ysem.at[slot]).wait()

        ybuf[slot] = y.astype(_BF16)
        _tile_write(ybuf.at[slot], ys_hbm, t, ysem.at[slot]).start(priority=GATHER_PRIORITY)
        return carry

    lax.fori_loop(t_lo, t_hi, tile_body, 0)

    for c in range(W_CHUNKS):
        @pl.when(jnp.logical_and(has_next, c >= t_hi - t_lo))
        def _():
            start_chunk(e + 1, c, 1 - buf)

    @pl.when(e == N_EXPERTS - 1)
    def _():
        for ahead in range(X_BUFS - 1):
            t = n_live + ahead
            _unit_gather_wait(xp_hbm, TILE_UNITS, xbuf.at[t % X_BUFS], xsem.at[t % X_BUFS])
        for back in (2, 1):
            @pl.when(n_live >= back)
            def _():
                t = n_live - back
                _tile_write(ybuf.at[t % 2], ys_hbm, t, ysem.at[t % 2]).wait()

        ybuf[0] = jnp.zeros((TE, D_MODEL), _BF16)

        def zero_start(t, carry):
            _tile_write(ybuf.at[0], ys_hbm, t, ysem.at[0]).start()
            return carry

        def zero_wait(t, carry):
            _tile_write(ybuf.at[0], ys_hbm, t, ysem.at[0]).wait()
            return carry

        lax.fori_loop(n_live, N_ETILES, zero_start, 0)
        lax.fori_loop(n_live, N_ETILES, zero_wait, 0)


def _experts(tile_first, n_tiles, src_unit, xp, w_gate_up, b_gate_up, w_down, b_down, layer):
    grid_spec = pltpu.PrefetchScalarGridSpec(
        num_scalar_prefetch=3,
        grid=(N_EXPERTS,),
        in_specs=[
            pl.BlockSpec(memory_space=pl.ANY),
            pl.BlockSpec(memory_space=pl.ANY),
            pl.BlockSpec((None, None, 1, 2 * D_FF), lambda e, tf, nt, su: (layer, e, 0, 0)),
            pl.BlockSpec(memory_space=pl.ANY),
            pl.BlockSpec((None, None, 1, D_MODEL), lambda e, tf, nt, su: (layer, e, 0, 0)),
        ],
        out_specs=pl.BlockSpec(memory_space=pl.ANY),
        scratch_shapes=[
            pltpu.VMEM((2, D_MODEL, 2 * D_FF), _F32),
            pltpu.VMEM((2, D_FF, D_MODEL), _F32),
            pltpu.VMEM((D_MODEL, 2 * D_FF), _BF16),
            pltpu.VMEM((D_FF, D_MODEL), _BF16),
            pltpu.VMEM((X_BUFS, TE, D_MODEL), _BF16),
            pltpu.VMEM((2, TE, D_MODEL), _BF16),
            pltpu.SemaphoreType.DMA((2,)),
            pltpu.SemaphoreType.DMA((X_BUFS,)),
            pltpu.SemaphoreType.DMA((2,)),
        ],
    )
    return pl.pallas_call(
        functools.partial(_expert_kernel, layer),
        grid_spec=grid_spec,
        out_shape=jax.ShapeDtypeStruct((N_ETILES * TE, D_MODEL), _BF16),
        compiler_params=pltpu.CompilerParams(
            dimension_semantics=("arbitrary",), vmem_limit_bytes=VMEM_LIMIT),
        name="experts",
    )(tile_first, n_tiles, src_unit, xp, w_gate_up,
      b_gate_up.reshape(DEPTH, N_EXPERTS, 1, 2 * D_FF), w_down,
      b_down.reshape(DEPTH, N_EXPERTS, 1, D_MODEL))


def _combine_kernel(split_out, back_ref, ys_hbm, x_ref, mod_ref, qw_ref, g_ref, *rest):
    outs, ybuf, sem = rest[:-2], rest[-2], rest[-1]
    b = pl.program_id(0)

    def gather(block):
        _unit_gather(ys_hbm, back_ref, block * BLOCK_UNITS, BLOCK_UNITS,
                     ybuf.at[block % COMBINE_BUFS], sem.at[block % COMBINE_BUFS])

    def gather_wait(block):
        _unit_gather_wait(ys_hbm, BLOCK_UNITS, ybuf.at[block % COMBINE_BUFS],
                          sem.at[block % COMBINE_BUFS])

    @pl.when(b == 0)
    def _():
        for ahead in range(COMBINE_BUFS - 1):
            gather(ahead)

    gather(b + COMBINE_BUFS - 1)
    gather_wait(b)
    moe = jnp.dot(qw_ref[...], ybuf[b % COMBINE_BUFS], preferred_element_type=_F32)

    @pl.when(b == N_BLOCKS - 1)
    def _():
        for ahead in range(1, COMBINE_BUFS):
            gather_wait(b + ahead)

    gt2 = mod_ref[:, 5 * D_MODEL:6 * D_MODEL]
    res = x_ref[...] + gt2 * _rms(moe, g_ref[...])
    if split_out:
        @pl.when(b < N_CTX // TB)
        def _():
            outs[0][...] = res

        @pl.when(b >= N_CTX // TB)
        def _():
            outs[1][...] = res
    else:
        outs[0][...] = res


def _combine(back_unit, ys, x1, mod3, qw, g_post, layer, split_out):
    n_ctx_blocks = N_CTX // TB
    if split_out:
        out_specs = [
            pl.BlockSpec((TB, D_MODEL), lambda b, bu: (jnp.minimum(b, n_ctx_blocks - 1), 0)),
            pl.BlockSpec((TB, D_MODEL), lambda b, bu: (jnp.maximum(b - n_ctx_blocks, 0), 0)),
        ]
        out_shape = [jax.ShapeDtypeStruct((N_CTX, D_MODEL), _F32),
                     jax.ShapeDtypeStruct((N_LAT, D_MODEL), _F32)]
    else:
        out_specs = [pl.BlockSpec((TB, D_MODEL), lambda b, bu: (b, 0))]
        out_shape = [jax.ShapeDtypeStruct((N_TOK, D_MODEL), _F32)]
    grid_spec = pltpu.PrefetchScalarGridSpec(
        num_scalar_prefetch=1,
        grid=(N_BLOCKS,),
        in_specs=[
            pl.BlockSpec(memory_space=pl.ANY),
            pl.BlockSpec((TB, D_MODEL), lambda b, bu: (b, 0)),
            pl.BlockSpec((None, 1, 6 * D_MODEL),
                         lambda b, bu: (layer * COND_ROWS + _cond_index(b, TB), 0, 0)),
            pl.BlockSpec((TB, SLOTS), lambda b, bu: (b, 0)),
            pl.BlockSpec((1, D_MODEL), lambda b, bu: (0, 0)),
        ],
        out_specs=out_specs,
        scratch_shapes=[
            pltpu.VMEM((COMBINE_BUFS, SLOTS, D_MODEL), _BF16),
            pltpu.SemaphoreType.DMA((COMBINE_BUFS,)),
        ],
    )
    return pl.pallas_call(
        functools.partial(_combine_kernel, split_out),
        grid_spec=grid_spec,
        out_shape=out_shape,
        compiler_params=pltpu.CompilerParams(
            dimension_semantics=("arbitrary",), vmem_limit_bytes=VMEM_LIMIT),
        name="combine_residual",
    )(back_unit, ys, x1, mod3, qw, g_post)


def _rope_tables():
    pos = jnp.arange(DEC_SEQ)
    row = (pos // GRID_W).astype(_F32)
    col = (pos % GRID_W).astype(_F32)
    inv = ROPE_THETA ** (-jnp.arange(ROPE_FREQS, dtype=_F32) / ROPE_FREQS)
    ang_r = row[:, None] * inv[None, :]
    ang_c = col[:, None] * inv[None, :]
    cos = jnp.concatenate([jnp.cos(ang_r)] * 2 + [jnp.cos(ang_c)] * 2, axis=-1)
    sin = jnp.concatenate([-jnp.sin(ang_r), jnp.sin(ang_r), -jnp.sin(ang_c), jnp.sin(ang_c)], axis=-1)
    reps = LANE // HEAD_DIM
    return jnp.tile(cos, (1, reps)), jnp.tile(sin, (1, reps))


def kernel(x_prompt, x_sample, cache_k, cache_v, c, c_ctx, w_ada, b_ada, g_pre_mix, g_post_mix,
           g_pre_ffn, g_post_ffn, w_in, attn_sink, w_attn_o, conv_w, conv_b, conv_ln_g, conv_ln_b,
           w_conv_o, w_out, w_router, b_router, w_gate_up, b_gate_up, w_down, b_down):
    cond =jnp.concatenate([c_ctx[None, :], c, jnp.zeros((COND_ROWS - N_COND, D_MODEL), _F32)], axis=0)
    mod = _modulation(cond, w_ada, b_ada)
    mod3 = mod.reshape(DEPTH * COND_ROWS, 1, 6 * D_MODEL)
    cos_t, sin_t = _rope_tables()
    ck = cache_k.reshape(DEC_BATCH, DEPTH, PAST_LEN, KV_DIM)
    cv = cache_v.reshape(DEC_BATCH, DEPTH, PAST_LEN, KV_DIM)
    w_router_pad = jnp.pad(w_router, ((0, 0), (0, 0), (0, LANE - N_EXPERTS)))
    wr_hi = w_router_pad.astype(_BF16)
    wr_lo = (w_router_pad - wr_hi.astype(_F32)).astype(_BF16)
    b_router_pad = jnp.pad(b_router, ((0, 0), (0, LANE - N_EXPERTS)), constant_values=-jnp.inf)

    new_k, new_v = [], []
    x = None
    for l in range(DEPTH):
        row = lambda a: a[l][None, :]
        if l == 0:
            q, kv, u, sg, x = _inproj(x_prompt.reshape(N_CTX, D_MODEL), x_sample.reshape(N_LAT, D_MODEL),
                                      0, mod3, row(g_pre_mix), w_in, cos_t, sin_t, l)
        else:
            q, kv, u, sg = _inproj(x, x, N_CTX // TM, mod3, row(g_pre_mix), w_in, cos_t, sin_t, l)
        new_k.append(kv[:N_CTX, :KV_DIM].reshape(BATCH, SEQ, N_KV_HEADS, HEAD_DIM))
        new_v.append(kv[:N_CTX, KV_DIM:].reshape(BATCH, SEQ, N_KV_HEADS, HEAD_DIM))
        att = _attention(q, kv, ck, cv, attn_sink[l], l)
        cvn = _conv_branch(u, conv_w[l], row(conv_b), row(conv_ln_g), row(conv_ln_b))
        x1, xp, qw, meta = _mix(
            x, mod3, att, cvn, sg, w_attn_o, w_conv_o, w_out, row(g_post_mix), row(g_pre_ffn),
            wr_hi[l], wr_lo[l], row(b_router_pad), l)
        tile_first, n_tiles, src_unit, back_unit = _plan(meta)
        ys = _experts(tile_first, n_tiles, src_unit, xp, w_gate_up, b_gate_up, w_down, b_down, l)
        outs = _combine(back_unit, ys, x1, mod3, qw, row(g_post_ffn), l, l == DEPTH - 1)
        x = outs[0]

    y_prompt = outs[0].reshape(BATCH, SEQ, D_MODEL)
    y_sample = outs[1].reshape(DEC_BATCH, DEC_SEQ, D_MODEL)
    return (y_prompt, y_sample, jnp.stack(new_k, axis=1), jnp.stack(new_v, axis=1))
```

```python
import functools

import jax
import jax.numpy as jnp
from jax import lax
from jax.experimental import pallas as pl
from jax.experimental.pallas import tpu as pltpu

D_MODEL = 1024
BATCH = 16
SEQ = 256
DEPTH = 2
DEC_BATCH = 2
DEC_SEQ = 2048
PAST_LEN = 256
GRID_W = 64
N_HEADS = 16
N_KV_HEADS = 4
GROUP = N_HEADS // N_KV_HEADS
HEAD_DIM = 64
Q_DIM = N_HEADS * HEAD_DIM
KV_DIM = N_KV_HEADS * HEAD_DIM
WINDOW = 128
ATTN_SCALE = HEAD_DIM ** -0.5
ROPE_THETA = 10000.0
ROPE_HALF = HEAD_DIM // 2
ROPE_FREQS = ROPE_HALF // 2
C_CONV = D_MODEL // 2
CONV_WIDTH = 31
CONV_PAD = (CONV_WIDTH - 1) // 2
N_EXPERTS = 32
TOP_K = 4
D_FF = D_MODEL
SWIGLU_LIMIT = 7.0
SWIGLU_ALPHA = 1.702
EPS = 1e-6
IN_COLS = Q_DIM + 2 * KV_DIM + 2 * C_CONV + 2 * D_MODEL

N_CTX = BATCH * SEQ
N_LAT = DEC_BATCH * DEC_SEQ
N_TOK = N_CTX + N_LAT
N_COND = 1 + DEC_BATCH
COND_ROWS = 8

LANE = 128
SUBLANE = 8
TM = 512
TQ = 256
TQ_SUB = 256
KWIN = TQ_SUB + 2 * WINDOW
assert GROUP == 4 and 2 * HEAD_DIM == LANE
LOG2E = 1.4426950408889634
QK_SCALE = ATTN_SCALE * LOG2E
HALO = 16
CONV_SPAN = SEQ + ((HALO - CONV_PAD + CONV_WIDTH - 1) // SUBLANE) * SUBLANE
assert CONV_SPAN + SUBLANE - 1 <= SEQ + 2 * HALO

TB = 256
N_BLOCKS = N_TOK // TB
UNIT = 2 * SUBLANE
SLOTS = 1536
BLOCK_UNITS = SLOTS // UNIT
TE = 256
TILE_UNITS = TE // UNIT
N_ETILES = (N_BLOCKS * BLOCK_UNITS) // TILE_UNITS + N_EXPERTS
COMBINE_BUFS = 3
X_BUFS = 3
MAX_LIVE_TILES = (N_BLOCKS * ((TB * TOP_K + N_EXPERTS * (UNIT - 1)) // UNIT)) // TILE_UNITS + N_EXPERTS
assert MAX_LIVE_TILES + X_BUFS - 1 <= N_ETILES
W_CHUNKS = 4
W_ROWS = D_MODEL // W_CHUNKS
assert D_FF == D_MODEL
GATHER_PRIORITY = 1
VMEM_LIMIT = 58 * 1024 * 1024

assert SLOTS >= TB * TOP_K + N_EXPERTS * (UNIT - 1) and SLOTS % UNIT == 0

_F32 = jnp.float32
_BF16 = jnp.bfloat16
_I32 = jnp.int32


def _rms(x, g):
    return x * lax.rsqrt(jnp.mean(x * x, axis=-1, keepdims=True) + EPS) * g


def _cond_index(i, tile):
    n_ctx_tiles = N_CTX // tile
    return jnp.where(i < n_ctx_tiles, 0, 1 + (i - n_ctx_tiles) // (DEC_SEQ // tile))


def _mod_kernel(cond_ref, w_ref, b_ref, out_ref):
    cnd = cond_ref[...]
    s = cnd * jax.nn.sigmoid(cnd)
    out_ref[...] = jnp.dot(s, w_ref[...], precision=lax.Precision.HIGHEST,
                           preferred_element_type=_F32) + b_ref[...]


def _modulation(cond, w_ada, b_ada):
    tn = 1536
    nt = 6 * D_MODEL // tn
    return pl.pallas_call(
        _mod_kernel,
        grid=(DEPTH, nt),
        in_specs=[
            pl.BlockSpec((COND_ROWS, D_MODEL), lambda l, n: (0, 0)),
            pl.BlockSpec((None, D_MODEL, tn), lambda l, n: (l, 0, n)),
            pl.BlockSpec((None, 1, tn), lambda l, n: (l, 0, n)),
        ],
        out_specs=pl.BlockSpec((None, COND_ROWS, tn), lambda l, n: (l, 0, n)),
        out_shape=jax.ShapeDtypeStruct((DEPTH, COND_ROWS, 6 * D_MODEL), _F32),
        compiler_params=pltpu.CompilerParams(
            dimension_semantics=("arbitrary", "arbitrary"), vmem_limit_bytes=VMEM_LIMIT),
        name="modulation",
    )(cond, w_ada, b_ada.reshape(DEPTH, 1, 6 * D_MODEL))


def _rope_chunk(x, cos, sin):
    lane = lax.broadcasted_iota(_I32, x.shape, 1)
    partner = jnp.where((lane & ROPE_FREQS) == 0,
                        pltpu.roll(x, LANE - ROPE_FREQS, 1), pltpu.roll(x, ROPE_FREQS, 1))
    return x * cos + partner * sin


def _inproj_kernel(first, *refs):
    n_in = 9
    xa_ref, xb_ref, mod_ref, g_ref, wf_ref, cos_ref, sin_ref = refs[:7]
    q_ref, kv_ref, u_ref, sg_ref = refs[n_in:n_in + 4]
    new_refs = refs[n_in + 4:n_in + 6]
    rest = refs[n_in + 6:]
    w_ref = rest[-1]
    i = pl.program_id(0)

    @pl.when(i == 0)
    def _():
        w_ref[...] = wf_ref[...].astype(_BF16)

    x = jnp.where(i < N_CTX // TM, xa_ref[...], xb_ref[...])
    if first:
        rest[0][...] = x
    sh = mod_ref[:, 0:D_MODEL]
    sc = mod_ref[:, D_MODEL:2 * D_MODEL]
    h = (_rms(x, g_ref[...]) * (1.0 + sc) + sh).astype(_BF16)

    c0 = 0
    q = jnp.dot(h, w_ref[:, c0:c0 + Q_DIM], preferred_element_type=_F32) * QK_SCALE
    c0 += Q_DIM
    kv = jnp.dot(h, w_ref[:, c0:c0 + 2 * KV_DIM], preferred_element_type=_F32)
    c0 += 2 * KV_DIM
    ua = jnp.dot(h, w_ref[:, c0:c0 + C_CONV], preferred_element_type=_F32)
    c0 += C_CONV
    ub = jnp.dot(h, w_ref[:, c0:c0 + C_CONV], preferred_element_type=_F32)
    c0 += C_CONV
    g = jnp.dot(h, w_ref[:, c0:c0 + 2 * D_MODEL], preferred_element_type=_F32)

    u_ref[...] = ua * jax.nn.sigmoid(ub)
    sg_ref[...] = jax.nn.sigmoid(g).astype(_BF16)

    is_latent = i >= N_CTX // TM

    @pl.when(jnp.logical_not(is_latent))
    def _():
        q_ref[...] = q.astype(_BF16)
        kv_ref[...] = kv
        for which, new_ref in enumerate(new_refs):
            cols = slice(which * KV_DIM, (which + 1) * KV_DIM)
            for r in range(TM // SEQ):
                rows = slice(r * SEQ, (r + 1) * SEQ)
                new_ref[r] = kv[rows, cols]

    @pl.when(is_latent)
    def _():
        cos = cos_ref[...]
        sin = sin_ref[...]
        for j in range(Q_DIM // LANE):
            sl = slice(j * LANE, (j + 1) * LANE)
            q_ref[:, sl] = _rope_chunk(q[:, sl], cos, sin).astype(_BF16)
        for j in range(KV_DIM // LANE):
            sl = slice(j * LANE, (j + 1) * LANE)
            kv_ref[:, sl] = _rope_chunk(kv[:, sl], cos, sin)
        kv_ref[:, KV_DIM:] = kv[:, KV_DIM:]


def _inproj(x_ctx, x_lat, kv_all, mod3, g_pre, w_in, cos_t, sin_t, layer):
    n_ctx_tiles = N_CTX // TM
    lat_tiles = DEC_SEQ // TM
    stack_x = layer == 0
    lat_off = 0 if stack_x else n_ctx_tiles

    def rope_map(i):
        return (jnp.where(i < n_ctx_tiles, 0, (i - n_ctx_tiles) % lat_tiles), 0)

    row_tile = lambda width: pl.BlockSpec((TM, width), lambda i: (i, 0))
    new_kv = pl.BlockSpec((TM // SEQ, None, SEQ, KV_DIM),
                          lambda i: (jnp.minimum(i, n_ctx_tiles - 1), layer, 0, 0))
    new_kv_shape = jax.ShapeDtypeStruct((BATCH, DEPTH, SEQ, KV_DIM), _F32)
    n_fixed_inputs = 7
    out_specs = [row_tile(Q_DIM), row_tile(2 * KV_DIM), row_tile(C_CONV), row_tile(2 * D_MODEL),
                 new_kv, new_kv]
    out_shape = [
        jax.ShapeDtypeStruct((N_TOK, Q_DIM), _BF16),
        jax.ShapeDtypeStruct((N_TOK, 2 * KV_DIM), _F32),
        jax.ShapeDtypeStruct((N_TOK, C_CONV), _F32),
        jax.ShapeDtypeStruct((N_TOK, 2 * D_MODEL), _BF16),
        new_kv_shape, new_kv_shape,
    ]
    if stack_x:
        out_specs.append(row_tile(D_MODEL))
        out_shape.append(jax.ShapeDtypeStruct((N_TOK, D_MODEL), _F32))
    return pl.pallas_call(
        functools.partial(_inproj_kernel, stack_x),
        grid=(N_TOK // TM,),
        in_specs=[
            pl.BlockSpec((TM, D_MODEL), lambda i: (jnp.minimum(i, n_ctx_tiles - 1), 0)),
            pl.BlockSpec((TM, D_MODEL), lambda i: (lat_off + jnp.maximum(i - n_ctx_tiles, 0), 0)),
            pl.BlockSpec((None, 1, 6 * D_MODEL),
                         lambda i: (layer * COND_ROWS + _cond_index(i, TM), 0, 0)),
            pl.BlockSpec((1, D_MODEL), lambda i: (0, 0)),
            pl.BlockSpec((None, D_MODEL, IN_COLS), lambda i: (layer, 0, 0),
                         pipeline_mode=pl.Buffered(1)),
            pl.BlockSpec((TM, LANE), rope_map),
            pl.BlockSpec((TM, LANE), rope_map),
        ] + [pl.BlockSpec(memory_space=pl.ANY)] * len(kv_all),
        out_specs=out_specs,
        out_shape=out_shape,
        input_output_aliases={n_fixed_inputs + j: 4 + j for j in range(len(kv_all))},
        scratch_shapes=[pltpu.VMEM((D_MODEL, IN_COLS), _BF16)],
        compiler_params=pltpu.CompilerParams(
            dimension_semantics=("arbitrary",), vmem_limit_bytes=VMEM_LIMIT),
        name="inproj",
    )(x_ctx, x_lat, mod3, g_pre, w_in, cos_t, sin_t, *kv_all)


def _pair_operands(k, v):
    zero = jnp.zeros_like(k)
    one = jnp.ones_like(v)
    ka = jnp.concatenate([k, zero], axis=1).astype(_BF16)
    kb = jnp.concatenate([zero, k], axis=1).astype(_BF16)
    va = jnp.concatenate([v, zero, one, zero], axis=1).astype(_BF16)
    vb = jnp.concatenate([zero, v, zero, one], axis=1).astype(_BF16)
    return ka, kb, va, vb


def _pair_attend(qq, operands, masks, sink_a, sink_b):
    def scores(which):
        out = []
        for ops, mask in zip(operands, masks):
            s = lax.dot_general(qq, ops[which], (((1,), (1,)), ((), ())),
                                preferred_element_type=_F32)
            out.append(s if mask is None else jnp.where(mask, s, -jnp.inf))
        return out

    acc = jnp.zeros((qq.shape[0], 2 * LANE), _F32)
    sink_terms = []
    for which, sink in ((0, sink_a), (1, sink_b)):
        ss = scores(which)
        m = sink
        for s in ss:
            m = jnp.maximum(m, jnp.max(s, axis=-1, keepdims=True))
        for s, ops in zip(ss, operands):
            acc = acc + jnp.dot(jnp.exp2(s - m).astype(_BF16), ops[2 + which],
                                preferred_element_type=_F32)
        sink_terms.append(jnp.exp2(sink - m))
    lane = lax.broadcasted_iota(_I32, (qq.shape[0], LANE), 1)
    sink_term = jnp.where(lane < HEAD_DIM, sink_terms[0], sink_terms[1])
    return acc[:, :LANE] / (acc[:, LANE:] + sink_term)


def _group_attend(q_ref, o_ref, hk, rows, operands, masks, sink_ref):
    n_rows = rows.stop - rows.start
    pairs = [slice((2 * hk + j) * LANE, (2 * hk + j + 1) * LANE) for j in range(GROUP // 2)]
    qq = jnp.concatenate([q_ref[rows, sl] for sl in pairs], axis=0)
    first = lax.broadcasted_iota(_I32, (qq.shape[0], 1), 0) < n_rows
    sink_a = jnp.where(first, sink_ref[GROUP * hk], sink_ref[GROUP * hk + 2]) * LOG2E
    sink_b = jnp.where(first, sink_ref[GROUP * hk + 1], sink_ref[GROUP * hk + 3]) * LOG2E
    out = _pair_attend(qq, operands, masks, sink_a, sink_b)
    for j, sl in enumerate(pairs):
        o_ref[rows, sl] = out[j * n_rows:(j + 1) * n_rows].astype(_BF16)


def _attn_kernel(sink_ref, q_ref, kv_own_ref, kv_seq_ref, ck_ref, cv_ref, o_ref):
    i = pl.program_id(0)
    n_ctx_steps = N_CTX // TQ

    @pl.when(i < n_ctx_steps)
    def _():
        for hk in range(N_KV_HEADS):
            ks = slice(hk * HEAD_DIM, (hk + 1) * HEAD_DIM)
            vs = slice(KV_DIM + hk * HEAD_DIM, KV_DIM + (hk + 1) * HEAD_DIM)
            own = _pair_operands(kv_own_ref[:, ks], kv_own_ref[:, vs])
            _group_attend(q_ref, o_ref, hk, slice(0, TQ), [own], [None], sink_ref)

    @pl.when(i >= n_ctx_steps)
    def _():
        qb = (i - n_ctx_steps) % (DEC_SEQ // TQ)
        stacked = (GROUP // 2) * TQ_SUB
        starts, masks = [], []
        for sb in range(TQ // TQ_SUB):
            q_start = qb * TQ + sb * TQ_SUB
            k_start = pl.multiple_of(jnp.clip(q_start - WINDOW, 0, DEC_SEQ - KWIN), WINDOW)
            qpos = q_start + lax.broadcasted_iota(_I32, (stacked, KWIN), 0) % TQ_SUB
            kpos = k_start + lax.broadcasted_iota(_I32, (stacked, KWIN), 1)
            starts.append(k_start)
            masks.append(jnp.abs(kpos - qpos) <= WINDOW)
        for hk in range(N_KV_HEADS):
            ks = slice(hk * HEAD_DIM, (hk + 1) * HEAD_DIM)
            vs = slice(KV_DIM + hk * HEAD_DIM, KV_DIM + (hk + 1) * HEAD_DIM)
            cached = _pair_operands(ck_ref[:, ks], cv_ref[:, ks])
            for sb in range(TQ // TQ_SUB):
                local = _pair_operands(kv_seq_ref[pl.ds(starts[sb], KWIN), ks],
                                       kv_seq_ref[pl.ds(starts[sb], KWIN), vs])
                _group_attend(q_ref, o_ref, hk, slice(sb * TQ_SUB, (sb + 1) * TQ_SUB),
                              [local, cached], [masks[sb], None], sink_ref)


def _attention(q, kv, cache_k, cache_v, sink, layer):
    n_ctx_steps = N_CTX // TQ
    nq = DEC_SEQ // TQ
    kv_off = N_CTX // DEC_SEQ

    def lat_batch(i):
        return jnp.maximum(i - n_ctx_steps, 0) // nq

    return pl.pallas_call(
        _attn_kernel,
        grid=(N_TOK // TQ,),
        in_specs=[
            pl.BlockSpec(memory_space=pltpu.SMEM),
            pl.BlockSpec((TQ, Q_DIM), lambda i: (i, 0)),
            pl.BlockSpec((TQ, 2 * KV_DIM), lambda i: (i, 0)),
            pl.BlockSpec((DEC_SEQ, 2 * KV_DIM), lambda i: (kv_off + lat_batch(i), 0)),
            pl.BlockSpec((None, None, PAST_LEN, KV_DIM), lambda i: (lat_batch(i), layer, 0, 0)),
            pl.BlockSpec((None, None, PAST_LEN, KV_DIM), lambda i: (lat_batch(i), layer, 0, 0)),
        ],
        out_specs=pl.BlockSpec((TQ, Q_DIM), lambda i: (i, 0)),
        out_shape=jax.ShapeDtypeStruct((N_TOK, Q_DIM), _BF16),
        compiler_params=pltpu.CompilerParams(
            dimension_semantics=("arbitrary",), vmem_limit_bytes=VMEM_LIMIT),
        name="attention",
    )(sink, q, kv, kv, cache_k, cache_v)


def _conv_kernel(prev_ref, cur_ref, next_ref, w_ref, b_ref, lg_ref, lb_ref, y_ref, pad_ref, sh_ref):
    i = pl.program_id(0)
    n_ctx_tiles = N_CTX // SEQ
    tiles_per_seq = jnp.where(i < n_ctx_tiles, 1, DEC_SEQ // SEQ)
    j = jnp.where(i < n_ctx_tiles, 0, (i - n_ctx_tiles) % (DEC_SEQ // SEQ))
    pad_ref[0:HALO, :] = jnp.where(j > 0, prev_ref[...], 0.0)
    pad_ref[HALO:HALO + SEQ, :] = cur_ref[...]
    pad_ref[HALO + SEQ:HALO + SEQ + HALO, :] = jnp.where(j < tiles_per_seq - 1, next_ref[...], 0.0)

    for r in range(SUBLANE):
        sh_ref[r] = pad_ref[r:r + CONV_SPAN, :]

    rows = 64
    for r0 in range(0, SEQ, rows):
        acc = jnp.zeros((rows, C_CONV), _F32) + b_ref[...]
        for t in range(CONV_WIDTH):
            off = HALO - CONV_PAD + t
            start = (off // SUBLANE) * SUBLANE + r0
            acc = acc + sh_ref[off % SUBLANE, start:start + rows, :] * w_ref[t:t + 1, :]
        mu = jnp.mean(acc, axis=-1, keepdims=True)
        d = acc - mu
        var = jnp.mean(d * d, axis=-1, keepdims=True)
        y = d * lax.rsqrt(var + EPS) * lg_ref[...] + lb_ref[...]
        y_ref[r0:r0 + rows, :] = (y * jax.nn.sigmoid(y)).astype(_BF16)


def _conv_branch(u, conv_w, conv_b, ln_g, ln_b):
    n_tiles = N_TOK // SEQ
    hb = SEQ // HALO
    last = N_TOK // HALO - 1
    return pl.pallas_call(
        _conv_kernel,
        grid=(n_tiles,),
        in_specs=[
            pl.BlockSpec((HALO, C_CONV), lambda i: (jnp.maximum(i * hb - 1, 0), 0)),
            pl.BlockSpec((SEQ, C_CONV), lambda i: (i, 0)),
            pl.BlockSpec((HALO, C_CONV), lambda i: (jnp.minimum((i + 1) * hb, last), 0)),
            pl.BlockSpec((CONV_WIDTH, C_CONV), lambda i: (0, 0)),
            pl.BlockSpec((1, C_CONV), lambda i: (0, 0)),
            pl.BlockSpec((1, C_CONV), lambda i: (0, 0)),
            pl.BlockSpec((1, C_CONV), lambda i: (0, 0)),
        ],
        out_specs=pl.BlockSpec((SEQ, C_CONV), lambda i: (i, 0)),
        out_shape=jax.ShapeDtypeStruct((N_TOK, C_CONV), _BF16),
        scratch_shapes=[pltpu.VMEM((SEQ + 2 * HALO, C_CONV), _F32),
                        pltpu.VMEM((SUBLANE, CONV_SPAN, C_CONV), _F32)],
        compiler_params=pltpu.CompilerParams(
            dimension_semantics=("arbitrary",), vmem_limit_bytes=VMEM_LIMIT),
        name="conv_branch",
    )(u, u, u, conv_w, conv_b, ln_g, ln_b)


def _mix_kernel(x_ref, mod_ref, att_ref, cv_ref, sg_ref, waf_ref, wcf_ref, wof_ref,
                gpost_ref, gffn_ref, wr_ref, wrlo_ref, br_ref,
                x1_ref, xp_ref, qw_ref, meta_ref, wa_ref, wc_ref, wo_ref):
    @pl.when(pl.program_id(0) == 0)
    def _():
        wa_ref[...] = waf_ref[...].astype(_BF16)
        wc_ref[...] = wcf_ref[...].astype(_BF16)
        wo_ref[...] = wof_ref[...].astype(_BF16)

    a = jnp.dot(att_ref[...], wa_ref[...], preferred_element_type=_F32)
    cv = jnp.dot(cv_ref[...], wc_ref[...], preferred_element_type=_F32)
    m = sg_ref[:, 0:D_MODEL].astype(_F32) * a + sg_ref[:, D_MODEL:].astype(_F32) * cv
    mix = jnp.dot(m.astype(_BF16), wo_ref[...], preferred_element_type=_F32)
    gt1 = mod_ref[:, 2 * D_MODEL:3 * D_MODEL]
    sh2 = mod_ref[:, 3 * D_MODEL:4 * D_MODEL]
    sc2 = mod_ref[:, 4 * D_MODEL:5 * D_MODEL]
    x1 = x_ref[...] + gt1 * _rms(mix, gpost_ref[...])
    x1_ref[...] = x1
    h2 = _rms(x1, gffn_ref[...]) * (1.0 + sc2) + sh2
    h2b = h2.astype(_BF16)

    h2_lo = (h2 - h2b.astype(_F32)).astype(_BF16)
    logits = (jnp.dot(h2b, wr_ref[...], preferred_element_type=_F32)
              + jnp.dot(h2_lo, wr_ref[...], preferred_element_type=_F32)
              + jnp.dot(h2b, wrlo_ref[...], preferred_element_type=_F32) + br_ref[...])
    lane = lax.broadcasted_iota(_I32, (TB, LANE), 1).astype(_F32)
    member = jnp.zeros((TB, LANE), _F32)
    hots, exps = [], []
    top = None
    total = jnp.zeros((TB, 1), _F32)
    for k in range(TOP_K):
        mval = jnp.max(logits, axis=-1, keepdims=True)
        sel = jnp.min(jnp.where(logits == mval, lane, float(LANE)), axis=-1, keepdims=True)
        if top is None:
            top = mval
        e = jnp.exp(mval - top)
        total = total + e
        hot = lane == sel
        hots.append(hot)
        exps.append(e)
        member = member + jnp.where(hot, 1.0, 0.0)
        logits = jnp.where(hot, -jnp.inf, logits)

    r_i = lax.broadcasted_iota(_I32, (TB, TB), 0)
    c_i = lax.broadcasted_iota(_I32, (TB, TB), 1)
    lower = jnp.where(r_i > c_i, 1.0, 0.0).astype(_BF16)
    rank = jnp.dot(lower, member.astype(_BF16), preferred_element_type=_F32)
    count = jnp.sum(member, axis=0, keepdims=True)
    units = jnp.floor((count + float(UNIT - 1)) * (1.0 / UNIT))
    r_l = lax.broadcasted_iota(_I32, (LANE, LANE), 0)
    c_l = lax.broadcasted_iota(_I32, (LANE, LANE), 1)
    upper = jnp.where(r_l < c_l, 1.0, 0.0).astype(_BF16)
    unit_off = jnp.dot(jnp.broadcast_to(units, (SUBLANE, LANE)).astype(_BF16), upper,
                       preferred_element_type=_F32)[0:1, :]
    base = unit_off * float(UNIT) + rank

    slot_lane = lax.broadcasted_iota(_I32, (TB, SLOTS), 1).astype(_F32)
    qw = jnp.zeros((TB, SLOTS), _F32)
    hit = jnp.zeros((TB, SLOTS), _F32)
    for k in range(TOP_K):
        slot = jnp.sum(jnp.where(hots[k], base, 0.0), axis=-1, keepdims=True)
        here = slot_lane == slot
        qw = jnp.where(here, exps[k] / total, qw)
        hit = jnp.where(here, 1.0, hit)
    qw_ref[...] = qw.astype(_BF16)

    xp_ref[...] = lax.dot_general(hit.astype(_BF16), h2b, (((0,), (0,)), ((), ())),
                                  preferred_element_type=_F32).astype(_BF16)

    sub = lax.broadcasted_iota(_I32, (SUBLANE, LANE), 0)
    meta = jnp.where(sub == 0, units, jnp.where(sub == 1, unit_off, 0.0))
    meta_ref[...] = meta.astype(_I32)


def _mix(x, mod3, att, cvn, sg, wa, wc, wo, g_post, g_ffn, wr_hi, wr_lo, b_router_pad, layer):
    full = lambda shape: pl.BlockSpec(shape, lambda i: (0,) * len(shape))
    layer_weight = lambda rows: pl.BlockSpec((None, rows, D_MODEL), lambda i: (layer, 0, 0),
                                             pipeline_mode=pl.Buffered(1))
    return pl.pallas_call(
        _mix_kernel,
        grid=(N_BLOCKS,),
        in_specs=[
            pl.BlockSpec((TB, D_MODEL), lambda i: (i, 0)),
            pl.BlockSpec((None, 1, 6 * D_MODEL),
                         lambda i: (layer * COND_ROWS + _cond_index(i, TB), 0, 0)),
            pl.BlockSpec((TB, Q_DIM), lambda i: (i, 0)),
            pl.BlockSpec((TB, C_CONV), lambda i: (i, 0)),
            pl.BlockSpec((TB, 2 * D_MODEL), lambda i: (i, 0)),
            layer_weight(Q_DIM),
            layer_weight(C_CONV),
            layer_weight(D_MODEL),
            full((1, D_MODEL)),
            full((1, D_MODEL)),
            full((D_MODEL, LANE)),
            full((D_MODEL, LANE)),
            full((1, LANE)),
        ],
        out_specs=[
            pl.BlockSpec((TB, D_MODEL), lambda i: (i, 0)),
            pl.BlockSpec((SLOTS, D_MODEL), lambda i: (i, 0)),
            pl.BlockSpec((TB, SLOTS), lambda i: (i, 0)),
            pl.BlockSpec((None, SUBLANE, LANE), lambda i: (i, 0, 0)),
        ],
        out_shape=[
            jax.ShapeDtypeStruct((N_TOK, D_MODEL), _F32),
            jax.ShapeDtypeStruct((N_BLOCKS * SLOTS, D_MODEL), _BF16),
            jax.ShapeDtypeStruct((N_TOK, SLOTS), _BF16),
            jax.ShapeDtypeStruct((N_BLOCKS, SUBLANE, LANE), _I32),
        ],
        scratch_shapes=[pltpu.VMEM((Q_DIM, D_MODEL), _BF16), pltpu.VMEM((C_CONV, D_MODEL), _BF16),
                        pltpu.VMEM((D_MODEL, D_MODEL), _BF16)],
        compiler_params=pltpu.CompilerParams(
            dimension_semantics=("arbitrary",), vmem_limit_bytes=VMEM_LIMIT),
        name="mix_router",
    )(x, mod3, att, cvn, sg, wa, wc, wo, g_post, g_ffn, wr_hi, wr_lo, b_router_pad)


def _plan(meta):
    units = meta[:, 0, :N_EXPERTS]
    seg_off = meta[:, 1, :N_EXPERTS]
    tiles = (jnp.sum(units, axis=0) + TILE_UNITS - 1) // TILE_UNITS
    tile_end = jnp.cumsum(tiles)
    n_tiles = tile_end[-1]
    region = (tile_end - tiles) * TILE_UNITS
    dst = region[None, :] + jnp.cumsum(units, axis=0) - units
    src = jnp.arange(N_BLOCKS, dtype=_I32)[:, None] * BLOCK_UNITS + seg_off

    tile_first = jnp.concatenate([jnp.zeros((1,), _I32), tile_end.astype(_I32)])

    dst_f, len_f, src_f = dst.reshape(1, -1), units.reshape(1, -1), src.reshape(1, -1)
    d = jnp.arange(N_ETILES * TILE_UNITS, dtype=_I32)[:, None]
    in_seg = (dst_f <= d) & (d < dst_f + len_f)
    src_unit = jnp.sum(jnp.where(in_seg, src_f + d - dst_f, 0), axis=1)

    u = jnp.arange(BLOCK_UNITS, dtype=_I32)[None, :, None]
    so, un = seg_off[:, None, :], units[:, None, :]
    in_blk = (so <= u) & (u < so + un)
    back_unit = jnp.sum(jnp.where(in_blk, dst[:, None, :] + u - so, 0), axis=2)
    back_unit = jnp.concatenate([back_unit.reshape(-1).astype(_I32),
                                 jnp.zeros(((COMBINE_BUFS - 1) * BLOCK_UNITS,), _I32)])
    return tile_first, n_tiles.reshape(1).astype(_I32), src_unit.astype(_I32), back_unit


def _unit_gather(src_hbm, unit_ref, first, n_units, dst_buf, sem):
    for i in range(n_units):
        row = pl.multiple_of(unit_ref[first + i] * UNIT, UNIT)
        pltpu.make_async_copy(src_hbm.at[pl.ds(row, UNIT), :],
                              dst_buf.at[pl.ds(i * UNIT, UNIT), :], sem).start(priority=GATHER_PRIORITY)


def _unit_gather_wait(src_hbm, n_units, dst_buf, sem):
    pltpu.make_async_copy(src_hbm.at[pl.ds(0, n_units * UNIT), :], dst_buf, sem).wait()


def _tile_write(ybuf_slot, ys_hbm, tile, sem):
    row = pl.multiple_of(tile * TE, TE)
    return pltpu.make_async_copy(ybuf_slot, ys_hbm.at[pl.ds(row, TE), :], sem)


def _weight_chunk(wgu_hbm, wd_hbm, layer, expert, chunk, wgu_f32, wd_f32, buf, sem):
    r = pl.multiple_of(chunk * W_ROWS, W_ROWS)
    return (pltpu.make_async_copy(wgu_hbm.at[layer, expert, pl.ds(r, W_ROWS), :],
                                  wgu_f32.at[buf, pl.ds(r, W_ROWS), :], sem),
            pltpu.make_async_copy(wd_hbm.at[layer, expert, pl.ds(r, W_ROWS), :],
                                  wd_f32.at[buf, pl.ds(r, W_ROWS), :], sem))


def _expert_kernel(layer, first_ref, nt_ref, src_ref, xp_hbm, wgu_hbm, bgu_ref, wd_hbm, bd_ref,
                   ys_hbm, wgu_f32, wd_f32, wgu_bf, wd_bf, xbuf, ybuf, wsem, xsem, ysem):
    e = pl.program_id(0)
    n_live = nt_ref[0]
    t_lo = first_ref[e]
    t_hi = first_ref[e + 1]
    buf = e % 2
    has_next = e + 1 < N_EXPERTS

    def x_gather(tile):
        _unit_gather(xp_hbm, src_ref, tile * TILE_UNITS, TILE_UNITS,
                     xbuf.at[tile % X_BUFS], xsem.at[tile % X_BUFS])

    def start_chunk(expert, chunk, into):
        for cp in _weight_chunk(wgu_hbm, wd_hbm, layer, expert, chunk, wgu_f32, wd_f32,
                                into, wsem.at[into]):
            cp.start()

    @pl.when(e == 0)
    def _():
        for ahead in range(X_BUFS - 1):
            x_gather(ahead)
        for c in range(W_CHUNKS):
            start_chunk(0, c, 0)

    pltpu.make_async_copy(wgu_hbm.at[layer, e], wgu_f32.at[buf], wsem.at[buf]).wait()
    pltpu.make_async_copy(wd_hbm.at[layer, e], wd_f32.at[buf], wsem.at[buf]).wait()

    @pl.when(t_hi > t_lo)
    def _():
        wgu_bf[...] = wgu_f32[buf].astype(_BF16)
        wd_bf[...] = wd_f32[buf].astype(_BF16)

    def tile_body(t, carry):
        slot = t % 2

        x_gather(t + X_BUFS - 1)

        @pl.when(jnp.logical_and(has_next, t - t_lo < W_CHUNKS))
        def _():
            start_chunk(e + 1, t - t_lo, 1 - buf)

        _unit_gather_wait(xp_hbm, TILE_UNITS, xbuf.at[t % X_BUFS], xsem.at[t % X_BUFS])
        gu = jnp.dot(xbuf[t % X_BUFS], wgu_bf[...],
                     preferred_element_type=_F32) + bgu_ref[...]
        gate = jnp.minimum(gu[:, :D_FF], SWIGLU_LIMIT)
        lin = jnp.clip(gu[:, D_FF:], -SWIGLU_LIMIT, SWIGLU_LIMIT)
        act = gate * jax.nn.sigmoid(SWIGLU_ALPHA * gate) * (lin + 1.0)
        y = jnp.dot(act.astype(_BF16), wd_bf[...], preferred_element_type=_F32) + bd_ref[...]

        @pl.when(t >= 2)
        def _():
            _tile_write(ybuf.at[slot], ys_hbm, t - 2, ysem.at[slot]).wait()

        ybuf[slot] = y.astype(_BF16)
        _tile_write(ybuf.at[slot], ys_hbm, t, ysem.at[slot]).start(priority=GATHER_PRIORITY)
        return carry

    lax.fori_loop(t_lo, t_hi, tile_body, 0)

    for c in range(W_CHUNKS):
        @pl.when(jnp.logical_and(has_next, c >= t_hi - t_lo))
        def _():
            start_chunk(e + 1, c, 1 - buf)

    @pl.when(e == N_EXPERTS - 1)
    def _():
        for ahead in range(X_BUFS - 1):
            t = n_live + ahead
            _unit_gather_wait(xp_hbm, TILE_UNITS, xbuf.at[t % X_BUFS], xsem.at[t % X_BUFS])
        for back in (2, 1):
            @pl.when(n_live >= back)
            def _():
                t = n_live - back
                _tile_write(ybuf.at[t % 2], ys_hbm, t, ysem.at[t % 2]).wait()

        ybuf[0] = jnp.zeros((TE, D_MODEL), _BF16)

        def zero_start(t, carry):
            _tile_write(ybuf.at[0], ys_hbm, t, ysem.at[0]).start()
            return carry

        def zero_wait(t, carry):
            _tile_write(ybuf.at[0], ys_hbm, t, ysem.at[0]).wait()
            return carry

        lax.fori_loop(n_live, N_ETILES, zero_start, 0)
        lax.fori_loop(n_live, N_ETILES, zero_wait, 0)


def _experts(tile_first, n_tiles, src_unit, xp, w_gate_up, b_gate_up, w_down, b_down, layer):
    grid_spec = pltpu.PrefetchScalarGridSpec(
        num_scalar_prefetch=3,
        grid=(N_EXPERTS,),
        in_specs=[
            pl.BlockSpec(memory_space=pl.ANY),
            pl.BlockSpec(memory_space=pl.ANY),
            pl.BlockSpec((None, None, 1, 2 * D_FF), lambda e, tf, nt, su: (layer, e, 0, 0)),
            pl.BlockSpec(memory_space=pl.ANY),
            pl.BlockSpec((None, None, 1, D_MODEL), lambda e, tf, nt, su: (layer, e, 0, 0)),
        ],
        out_specs=pl.BlockSpec(memory_space=pl.ANY),
        scratch_shapes=[
            pltpu.VMEM((2, D_MODEL, 2 * D_FF), _F32),
            pltpu.VMEM((2, D_FF, D_MODEL), _F32),
            pltpu.VMEM((D_MODEL, 2 * D_FF), _BF16),
            pltpu.VMEM((D_FF, D_MODEL), _BF16),
            pltpu.VMEM((X_BUFS, TE, D_MODEL), _BF16),
            pltpu.VMEM((2, TE, D_MODEL), _BF16),
            pltpu.SemaphoreType.DMA((2,)),
            pltpu.SemaphoreType.DMA((X_BUFS,)),
            pltpu.SemaphoreType.DMA((2,)),
        ],
    )
    return pl.pallas_call(
        functools.partial(_expert_kernel, layer),
        grid_spec=grid_spec,
        out_shape=jax.ShapeDtypeStruct((N_ETILES * TE, D_MODEL), _BF16),
        compiler_params=pltpu.CompilerParams(
            dimension_semantics=("arbitrary",), vmem_limit_bytes=VMEM_LIMIT),
        name="experts",
    )(tile_first, n_tiles, src_unit, xp, w_gate_up,
      b_gate_up.reshape(DEPTH, N_EXPERTS, 1, 2 * D_FF), w_down,
      b_down.reshape(DEPTH, N_EXPERTS, 1, D_MODEL))


def _combine_kernel(split_out, back_ref, ys_hbm, x_ref, mod_ref, qw_ref, g_ref, *rest):
    outs, ybuf, sem = rest[:-2], rest[-2], rest[-1]
    b = pl.program_id(0)

    def gather(block):
        _unit_gather(ys_hbm, back_ref, block * BLOCK_UNITS, BLOCK_UNITS,
                     ybuf.at[block % COMBINE_BUFS], sem.at[block % COMBINE_BUFS])

    def gather_wait(block):
        _unit_gather_wait(ys_hbm, BLOCK_UNITS, ybuf.at[block % COMBINE_BUFS],
                          sem.at[block % COMBINE_BUFS])

    @pl.when(b == 0)
    def _():
        for ahead in range(COMBINE_BUFS - 1):
            gather(ahead)

    gather(b + COMBINE_BUFS - 1)
    gather_wait(b)
    moe = jnp.dot(qw_ref[...], ybuf[b % COMBINE_BUFS], preferred_element_type=_F32)

    @pl.when(b == N_BLOCKS - 1)
    def _():
        for ahead in range(1, COMBINE_BUFS):
            gather_wait(b + ahead)

    gt2 = mod_ref[:, 5 * D_MODEL:6 * D_MODEL]
    res = x_ref[...] + gt2 * _rms(moe, g_ref[...])
    if split_out:
        @pl.when(b < N_CTX // TB)
        def _():
            outs[0][...] = res

        @pl.when(b >= N_CTX // TB)
        def _():
            outs[1][...] = res
    else:
        outs[0][...] = res


def _combine(back_unit, ys, x1, mod3, qw, g_post, layer, split_out):
    n_ctx_blocks = N_CTX // TB
    if split_out:
        out_specs = [
            pl.BlockSpec((TB, D_MODEL), lambda b, bu: (jnp.minimum(b, n_ctx_blocks - 1), 0)),
            pl.BlockSpec((TB, D_MODEL), lambda b, bu: (jnp.maximum(b - n_ctx_blocks, 0), 0)),
        ]
        out_shape = [jax.ShapeDtypeStruct((N_CTX, D_MODEL), _F32),
                     jax.ShapeDtypeStruct((N_LAT, D_MODEL), _F32)]
    else:
        out_specs = [pl.BlockSpec((TB, D_MODEL), lambda b, bu: (b, 0))]
        out_shape = [jax.ShapeDtypeStruct((N_TOK, D_MODEL), _F32)]
    grid_spec = pltpu.PrefetchScalarGridSpec(
        num_scalar_prefetch=1,
        grid=(N_BLOCKS,),
        in_specs=[
            pl.BlockSpec(memory_space=pl.ANY),
            pl.BlockSpec((TB, D_MODEL), lambda b, bu: (b, 0)),
            pl.BlockSpec((None, 1, 6 * D_MODEL),
                         lambda b, bu: (layer * COND_ROWS + _cond_index(b, TB), 0, 0)),
            pl.BlockSpec((TB, SLOTS), lambda b, bu: (b, 0)),
            pl.BlockSpec((1, D_MODEL), lambda b, bu: (0, 0)),
        ],
        out_specs=out_specs,
        scratch_shapes=[
            pltpu.VMEM((COMBINE_BUFS, SLOTS, D_MODEL), _BF16),
            pltpu.SemaphoreType.DMA((COMBINE_BUFS,)),
        ],
    )
    return pl.pallas_call(
        functools.partial(_combine_kernel, split_out),
        grid_spec=grid_spec,
        out_shape=out_shape,
        compiler_params=pltpu.CompilerParams(
            dimension_semantics=("arbitrary",), vmem_limit_bytes=VMEM_LIMIT),
        name="combine_residual",
    )(back_unit, ys, x1, mod3, qw, g_post)


def _rope_tables():
    pos = jnp.arange(DEC_SEQ)
    row = (pos // GRID_W).astype(_F32)
    col = (pos % GRID_W).astype(_F32)
    inv = ROPE_THETA ** (-jnp.arange(ROPE_FREQS, dtype=_F32) / ROPE_FREQS)
    ang_r = row[:, None] * inv[None, :]
    ang_c = col[:, None] * inv[None, :]
    cos = jnp.concatenate([jnp.cos(ang_r)] * 2 + [jnp.cos(ang_c)] * 2, axis=-1)
    sin = jnp.concatenate([-jnp.sin(ang_r), jnp.sin(ang_r), -jnp.sin(ang_c), jnp.sin(ang_c)], axis=-1)
    reps = LANE // HEAD_DIM
    return jnp.tile(cos, (1, reps)), jnp.tile(sin, (1, reps))


def kernel(x_prompt, x_sample, cache_k, cache_v, c, c_ctx, w_ada, b_ada, g_pre_mix, g_post_mix,
           g_pre_ffn, g_post_ffn, w_in, attn_sink, w_attn_o, conv_w, conv_b, conv_ln_g, conv_ln_b,
           w_conv_o, w_out, w_router, b_router, w_gate_up, b_gate_up, w_down, b_down):
    cond =jnp.concatenate([c_ctx[None, :], c, jnp.zeros((COND_ROWS - N_COND, D_MODEL), _F32)], axis=0)
    mod = _modulation(cond, w_ada, b_ada)
    mod3 = mod.reshape(DEPTH * COND_ROWS, 1, 6 * D_MODEL)
    cos_t, sin_t = _rope_tables()
    ck = cache_k.reshape(DEC_BATCH, DEPTH, PAST_LEN, KV_DIM)
    cv = cache_v.reshape(DEC_BATCH, DEPTH, PAST_LEN, KV_DIM)
    w_router_pad = jnp.pad(w_router, ((0, 0), (0, 0), (0, LANE - N_EXPERTS)))
    wr_hi = w_router_pad.astype(_BF16)
    wr_lo = (w_router_pad - wr_hi.astype(_F32)).astype(_BF16)
    b_router_pad = jnp.pad(b_router, ((0, 0), (0, LANE - N_EXPERTS)), constant_values=-jnp.inf)

    kv_all = tuple(jnp.zeros((BATCH, DEPTH, SEQ, KV_DIM), _F32) for _ in range(2))
    x = None
    for l in range(DEPTH):
        row = lambda a: a[l][None, :]
        if l == 0:
            q, kv, u, sg, *kv_all, x = _inproj(
                x_prompt.reshape(N_CTX, D_MODEL), x_sample.reshape(N_LAT, D_MODEL), kv_all, mod3,
                row(g_pre_mix), w_in, cos_t, sin_t, l)
        else:
            q, kv, u, sg, *kv_all = _inproj(x, x, kv_all, mod3, row(g_pre_mix), w_in, cos_t, sin_t, l)
        att = _attention(q, kv, ck, cv, attn_sink[l], l)
        cvn = _conv_branch(u, conv_w[l], row(conv_b), row(conv_ln_g), row(conv_ln_b))
        x1, xp, qw, meta = _mix(
            x, mod3, att, cvn, sg, w_attn_o, w_conv_o, w_out, row(g_post_mix), row(g_pre_ffn),
            wr_hi[l], wr_lo[l], row(b_router_pad), l)
        tile_first, n_tiles, src_unit, back_unit = _plan(meta)
        ys = _experts(tile_first, n_tiles, src_unit, xp, w_gate_up, b_gate_up, w_down, b_down, l)
        outs = _combine(back_unit, ys, x1, mod3, qw, row(g_post_ffn), l, l == DEPTH - 1)
        x = outs[0]

    y_prompt = outs[0].reshape(BATCH, SEQ, D_MODEL)
    y_sample = outs[1].reshape(DEC_BATCH, DEC_SEQ, D_MODEL)
    new_k, new_v = (a.reshape(BATCH, DEPTH, SEQ, N_KV_HEADS, HEAD_DIM) for a in kv_all)
    return (y_prompt, y_sample, new_k, new_v)
```

```python
import functools

import jax
import jax.numpy as jnp
from jax import lax
from jax.experimental import pallas as pl
from jax.experimental.pallas import tpu as pltpu

D_MODEL = 1024
BATCH = 16
SEQ = 256
DEPTH = 2
DEC_BATCH = 2
DEC_SEQ = 2048
PAST_LEN = 256
GRID_W = 64
N_HEADS = 16
N_KV_HEADS = 4
GROUP = N_HEADS // N_KV_HEADS
HEAD_DIM = 64
Q_DIM = N_HEADS * HEAD_DIM
KV_DIM = N_KV_HEADS * HEAD_DIM
WINDOW = 128
ATTN_SCALE = HEAD_DIM ** -0.5
ROPE_THETA = 10000.0
ROPE_HALF = HEAD_DIM // 2
ROPE_FREQS = ROPE_HALF // 2
C_CONV = D_MODEL // 2
CONV_WIDTH = 31
CONV_PAD = (CONV_WIDTH - 1) // 2
N_EXPERTS = 32
TOP_K = 4
D_FF = D_MODEL
SWIGLU_LIMIT = 7.0
SWIGLU_ALPHA = 1.702
EPS = 1e-6
IN_COLS = Q_DIM + 2 * KV_DIM + 2 * C_CONV + 2 * D_MODEL

N_CTX = BATCH * SEQ
N_LAT = DEC_BATCH * DEC_SEQ
N_TOK = N_CTX + N_LAT
N_COND = 1 + DEC_BATCH
COND_ROWS = 8

LANE = 128
SUBLANE = 8
TM = 512
TQ = 256
TQ_SUB = 256
KWIN = TQ_SUB + 2 * WINDOW
assert GROUP == 4 and 2 * HEAD_DIM == LANE
LOG2E = 1.4426950408889634
QK_SCALE = ATTN_SCALE * LOG2E
HALO = 16
CONV_SPAN = SEQ + ((HALO - CONV_PAD + CONV_WIDTH - 1) // SUBLANE) * SUBLANE
assert CONV_SPAN + SUBLANE - 1 <= SEQ + 2 * HALO

TB = 256
N_BLOCKS = N_TOK // TB
MIX_BLOCKS = 2
MIX_ROWS = MIX_BLOCKS * TB
UNIT = 2 * SUBLANE
SLOTS = 1536
BLOCK_UNITS = SLOTS // UNIT
TE = 256
TILE_UNITS = TE // UNIT
N_ETILES = (N_BLOCKS * BLOCK_UNITS) // TILE_UNITS + N_EXPERTS
COMBINE_BUFS = 3
X_BUFS = 3
MAX_LIVE_TILES = (N_BLOCKS * ((TB * TOP_K + N_EXPERTS * (UNIT - 1)) // UNIT)) // TILE_UNITS + N_EXPERTS
assert MAX_LIVE_TILES + X_BUFS - 1 <= N_ETILES
W_CHUNKS = 4
W_ROWS = D_MODEL // W_CHUNKS
assert D_FF == D_MODEL
GATHER_PRIORITY = 1
VMEM_LIMIT = 58 * 1024 * 1024

assert SLOTS >= TB * TOP_K + N_EXPERTS * (UNIT - 1) and SLOTS % UNIT == 0

_F32 = jnp.float32
_BF16 = jnp.bfloat16
_I32 = jnp.int32


def _rms(x, g):
    return x * lax.rsqrt(jnp.mean(x * x, axis=-1, keepdims=True) + EPS) * g


def _cond_index(i, tile):
    n_ctx_tiles = N_CTX // tile
    return jnp.where(i < n_ctx_tiles, 0, 1 + (i - n_ctx_tiles) // (DEC_SEQ // tile))


def _mod_kernel(cond_ref, w_ref, b_ref, out_ref):
    cnd = cond_ref[...]
    s = cnd * jax.nn.sigmoid(cnd)
    out_ref[...] = jnp.dot(s, w_ref[...], precision=lax.Precision.HIGHEST,
                           preferred_element_type=_F32) + b_ref[...]


def _modulation(cond, w_ada, b_ada):
    tn = 1536
    nt = 6 * D_MODEL // tn
    return pl.pallas_call(
        _mod_kernel,
        grid=(DEPTH, nt),
        in_specs=[
            pl.BlockSpec((COND_ROWS, D_MODEL), lambda l, n: (0, 0)),
            pl.BlockSpec((None, D_MODEL, tn), lambda l, n: (l, 0, n)),
            pl.BlockSpec((None, 1, tn), lambda l, n: (l, 0, n)),
        ],
        out_specs=pl.BlockSpec((None, COND_ROWS, tn), lambda l, n: (l, 0, n)),
        out_shape=jax.ShapeDtypeStruct((DEPTH, COND_ROWS, 6 * D_MODEL), _F32),
        compiler_params=pltpu.CompilerParams(
            dimension_semantics=("arbitrary", "arbitrary"), vmem_limit_bytes=VMEM_LIMIT),
        name="modulation",
    )(cond, w_ada, b_ada.reshape(DEPTH, 1, 6 * D_MODEL))


def _rope_chunk(x, cos, sin):
    lane = lax.broadcasted_iota(_I32, x.shape, 1)
    partner = jnp.where((lane & ROPE_FREQS) == 0,
                        pltpu.roll(x, LANE - ROPE_FREQS, 1), pltpu.roll(x, ROPE_FREQS, 1))
    return x * cos + partner * sin


def _inproj_kernel(first, *refs):
    n_in = 9
    xa_ref, xb_ref, mod_ref, g_ref, wf_ref, cos_ref, sin_ref = refs[:7]
    q_ref, kv_ref, u_ref, sg_ref = refs[n_in:n_in + 4]
    new_refs = refs[n_in + 4:n_in + 6]
    rest = refs[n_in + 6:]
    w_ref = rest[-1]
    i = pl.program_id(0)

    @pl.when(i == 0)
    def _():
        w_ref[...] = wf_ref[...].astype(_BF16)

    x = jnp.where(i < N_CTX // TM, xa_ref[...], xb_ref[...])
    if first:
        rest[0][...] = x
    sh = mod_ref[:, 0:D_MODEL]
    sc = mod_ref[:, D_MODEL:2 * D_MODEL]
    h = (_rms(x, g_ref[...]) * (1.0 + sc) + sh).astype(_BF16)

    c0 = 0
    q = jnp.dot(h, w_ref[:, c0:c0 + Q_DIM], preferred_element_type=_F32) * QK_SCALE
    c0 += Q_DIM
    kv = jnp.dot(h, w_ref[:, c0:c0 + 2 * KV_DIM], preferred_element_type=_F32)
    c0 += 2 * KV_DIM
    ua = jnp.dot(h, w_ref[:, c0:c0 + C_CONV], preferred_element_type=_F32)
    c0 += C_CONV
    ub = jnp.dot(h, w_ref[:, c0:c0 + C_CONV], preferred_element_type=_F32)
    c0 += C_CONV
    g = jnp.dot(h, w_ref[:, c0:c0 + 2 * D_MODEL], preferred_element_type=_F32)

    u_ref[...] = ua * jax.nn.sigmoid(ub)
    sg_ref[...] = jax.nn.sigmoid(g).astype(_BF16)

    is_latent = i >= N_CTX // TM

    @pl.when(jnp.logical_not(is_latent))
    def _():
        q_ref[...] = q.astype(_BF16)
        kv_ref[...] = kv
        for which, new_ref in enumerate(new_refs):
            cols = slice(which * KV_DIM, (which + 1) * KV_DIM)
            for r in range(TM // SEQ):
                rows = slice(r * SEQ, (r + 1) * SEQ)
                new_ref[r] = kv[rows, cols]

    @pl.when(is_latent)
    def _():
        cos = cos_ref[...]
        sin = sin_ref[...]
        for j in range(Q_DIM // LANE):
            sl = slice(j * LANE, (j + 1) * LANE)
            q_ref[:, sl] = _rope_chunk(q[:, sl], cos, sin).astype(_BF16)
        for j in range(KV_DIM // LANE):
            sl = slice(j * LANE, (j + 1) * LANE)
            kv_ref[:, sl] = _rope_chunk(kv[:, sl], cos, sin)
        kv_ref[:, KV_DIM:] = kv[:, KV_DIM:]


def _inproj(x_ctx, x_lat, kv_all, mod3, g_pre, w_in, cos_t, sin_t, layer):
    n_ctx_tiles = N_CTX // TM
    lat_tiles = DEC_SEQ // TM
    stack_x = layer == 0
    lat_off = 0 if stack_x else n_ctx_tiles

    def rope_map(i):
        return (jnp.where(i < n_ctx_tiles, 0, (i - n_ctx_tiles) % lat_tiles), 0)

    row_tile = lambda width: pl.BlockSpec((TM, width), lambda i: (i, 0))
    new_kv = pl.BlockSpec((TM // SEQ, None, SEQ, KV_DIM),
                          lambda i: (jnp.minimum(i, n_ctx_tiles - 1), layer, 0, 0))
    new_kv_shape = jax.ShapeDtypeStruct((BATCH, DEPTH, SEQ, KV_DIM), _F32)
    n_fixed_inputs = 7
    out_specs = [row_tile(Q_DIM), row_tile(2 * KV_DIM), row_tile(C_CONV), row_tile(2 * D_MODEL),
                 new_kv, new_kv]
    out_shape = [
        jax.ShapeDtypeStruct((N_TOK, Q_DIM), _BF16),
        jax.ShapeDtypeStruct((N_TOK, 2 * KV_DIM), _F32),
        jax.ShapeDtypeStruct((N_TOK, C_CONV), _F32),
        jax.ShapeDtypeStruct((N_TOK, 2 * D_MODEL), _BF16),
        new_kv_shape, new_kv_shape,
    ]
    if stack_x:
        out_specs.append(row_tile(D_MODEL))
        out_shape.append(jax.ShapeDtypeStruct((N_TOK, D_MODEL), _F32))
    return pl.pallas_call(
        functools.partial(_inproj_kernel, stack_x),
        grid=(N_TOK // TM,),
        in_specs=[
            pl.BlockSpec((TM, D_MODEL), lambda i: (jnp.minimum(i, n_ctx_tiles - 1), 0)),
            pl.BlockSpec((TM, D_MODEL), lambda i: (lat_off + jnp.maximum(i - n_ctx_tiles, 0), 0)),
            pl.BlockSpec((None, 1, 6 * D_MODEL),
                         lambda i: (layer * COND_ROWS + _cond_index(i, TM), 0, 0)),
            pl.BlockSpec((1, D_MODEL), lambda i: (0, 0)),
            pl.BlockSpec((None, D_MODEL, IN_COLS), lambda i: (layer, 0, 0),
                         pipeline_mode=pl.Buffered(1)),
            pl.BlockSpec((TM, LANE), rope_map),
            pl.BlockSpec((TM, LANE), rope_map),
        ] + [pl.BlockSpec(memory_space=pl.ANY)] * len(kv_all),
        out_specs=out_specs,
        out_shape=out_shape,
        input_output_aliases={n_fixed_inputs + j: 4 + j for j in range(len(kv_all))},
        scratch_shapes=[pltpu.VMEM((D_MODEL, IN_COLS), _BF16)],
        compiler_params=pltpu.CompilerParams(
            dimension_semantics=("arbitrary",), vmem_limit_bytes=VMEM_LIMIT),
        name="inproj",
    )(x_ctx, x_lat, mod3, g_pre, w_in, cos_t, sin_t, *kv_all)


def _pair_operands(k, v):
    zero = jnp.zeros_like(k)
    one = jnp.ones_like(v)
    ka = jnp.concatenate([k, zero], axis=1).astype(_BF16)
    kb = jnp.concatenate([zero, k], axis=1).astype(_BF16)
    va = jnp.concatenate([v, zero, one, zero], axis=1).astype(_BF16)
    vb = jnp.concatenate([zero, v, zero, one], axis=1).astype(_BF16)
    return ka, kb, va, vb


def _pair_attend(qq, operands, masks, sink_a, sink_b):
    def scores(which):
        out = []
        for ops, mask in zip(operands, masks):
            s = lax.dot_general(qq, ops[which], (((1,), (1,)), ((), ())),
                                preferred_element_type=_F32)
            out.append(s if mask is None else jnp.where(mask, s, -jnp.inf))
        return out

    acc = jnp.zeros((qq.shape[0], 2 * LANE), _F32)
    sink_terms = []
    for which, sink in ((0, sink_a), (1, sink_b)):
        ss = scores(which)
        m = sink
        for s in ss:
            m = jnp.maximum(m, jnp.max(s, axis=-1, keepdims=True))
        for s, ops in zip(ss, operands):
            acc = acc + jnp.dot(jnp.exp2(s - m).astype(_BF16), ops[2 + which],
                                preferred_element_type=_F32)
        sink_terms.append(jnp.exp2(sink - m))
    lane = lax.broadcasted_iota(_I32, (qq.shape[0], LANE), 1)
    sink_term = jnp.where(lane < HEAD_DIM, sink_terms[0], sink_terms[1])
    return acc[:, :LANE] / (acc[:, LANE:] + sink_term)


def _group_attend(q_ref, o_ref, hk, rows, operands, masks, sink_ref):
    n_rows = rows.stop - rows.start
    pairs = [slice((2 * hk + j) * LANE, (2 * hk + j + 1) * LANE) for j in range(GROUP // 2)]
    qq = jnp.concatenate([q_ref[rows, sl] for sl in pairs], axis=0)
    first = lax.broadcasted_iota(_I32, (qq.shape[0], 1), 0) < n_rows
    sink_a = jnp.where(first, sink_ref[GROUP * hk], sink_ref[GROUP * hk + 2]) * LOG2E
    sink_b = jnp.where(first, sink_ref[GROUP * hk + 1], sink_ref[GROUP * hk + 3]) * LOG2E
    out = _pair_attend(qq, operands, masks, sink_a, sink_b)
    for j, sl in enumerate(pairs):
        o_ref[rows, sl] = out[j * n_rows:(j + 1) * n_rows].astype(_BF16)


def _attn_kernel(sink_ref, q_ref, kv_own_ref, kv_seq_ref, ck_ref, cv_ref, o_ref):
    i = pl.program_id(0)
    n_ctx_steps = N_CTX // TQ

    @pl.when(i < n_ctx_steps)
    def _():
        for hk in range(N_KV_HEADS):
            ks = slice(hk * HEAD_DIM, (hk + 1) * HEAD_DIM)
            vs = slice(KV_DIM + hk * HEAD_DIM, KV_DIM + (hk + 1) * HEAD_DIM)
            own = _pair_operands(kv_own_ref[:, ks], kv_own_ref[:, vs])
            _group_attend(q_ref, o_ref, hk, slice(0, TQ), [own], [None], sink_ref)

    @pl.when(i >= n_ctx_steps)
    def _():
        qb = (i - n_ctx_steps) % (DEC_SEQ // TQ)
        stacked = (GROUP // 2) * TQ_SUB
        starts, masks = [], []
        for sb in range(TQ // TQ_SUB):
            q_start = qb * TQ + sb * TQ_SUB
            k_start = pl.multiple_of(jnp.clip(q_start - WINDOW, 0, DEC_SEQ - KWIN), WINDOW)
            qpos = q_start + lax.broadcasted_iota(_I32, (stacked, KWIN), 0) % TQ_SUB
            kpos = k_start + lax.broadcasted_iota(_I32, (stacked, KWIN), 1)
            starts.append(k_start)
            masks.append(jnp.abs(kpos - qpos) <= WINDOW)
        for hk in range(N_KV_HEADS):
            ks = slice(hk * HEAD_DIM, (hk + 1) * HEAD_DIM)
            vs = slice(KV_DIM + hk * HEAD_DIM, KV_DIM + (hk + 1) * HEAD_DIM)
            cached = _pair_operands(ck_ref[:, ks], cv_ref[:, ks])
            for sb in range(TQ // TQ_SUB):
                local = _pair_operands(kv_seq_ref[pl.ds(starts[sb], KWIN), ks],
                                       kv_seq_ref[pl.ds(starts[sb], KWIN), vs])
                _group_attend(q_ref, o_ref, hk, slice(sb * TQ_SUB, (sb + 1) * TQ_SUB),
                              [local, cached], [masks[sb], None], sink_ref)


def _attention(q, kv, cache_k, cache_v, sink, layer):
    n_ctx_steps = N_CTX // TQ
    nq = DEC_SEQ // TQ
    kv_off = N_CTX // DEC_SEQ

    def lat_batch(i):
        return jnp.maximum(i - n_ctx_steps, 0) // nq

    return pl.pallas_call(
        _attn_kernel,
        grid=(N_TOK // TQ,),
        in_specs=[
            pl.BlockSpec(memory_space=pltpu.SMEM),
            pl.BlockSpec((TQ, Q_DIM), lambda i: (i, 0)),
            pl.BlockSpec((TQ, 2 * KV_DIM), lambda i: (i, 0)),
            pl.BlockSpec((DEC_SEQ, 2 * KV_DIM), lambda i: (kv_off + lat_batch(i), 0)),
            pl.BlockSpec((None, None, PAST_LEN, KV_DIM), lambda i: (lat_batch(i), layer, 0, 0)),
            pl.BlockSpec((None, None, PAST_LEN, KV_DIM), lambda i: (lat_batch(i), layer, 0, 0)),
        ],
        out_specs=pl.BlockSpec((TQ, Q_DIM), lambda i: (i, 0)),
        out_shape=jax.ShapeDtypeStruct((N_TOK, Q_DIM), _BF16),
        compiler_params=pltpu.CompilerParams(
            dimension_semantics=("arbitrary",), vmem_limit_bytes=VMEM_LIMIT),
        name="attention",
    )(sink, q, kv, kv, cache_k, cache_v)


def _conv_kernel(prev_ref, cur_ref, next_ref, w_ref, b_ref, lg_ref, lb_ref, y_ref, pad_ref, sh_ref):
    i = pl.program_id(0)
    n_ctx_tiles = N_CTX // SEQ
    tiles_per_seq = jnp.where(i < n_ctx_tiles, 1, DEC_SEQ // SEQ)
    j = jnp.where(i < n_ctx_tiles, 0, (i - n_ctx_tiles) % (DEC_SEQ // SEQ))
    pad_ref[0:HALO, :] = jnp.where(j > 0, prev_ref[...], 0.0)
    pad_ref[HALO:HALO + SEQ, :] = cur_ref[...]
    pad_ref[HALO + SEQ:HALO + SEQ + HALO, :] = jnp.where(j < tiles_per_seq - 1, next_ref[...], 0.0)

    for r in range(SUBLANE):
        sh_ref[r] = pad_ref[r:r + CONV_SPAN, :]

    rows = 64
    for r0 in range(0, SEQ, rows):
        acc = jnp.zeros((rows, C_CONV), _F32) + b_ref[...]
        for t in range(CONV_WIDTH):
            off = HALO - CONV_PAD + t
            start = (off // SUBLANE) * SUBLANE + r0
            acc = acc + sh_ref[off % SUBLANE, start:start + rows, :] * w_ref[t:t + 1, :]
        mu = jnp.mean(acc, axis=-1, keepdims=True)
        d = acc - mu
        var = jnp.mean(d * d, axis=-1, keepdims=True)
        y = d * lax.rsqrt(var + EPS) * lg_ref[...] + lb_ref[...]
        y_ref[r0:r0 + rows, :] = (y * jax.nn.sigmoid(y)).astype(_BF16)


def _conv_branch(u, conv_w, conv_b, ln_g, ln_b):
    n_tiles = N_TOK // SEQ
    hb = SEQ // HALO
    last = N_TOK // HALO - 1
    return pl.pallas_call(
        _conv_kernel,
        grid=(n_tiles,),
        in_specs=[
            pl.BlockSpec((HALO, C_CONV), lambda i: (jnp.maximum(i * hb - 1, 0), 0)),
            pl.BlockSpec((SEQ, C_CONV), lambda i: (i, 0)),
            pl.BlockSpec((HALO, C_CONV), lambda i: (jnp.minimum((i + 1) * hb, last), 0)),
            pl.BlockSpec((CONV_WIDTH, C_CONV), lambda i: (0, 0)),
            pl.BlockSpec((1, C_CONV), lambda i: (0, 0)),
            pl.BlockSpec((1, C_CONV), lambda i: (0, 0)),
            pl.BlockSpec((1, C_CONV), lambda i: (0, 0)),
        ],
        out_specs=pl.BlockSpec((SEQ, C_CONV), lambda i: (i, 0)),
        out_shape=jax.ShapeDtypeStruct((N_TOK, C_CONV), _BF16),
        scratch_shapes=[pltpu.VMEM((SEQ + 2 * HALO, C_CONV), _F32),
                        pltpu.VMEM((SUBLANE, CONV_SPAN, C_CONV), _F32)],
        compiler_params=pltpu.CompilerParams(
            dimension_semantics=("arbitrary",), vmem_limit_bytes=VMEM_LIMIT),
        name="conv_branch",
    )(u, u, u, conv_w, conv_b, ln_g, ln_b)


def _mix_kernel(x_ref, mod_ref, att_ref, cv_ref, sg_ref, waf_ref, wcf_ref, wof_ref,
                gpost_ref, gffn_ref, wr_ref, wrlo_ref, br_ref,
                x1_ref, xp_ref, qw_ref, meta_ref, wa_ref, wc_ref, wo_ref):
    @pl.when(pl.program_id(0) == 0)
    def _():
        wa_ref[...] = waf_ref[...].astype(_BF16)
        wc_ref[...] = wcf_ref[...].astype(_BF16)
        wo_ref[...] = wof_ref[...].astype(_BF16)

    a = jnp.dot(att_ref[...], wa_ref[...], preferred_element_type=_F32)
    cv = jnp.dot(cv_ref[...], wc_ref[...], preferred_element_type=_F32)
    m = sg_ref[:, 0:D_MODEL].astype(_F32) * a + sg_ref[:, D_MODEL:].astype(_F32) * cv
    mix = jnp.dot(m.astype(_BF16), wo_ref[...], preferred_element_type=_F32)
    gt1 = mod_ref[:, 2 * D_MODEL:3 * D_MODEL]
    sh2 = mod_ref[:, 3 * D_MODEL:4 * D_MODEL]
    sc2 = mod_ref[:, 4 * D_MODEL:5 * D_MODEL]
    x1 = x_ref[...] + gt1 * _rms(mix, gpost_ref[...])
    x1_ref[...] = x1
    h2 = _rms(x1, gffn_ref[...]) * (1.0 + sc2) + sh2
    h2b = h2.astype(_BF16)

    h2_lo = (h2 - h2b.astype(_F32)).astype(_BF16)
    logits = (jnp.dot(h2b, wr_ref[...], preferred_element_type=_F32)
              + jnp.dot(h2_lo, wr_ref[...], preferred_element_type=_F32)
              + jnp.dot(h2b, wrlo_ref[...], preferred_element_type=_F32) + br_ref[...])

    for blk in range(MIX_BLOCKS):
        rows = slice(blk * TB, (blk + 1) * TB)
        qw, xp, meta = _route_block(logits[rows], h2b[rows])
        qw_ref[rows, :] = qw
        xp_ref[blk * SLOTS:(blk + 1) * SLOTS, :] = xp
        meta_ref[blk] = meta


def _route_block(logits, h2b):
    lane = lax.broadcasted_iota(_I32, (TB, LANE), 1).astype(_F32)
    member = jnp.zeros((TB, LANE), _F32)
    hots, exps = [], []
    top = None
    total = jnp.zeros((TB, 1), _F32)
    for k in range(TOP_K):
        mval = jnp.max(logits, axis=-1, keepdims=True)
        sel = jnp.min(jnp.where(logits == mval, lane, float(LANE)), axis=-1, keepdims=True)
        if top is None:
            top = mval
        e = jnp.exp(mval - top)
        total = total + e
        hot = lane == sel
        hots.append(hot)
        exps.append(e)
        member = member + jnp.where(hot, 1.0, 0.0)
        logits = jnp.where(hot, -jnp.inf, logits)

    r_i = lax.broadcasted_iota(_I32, (TB, TB), 0)
    c_i = lax.broadcasted_iota(_I32, (TB, TB), 1)
    lower = jnp.where(r_i > c_i, 1.0, 0.0).astype(_BF16)
    rank = jnp.dot(lower, member.astype(_BF16), preferred_element_type=_F32)
    count = jnp.sum(member, axis=0, keepdims=True)
    units = jnp.floor((count + float(UNIT - 1)) * (1.0 / UNIT))
    r_l = lax.broadcasted_iota(_I32, (LANE, LANE), 0)
    c_l = lax.broadcasted_iota(_I32, (LANE, LANE), 1)
    upper = jnp.where(r_l < c_l, 1.0, 0.0).astype(_BF16)
    unit_off = jnp.dot(jnp.broadcast_to(units, (SUBLANE, LANE)).astype(_BF16), upper,
                       preferred_element_type=_F32)[0:1, :]
    base = unit_off * float(UNIT) + rank

    slot_lane = lax.broadcasted_iota(_I32, (TB, SLOTS), 1).astype(_F32)
    qw = jnp.zeros((TB, SLOTS), _F32)
    hit = jnp.zeros((TB, SLOTS), _F32)
    for k in range(TOP_K):
        slot = jnp.sum(jnp.where(hots[k], base, 0.0), axis=-1, keepdims=True)
        here = slot_lane == slot
        qw = jnp.where(here, exps[k] / total, qw)
        hit = jnp.where(here, 1.0, hit)

    xp = lax.dot_general(hit.astype(_BF16), h2b, (((0,), (0,)), ((), ())),
                         preferred_element_type=_F32).astype(_BF16)

    sub = lax.broadcasted_iota(_I32, (SUBLANE, LANE), 0)
    meta = jnp.where(sub == 0, units, jnp.where(sub == 1, unit_off, 0.0))
    return qw.astype(_BF16), xp, meta.astype(_I32)


def _mix(x, mod3, att, cvn, sg, wa, wc, wo, g_post, g_ffn, wr_hi, wr_lo, b_router_pad, layer):
    full = lambda shape: pl.BlockSpec(shape, lambda i: (0,) * len(shape))
    layer_weight = lambda rows: pl.BlockSpec((None, rows, D_MODEL), lambda i: (layer, 0, 0),
                                             pipeline_mode=pl.Buffered(1))
    return pl.pallas_call(
        _mix_kernel,
        grid=(N_BLOCKS // MIX_BLOCKS,),
        in_specs=[
            pl.BlockSpec((MIX_ROWS, D_MODEL), lambda i: (i, 0)),
            pl.BlockSpec((None, 1, 6 * D_MODEL),
                         lambda i: (layer * COND_ROWS + _cond_index(i, MIX_ROWS), 0, 0)),
            pl.BlockSpec((MIX_ROWS, Q_DIM), lambda i: (i, 0)),
            pl.BlockSpec((MIX_ROWS, C_CONV), lambda i: (i, 0)),
            pl.BlockSpec((MIX_ROWS, 2 * D_MODEL), lambda i: (i, 0)),
            layer_weight(Q_DIM),
            layer_weight(C_CONV),
            layer_weight(D_MODEL),
            full((1, D_MODEL)),
            full((1, D_MODEL)),
            full((D_MODEL, LANE)),
            full((D_MODEL, LANE)),
            full((1, LANE)),
        ],
        out_specs=[
            pl.BlockSpec((MIX_ROWS, D_MODEL), lambda i: (i, 0)),
            pl.BlockSpec((MIX_BLOCKS * SLOTS, D_MODEL), lambda i: (i, 0)),
            pl.BlockSpec((MIX_ROWS, SLOTS), lambda i: (i, 0)),
            pl.BlockSpec((MIX_BLOCKS, SUBLANE, LANE), lambda i: (i, 0, 0)),
        ],
        out_shape=[
            jax.ShapeDtypeStruct((N_TOK, D_MODEL), _F32),
            jax.ShapeDtypeStruct((N_BLOCKS * SLOTS, D_MODEL), _BF16),
            jax.ShapeDtypeStruct((N_TOK, SLOTS), _BF16),
            jax.ShapeDtypeStruct((N_BLOCKS, SUBLANE, LANE), _I32),
        ],
        scratch_shapes=[pltpu.VMEM((Q_DIM, D_MODEL), _BF16), pltpu.VMEM((C_CONV, D_MODEL), _BF16),
                        pltpu.VMEM((D_MODEL, D_MODEL), _BF16)],
        compiler_params=pltpu.CompilerParams(
            dimension_semantics=("arbitrary",), vmem_limit_bytes=VMEM_LIMIT),
        name="mix_router",
    )(x, mod3, att, cvn, sg, wa, wc, wo, g_post, g_ffn, wr_hi, wr_lo, b_router_pad)


def _plan(meta):
    units = meta[:, 0, :N_EXPERTS]
    seg_off = meta[:, 1, :N_EXPERTS]
    tiles = (jnp.sum(units, axis=0) + TILE_UNITS - 1) // TILE_UNITS
    tile_end = jnp.cumsum(tiles)
    n_tiles = tile_end[-1]
    region = (tile_end - tiles) * TILE_UNITS
    dst = region[None, :] + jnp.cumsum(units, axis=0) - units
    src = jnp.arange(N_BLOCKS, dtype=_I32)[:, None] * BLOCK_UNITS + seg_off

    tile_first = jnp.concatenate([jnp.zeros((1,), _I32), tile_end.astype(_I32)])

    dst_f, len_f, src_f = dst.reshape(1, -1), units.reshape(1, -1), src.reshape(1, -1)
    d = jnp.arange(N_ETILES * TILE_UNITS, dtype=_I32)[:, None]
    in_seg = (dst_f <= d) & (d < dst_f + len_f)
    src_unit = jnp.sum(jnp.where(in_seg, src_f + d - dst_f, 0), axis=1)

    u = jnp.arange(BLOCK_UNITS, dtype=_I32)[None, :, None]
    so, un = seg_off[:, None, :], units[:, None, :]
    in_blk = (so <= u) & (u < so + un)
    back_unit = jnp.sum(jnp.where(in_blk, dst[:, None, :] + u - so, 0), axis=2)
    back_unit = jnp.concatenate([back_unit.reshape(-1).astype(_I32),
                                 jnp.zeros(((COMBINE_BUFS - 1) * BLOCK_UNITS,), _I32)])
    return tile_first, n_tiles.reshape(1).astype(_I32), src_unit.astype(_I32), back_unit


def _unit_gather(src_hbm, unit_ref, first, n_units, dst_buf, sem):
    for i in range(n_units):
        row = pl.multiple_of(unit_ref[first + i] * UNIT, UNIT)
        pltpu.make_async_copy(src_hbm.at[pl.ds(row, UNIT), :],
                              dst_buf.at[pl.ds(i * UNIT, UNIT), :], sem).start(priority=GATHER_PRIORITY)


def _unit_gather_wait(src_hbm, n_units, dst_buf, sem):
    pltpu.make_async_copy(src_hbm.at[pl.ds(0, n_units * UNIT), :], dst_buf, sem).wait()


def _tile_write(ybuf_slot, ys_hbm, tile, sem):
    row = pl.multiple_of(tile * TE, TE)
    return pltpu.make_async_copy(ybuf_slot, ys_hbm.at[pl.ds(row, TE), :], sem)


def _weight_chunk(wgu_hbm, wd_hbm, layer, expert, chunk, wgu_f32, wd_f32, buf, sem):
    r = pl.multiple_of(chunk * W_ROWS, W_ROWS)
    return (pltpu.make_async_copy(wgu_hbm.at[layer, expert, pl.ds(r, W_ROWS), :],
                                  wgu_f32.at[buf, pl.ds(r, W_ROWS), :], sem),
            pltpu.make_async_copy(wd_hbm.at[layer, expert, pl.ds(r, W_ROWS), :],
                                  wd_f32.at[buf, pl.ds(r, W_ROWS), :], sem))


def _expert_kernel(layer, first_ref, nt_ref, src_ref, xp_hbm, wgu_hbm, bgu_ref, wd_hbm, bd_ref,
                   ys_hbm, wgu_f32, wd_f32, wgu_bf, wd_bf, xbuf, ybuf, wsem, xsem, ysem):
    e = pl.program_id(0)
    n_live = nt_ref[0]
    t_lo = first_ref[e]
    t_hi = first_ref[e + 1]
    buf = e % 2
    has_next = e + 1 < N_EXPERTS

    def x_gather(tile):
        _unit_gather(xp_hbm, src_ref, tile * TILE_UNITS, TILE_UNITS,
                     xbuf.at[tile % X_BUFS], xsem.at[tile % X_BUFS])

    def start_chunk(expert, chunk, into):
        for cp in _weight_chunk(wgu_hbm, wd_hbm, layer, expert, chunk, wgu_f32, wd_f32,
                                into, wsem.at[into]):
            cp.start()

    @pl.when(e == 0)
    def _():
        for ahead in range(X_BUFS - 1):
            x_gather(ahead)
        for c in range(W_CHUNKS):
            start_chunk(0, c, 0)

    pltpu.make_async_copy(wgu_hbm.at[layer, e], wgu_f32.at[buf], wsem.at[buf]).wait()
    pltpu.make_async_copy(wd_hbm.at[layer, e], wd_f32.at[buf], wsem.at[buf]).wait()

    @pl.when(t_hi > t_lo)
    def _():
        wgu_bf[...] = wgu_f32[buf].astype(_BF16)
        wd_bf[...] = wd_f32[buf].astype(_BF16)

    def tile_body(t, carry):
        slot = t % 2

        x_gather(t + X_BUFS - 1)

        @pl.when(jnp.logical_and(has_next, t - t_lo < W_CHUNKS))
        def _():
            start_chunk(e + 1, t - t_lo, 1 - buf)

        _unit_gather_wait(xp_hbm, TILE_UNITS, xbuf.at[t % X_BUFS], xsem.at[t % X_BUFS])
        gu = jnp.dot(xbuf[t % X_BUFS], wgu_bf[...],
                     preferred_element_type=_F32) + bgu_ref[...]
        gate = jnp.minimum(gu[:, :D_FF], SWIGLU_LIMIT)
        lin = jnp.clip(gu[:, D_FF:], -SWIGLU_LIMIT, SWIGLU_LIMIT)
        act = gate * jax.nn.sigmoid(SWIGLU_ALPHA * gate) * (lin + 1.0)
        y = jnp.dot(act.astype(_BF16), wd_bf[...], preferred_element_type=_F32) + bd_ref[...]

        @pl.when(t >= 2)
        def _():
            _tile_write(ybuf.at[slot], ys_hbm, t - 2, ysem.at[slot]).wait()

        ybuf[slot] = y.astype(_BF16)
        _tile_write(ybuf.at[slot], ys_hbm, t, ysem.at[slot]).start(priority=GATHER_PRIORITY)
        return carry

    lax.fori_loop(t_lo, t_hi, tile_body, 0)

    for c in range(W_CHUNKS):
        @pl.when(jnp.logical_and(has_next, c >= t_hi - t_lo))
        def _():
            start_chunk(e + 1, c, 1 - buf)

    @pl.when(e == N_EXPERTS - 1)
    def _():
        for ahead in range(X_BUFS - 1):
            t = n_live + ahead
            _unit_gather_wait(xp_hbm, TILE_UNITS, xbuf.at[t % X_BUFS], xsem.at[t % X_BUFS])
        for back in (2, 1):
            @pl.when(n_live >= back)
            def _():
                t = n_live - back
                _tile_write(ybuf.at[t % 2], ys_hbm, t, ysem.at[t % 2]).wait()

        ybuf[0] = jnp.zeros((TE, D_MODEL), _BF16)

        def zero_start(t, carry):
            _tile_write(ybuf.at[0], ys_hbm, t, ysem.at[0]).start()
            return carry

        def zero_wait(t, carry):
            _tile_write(ybuf.at[0], ys_hbm, t, ysem.at[0]).wait()
            return carry

        lax.fori_loop(n_live, N_ETILES, zero_start, 0)
        lax.fori_loop(n_live, N_ETILES, zero_wait, 0)


def _experts(tile_first, n_tiles, src_unit, xp, w_gate_up, b_gate_up, w_down, b_down, layer):
    grid_spec = pltpu.PrefetchScalarGridSpec(
        num_scalar_prefetch=3,
        grid=(N_EXPERTS,),
        in_specs=[
            pl.BlockSpec(memory_space=pl.ANY),
            pl.BlockSpec(memory_space=pl.ANY),
            pl.BlockSpec((None, None, 1, 2 * D_FF), lambda e, tf, nt, su: (layer, e, 0, 0)),
            pl.BlockSpec(memory_space=pl.ANY),
            pl.BlockSpec((None, None, 1, D_MODEL), lambda e, tf, nt, su: (layer, e, 0, 0)),
        ],
        out_specs=pl.BlockSpec(memory_space=pl.ANY),
        scratch_shapes=[
            pltpu.VMEM((2, D_MODEL, 2 * D_FF), _F32),
            pltpu.VMEM((2, D_FF, D_MODEL), _F32),
            pltpu.VMEM((D_MODEL, 2 * D_FF), _BF16),
            pltpu.VMEM((D_FF, D_MODEL), _BF16),
            pltpu.VMEM((X_BUFS, TE, D_MODEL), _BF16),
            pltpu.VMEM((2, TE, D_MODEL), _BF16),
            pltpu.SemaphoreType.DMA((2,)),
            pltpu.SemaphoreType.DMA((X_BUFS,)),
            pltpu.SemaphoreType.DMA((2,)),
        ],
    )
    return pl.pallas_call(
        functools.partial(_expert_kernel, layer),
        grid_spec=grid_spec,
        out_shape=jax.ShapeDtypeStruct((N_ETILES * TE, D_MODEL), _BF16),
        compiler_params=pltpu.CompilerParams(
            dimension_semantics=("arbitrary",), vmem_limit_bytes=VMEM_LIMIT),
        name="experts",
    )(tile_first, n_tiles, src_unit, xp, w_gate_up,
      b_gate_up.reshape(DEPTH, N_EXPERTS, 1, 2 * D_FF), w_down,
      b_down.reshape(DEPTH, N_EXPERTS, 1, D_MODEL))


def _combine_kernel(split_out, back_ref, ys_hbm, x_ref, mod_ref, qw_ref, g_ref, *rest):
    outs, ybuf, sem = rest[:-2], rest[-2], rest[-1]
    b = pl.program_id(0)

    def gather(block):
        _unit_gather(ys_hbm, back_ref, block * BLOCK_UNITS, BLOCK_UNITS,
                     ybuf.at[block % COMBINE_BUFS], sem.at[block % COMBINE_BUFS])

    def gather_wait(block):
        _unit_gather_wait(ys_hbm, BLOCK_UNITS, ybuf.at[block % COMBINE_BUFS],
                          sem.at[block % COMBINE_BUFS])

    @pl.when(b == 0)
    def _():
        for ahead in range(COMBINE_BUFS - 1):
            gather(ahead)

    gather(b + COMBINE_BUFS - 1)
    gather_wait(b)
    moe = jnp.dot(qw_ref[...], ybuf[b % COMBINE_BUFS], preferred_element_type=_F32)

    @pl.when(b == N_BLOCKS - 1)
    def _():
        for ahead in range(1, COMBINE_BUFS):
            gather_wait(b + ahead)

    gt2 = mod_ref[:, 5 * D_MODEL:6 * D_MODEL]
    res = x_ref[...] + gt2 * _rms(moe, g_ref[...])
    if split_out:
        @pl.when(b < N_CTX // TB)
        def _():
            outs[0][...] = res

        @pl.when(b >= N_CTX // TB)
        def _():
            outs[1][...] = res
    else:
        outs[0][...] = res


def _combine(back_unit, ys, x1, mod3, qw, g_post, layer, split_out):
    n_ctx_blocks = N_CTX // TB
    if split_out:
        out_specs = [
            pl.BlockSpec((TB, D_MODEL), lambda b, bu: (jnp.minimum(b, n_ctx_blocks - 1), 0)),
            pl.BlockSpec((TB, D_MODEL), lambda b, bu: (jnp.maximum(b - n_ctx_blocks, 0), 0)),
        ]
        out_shape = [jax.ShapeDtypeStruct((N_CTX, D_MODEL), _F32),
                     jax.ShapeDtypeStruct((N_LAT, D_MODEL), _F32)]
    else:
        out_specs = [pl.BlockSpec((TB, D_MODEL), lambda b, bu: (b, 0))]
        out_shape = [jax.ShapeDtypeStruct((N_TOK, D_MODEL), _F32)]
    grid_spec = pltpu.PrefetchScalarGridSpec(
        num_scalar_prefetch=1,
        grid=(N_BLOCKS,),
        in_specs=[
            pl.BlockSpec(memory_space=pl.ANY),
            pl.BlockSpec((TB, D_MODEL), lambda b, bu: (b, 0)),
            pl.BlockSpec((None, 1, 6 * D_MODEL),
                         lambda b, bu: (layer * COND_ROWS + _cond_index(b, TB), 0, 0)),
            pl.BlockSpec((TB, SLOTS), lambda b, bu: (b, 0)),
            pl.BlockSpec((1, D_MODEL), lambda b, bu: (0, 0)),
        ],
        out_specs=out_specs,
        scratch_shapes=[
            pltpu.VMEM((COMBINE_BUFS, SLOTS, D_MODEL), _BF16),
            pltpu.SemaphoreType.DMA((COMBINE_BUFS,)),
        ],
    )
    return pl.pallas_call(
        functools.partial(_combine_kernel, split_out),
        grid_spec=grid_spec,
        out_shape=out_shape,
        compiler_params=pltpu.CompilerParams(
            dimension_semantics=("arbitrary",), vmem_limit_bytes=VMEM_LIMIT),
        name="combine_residual",
    )(back_unit, ys, x1, mod3, qw, g_post)


def _rope_tables():
    pos = jnp.arange(DEC_SEQ)
    row = (pos // GRID_W).astype(_F32)
    col = (pos % GRID_W).astype(_F32)
    inv = ROPE_THETA ** (-jnp.arange(ROPE_FREQS, dtype=_F32) / ROPE_FREQS)
    ang_r = row[:, None] * inv[None, :]
    ang_c = col[:, None] * inv[None, :]
    cos = jnp.concatenate([jnp.cos(ang_r)] * 2 + [jnp.cos(ang_c)] * 2, axis=-1)
    sin = jnp.concatenate([-jnp.sin(ang_r), jnp.sin(ang_r), -jnp.sin(ang_c), jnp.sin(ang_c)], axis=-1)
    reps = LANE // HEAD_DIM
    return jnp.tile(cos, (1, reps)), jnp.tile(sin, (1, reps))


def kernel(x_prompt, x_sample, cache_k, cache_v, c, c_ctx, w_ada, b_ada, g_pre_mix, g_post_mix,
           g_pre_ffn, g_post_ffn, w_in, attn_sink, w_attn_o, conv_w, conv_b, conv_ln_g, conv_ln_b,
           w_conv_o, w_out, w_router, b_router, w_gate_up, b_gate_up, w_down, b_down):
    cond =jnp.concatenate([c_ctx[None, :], c, jnp.zeros((COND_ROWS - N_COND, D_MODEL), _F32)], axis=0)
    mod = _modulation(cond, w_ada, b_ada)
    mod3 = mod.reshape(DEPTH * COND_ROWS, 1, 6 * D_MODEL)
    cos_t, sin_t = _rope_tables()
    ck = cache_k.reshape(DEC_BATCH, DEPTH, PAST_LEN, KV_DIM)
    cv = cache_v.reshape(DEC_BATCH, DEPTH, PAST_LEN, KV_DIM)
    w_router_pad = jnp.pad(w_router, ((0, 0), (0, 0), (0, LANE - N_EXPERTS)))
    wr_hi = w_router_pad.astype(_BF16)
    wr_lo = (w_router_pad - wr_hi.astype(_F32)).astype(_BF16)
    b_router_pad = jnp.pad(b_router, ((0, 0), (0, LANE - N_EXPERTS)), constant_values=-jnp.inf)

    kv_all = tuple(jnp.zeros((BATCH, DEPTH, SEQ, KV_DIM), _F32) for _ in range(2))
    x = None
    for l in range(DEPTH):
        row = lambda a: a[l][None, :]
        if l == 0:
            q, kv, u, sg, *kv_all, x = _inproj(
                x_prompt.reshape(N_CTX, D_MODEL), x_sample.reshape(N_LAT, D_MODEL), kv_all, mod3,
                row(g_pre_mix), w_in, cos_t, sin_t, l)
        else:
            q, kv, u, sg, *kv_all = _inproj(x, x, kv_all, mod3, row(g_pre_mix), w_in, cos_t, sin_t, l)
        att = _attention(q, kv, ck, cv, attn_sink[l], l)
        cvn = _conv_branch(u, conv_w[l], row(conv_b), row(conv_ln_g), row(conv_ln_b))
        x1, xp, qw, meta = _mix(
            x, mod3, att, cvn, sg, w_attn_o, w_conv_o, w_out, row(g_post_mix), row(g_pre_ffn),
            wr_hi[l], wr_lo[l], row(b_router_pad), l)
        tile_first, n_tiles, src_unit, back_unit = _plan(meta)
        ys = _experts(tile_first, n_tiles, src_unit, xp, w_gate_up, b_gate_up, w_down, b_down, l)
        outs = _combine(back_unit, ys, x1, mod3, qw, row(g_post_ffn), l, l == DEPTH - 1)
        x = outs[0]

    y_prompt = outs[0].reshape(BATCH, SEQ, D_MODEL)
    y_sample = outs[1].reshape(DEC_BATCH, DEC_SEQ, D_MODEL)
    new_k, new_v = (a.reshape(BATCH, DEPTH, SEQ, N_KV_HEADS, HEAD_DIM) for a in kv_all)
    return (y_prompt, y_sample, new_k, new_v)
```

```python
import functools

import jax
import jax.numpy as jnp
from jax import lax
from jax.experimental import pallas as pl
from jax.experimental.pallas import tpu as pltpu

D_MODEL = 1024
BATCH = 16
SEQ = 256
DEPTH = 2
DEC_BATCH = 2
DEC_SEQ = 2048
PAST_LEN = 256
GRID_W = 64
N_HEADS = 16
N_KV_HEADS = 4
GROUP = N_HEADS // N_KV_HEADS
HEAD_DIM = 64
Q_DIM = N_HEADS * HEAD_DIM
KV_DIM = N_KV_HEADS * HEAD_DIM
WINDOW = 128
ATTN_SCALE = HEAD_DIM ** -0.5
ROPE_THETA = 10000.0
ROPE_HALF = HEAD_DIM // 2
ROPE_FREQS = ROPE_HALF // 2
C_CONV = D_MODEL // 2
CONV_WIDTH = 31
CONV_PAD = (CONV_WIDTH - 1) // 2
N_EXPERTS = 32
TOP_K = 4
D_FF = D_MODEL
SWIGLU_LIMIT = 7.0
SWIGLU_ALPHA = 1.702
EPS = 1e-6
IN_COLS = Q_DIM + 2 * KV_DIM + 2 * C_CONV + 2 * D_MODEL

N_CTX = BATCH * SEQ
N_LAT = DEC_BATCH * DEC_SEQ
N_TOK = N_CTX + N_LAT
N_COND = 1 + DEC_BATCH
COND_ROWS = 8

LANE = 128
SUBLANE = 8
TM = 512
TQ = 256
TQ_SUB = 256
KWIN = TQ_SUB + 2 * WINDOW
assert GROUP == 4 and 2 * HEAD_DIM == LANE
LOG2E = 1.4426950408889634
QK_SCALE = ATTN_SCALE * LOG2E
HALO = 16
CONV_SPAN = SEQ + ((HALO - CONV_PAD + CONV_WIDTH - 1) // SUBLANE) * SUBLANE
assert CONV_SPAN + SUBLANE - 1 <= SEQ + 2 * HALO

TB = 256
N_BLOCKS = N_TOK // TB
MIX_BLOCKS = 2
MIX_ROWS = MIX_BLOCKS * TB
UNIT = 2 * SUBLANE
SLOTS = 1536
BLOCK_UNITS = SLOTS // UNIT
TE = 256
TILE_UNITS = TE // UNIT
N_ETILES = (N_BLOCKS * BLOCK_UNITS) // TILE_UNITS + N_EXPERTS
COMBINE_BUFS = 3
X_BUFS = 3
MAX_LIVE_TILES = (N_BLOCKS * ((TB * TOP_K + N_EXPERTS * (UNIT - 1)) // UNIT)) // TILE_UNITS + N_EXPERTS
assert MAX_LIVE_TILES + X_BUFS - 1 <= N_ETILES
W_CHUNKS = 4
W_ROWS = D_MODEL // W_CHUNKS
assert D_FF == D_MODEL
GATHER_PRIORITY = 1
VMEM_LIMIT = 58 * 1024 * 1024

assert SLOTS >= TB * TOP_K + N_EXPERTS * (UNIT - 1) and SLOTS % UNIT == 0

_F32 = jnp.float32
_BF16 = jnp.bfloat16
_I32 = jnp.int32


def _rms(x, g):
    return x * lax.rsqrt(jnp.mean(x * x, axis=-1, keepdims=True) + EPS) * g


def _cond_index(i, tile):
    n_ctx_tiles = N_CTX // tile
    return jnp.where(i < n_ctx_tiles, 0, 1 + (i - n_ctx_tiles) // (DEC_SEQ // tile))


def _mod_kernel(cond_ref, w_ref, b_ref, out_ref):
    cnd = cond_ref[...]
    s = cnd * jax.nn.sigmoid(cnd)
    out_ref[...] = jnp.dot(s, w_ref[...], precision=lax.Precision.HIGHEST,
                           preferred_element_type=_F32) + b_ref[...]


def _modulation(cond, w_ada, b_ada):
    tn = 1536
    nt = 6 * D_MODEL // tn
    return pl.pallas_call(
        _mod_kernel,
        grid=(DEPTH, nt),
        in_specs=[
            pl.BlockSpec((COND_ROWS, D_MODEL), lambda l, n: (0, 0)),
            pl.BlockSpec((None, D_MODEL, tn), lambda l, n: (l, 0, n)),
            pl.BlockSpec((None, 1, tn), lambda l, n: (l, 0, n)),
        ],
        out_specs=pl.BlockSpec((None, COND_ROWS, tn), lambda l, n: (l, 0, n)),
        out_shape=jax.ShapeDtypeStruct((DEPTH, COND_ROWS, 6 * D_MODEL), _F32),
        compiler_params=pltpu.CompilerParams(
            dimension_semantics=("arbitrary", "arbitrary"), vmem_limit_bytes=VMEM_LIMIT),
        name="modulation",
    )(cond, w_ada, b_ada.reshape(DEPTH, 1, 6 * D_MODEL))


def _rope_chunk(x, cos, sin):
    lane = lax.broadcasted_iota(_I32, x.shape, 1)
    partner = jnp.where((lane & ROPE_FREQS) == 0,
                        pltpu.roll(x, LANE - ROPE_FREQS, 1), pltpu.roll(x, ROPE_FREQS, 1))
    return x * cos + partner * sin


def _inproj_kernel(first, *refs):
    n_in = 9
    xa_ref, xb_ref, mod_ref, g_ref, wf_ref, cos_ref, sin_ref = refs[:7]
    q_ref, kv_ref, u_ref, sg_ref = refs[n_in:n_in + 4]
    new_refs = refs[n_in + 4:n_in + 6]
    rest = refs[n_in + 6:]
    w_ref = rest[-1]
    i = pl.program_id(0)

    @pl.when(i == 0)
    def _():
        w_ref[...] = wf_ref[...].astype(_BF16)

    x = jnp.where(i < N_CTX // TM, xa_ref[...], xb_ref[...])
    if first:
        rest[0][...] = x
    sh = mod_ref[:, 0:D_MODEL]
    sc = mod_ref[:, D_MODEL:2 * D_MODEL]
    h = (_rms(x, g_ref[...]) * (1.0 + sc) + sh).astype(_BF16)

    c0 = 0
    q = jnp.dot(h, w_ref[:, c0:c0 + Q_DIM], preferred_element_type=_F32) * QK_SCALE
    c0 += Q_DIM
    kv = jnp.dot(h, w_ref[:, c0:c0 + 2 * KV_DIM], preferred_element_type=_F32)
    c0 += 2 * KV_DIM
    ua = jnp.dot(h, w_ref[:, c0:c0 + C_CONV], preferred_element_type=_F32)
    c0 += C_CONV
    ub = jnp.dot(h, w_ref[:, c0:c0 + C_CONV], preferred_element_type=_F32)
    c0 += C_CONV
    g = jnp.dot(h, w_ref[:, c0:c0 + 2 * D_MODEL], preferred_element_type=_F32)

    u_ref[...] = ua * jax.nn.sigmoid(ub)
    sg_ref[...] = jax.nn.sigmoid(g).astype(_BF16)

    is_latent = i >= N_CTX // TM

    @pl.when(jnp.logical_not(is_latent))
    def _():
        q_ref[...] = q.astype(_BF16)
        kv_ref[...] = kv
        for which, new_ref in enumerate(new_refs):
            cols = slice(which * KV_DIM, (which + 1) * KV_DIM)
            for r in range(TM // SEQ):
                rows = slice(r * SEQ, (r + 1) * SEQ)
                new_ref[r] = kv[rows, cols]

    @pl.when(is_latent)
    def _():
        cos = cos_ref[...]
        sin = sin_ref[...]
        for j in range(Q_DIM // LANE):
            sl = slice(j * LANE, (j + 1) * LANE)
            q_ref[:, sl] = _rope_chunk(q[:, sl], cos, sin).astype(_BF16)
        for j in range(KV_DIM // LANE):
            sl = slice(j * LANE, (j + 1) * LANE)
            kv_ref[:, sl] = _rope_chunk(kv[:, sl], cos, sin)
        kv_ref[:, KV_DIM:] = kv[:, KV_DIM:]


def _inproj(x_ctx, x_lat, kv_all, mod3, g_pre, w_in, cos_t, sin_t, layer):
    n_ctx_tiles = N_CTX // TM
    lat_tiles = DEC_SEQ // TM
    stack_x = layer == 0
    lat_off = 0 if stack_x else n_ctx_tiles

    def rope_map(i):
        return (jnp.where(i < n_ctx_tiles, 0, (i - n_ctx_tiles) % lat_tiles), 0)

    row_tile = lambda width: pl.BlockSpec((TM, width), lambda i: (i, 0))
    new_kv = pl.BlockSpec((TM // SEQ, None, SEQ, KV_DIM),
                          lambda i: (jnp.minimum(i, n_ctx_tiles - 1), layer, 0, 0))
    new_kv_shape = jax.ShapeDtypeStruct((BATCH, DEPTH, SEQ, KV_DIM), _F32)
    n_fixed_inputs = 7
    out_specs = [row_tile(Q_DIM), row_tile(2 * KV_DIM), row_tile(C_CONV), row_tile(2 * D_MODEL),
                 new_kv, new_kv]
    out_shape = [
        jax.ShapeDtypeStruct((N_TOK, Q_DIM), _BF16),
        jax.ShapeDtypeStruct((N_TOK, 2 * KV_DIM), _F32),
        jax.ShapeDtypeStruct((N_TOK, C_CONV), _F32),
        jax.ShapeDtypeStruct((N_TOK, 2 * D_MODEL), _BF16),
        new_kv_shape, new_kv_shape,
    ]
    if stack_x:
        out_specs.append(row_tile(D_MODEL))
        out_shape.append(jax.ShapeDtypeStruct((N_TOK, D_MODEL), _F32))
    return pl.pallas_call(
        functools.partial(_inproj_kernel, stack_x),
        grid=(N_TOK // TM,),
        in_specs=[
            pl.BlockSpec((TM, D_MODEL), lambda i: (jnp.minimum(i, n_ctx_tiles - 1), 0)),
            pl.BlockSpec((TM, D_MODEL), lambda i: (lat_off + jnp.maximum(i - n_ctx_tiles, 0), 0)),
            pl.BlockSpec((None, 1, 6 * D_MODEL),
                         lambda i: (layer * COND_ROWS + _cond_index(i, TM), 0, 0)),
            pl.BlockSpec((1, D_MODEL), lambda i: (0, 0)),
            pl.BlockSpec((None, D_MODEL, IN_COLS), lambda i: (layer, 0, 0),
                         pipeline_mode=pl.Buffered(1)),
            pl.BlockSpec((TM, LANE), rope_map),
            pl.BlockSpec((TM, LANE), rope_map),
        ] + [pl.BlockSpec(memory_space=pl.ANY)] * len(kv_all),
        out_specs=out_specs,
        out_shape=out_shape,
        input_output_aliases={n_fixed_inputs + j: 4 + j for j in range(len(kv_all))},
        scratch_shapes=[pltpu.VMEM((D_MODEL, IN_COLS), _BF16)],
        compiler_params=pltpu.CompilerParams(
            dimension_semantics=("arbitrary",), vmem_limit_bytes=VMEM_LIMIT),
        name="inproj",
    )(x_ctx, x_lat, mod3, g_pre, w_in, cos_t, sin_t, *kv_all)


def _pair_operands(k, v):
    zero = jnp.zeros_like(k)
    one = jnp.ones_like(v)
    ka = jnp.concatenate([k, zero], axis=1).astype(_BF16)
    kb = jnp.concatenate([zero, k], axis=1).astype(_BF16)
    va = jnp.concatenate([v, zero, one, zero], axis=1).astype(_BF16)
    vb = jnp.concatenate([zero, v, zero, one], axis=1).astype(_BF16)
    return ka, kb, va, vb


def _pair_attend(qq, operands, masks, sink_a, sink_b):
    def scores(which):
        out = []
        for ops, mask in zip(operands, masks):
            s = lax.dot_general(qq, ops[which], (((1,), (1,)), ((), ())),
                                preferred_element_type=_F32)
            out.append(s if mask is None else jnp.where(mask, s, -jnp.inf))
        return out

    acc = jnp.zeros((qq.shape[0], 2 * LANE), _F32)
    sink_terms = []
    for which, sink in ((0, sink_a), (1, sink_b)):
        ss = scores(which)
        m = sink
        for s in ss:
            m = jnp.maximum(m, jnp.max(s, axis=-1, keepdims=True))
        for s, ops in zip(ss, operands):
            acc = acc + jnp.dot(jnp.exp2(s - m).astype(_BF16), ops[2 + which],
                                preferred_element_type=_F32)
        sink_terms.append(jnp.exp2(sink - m))
    lane = lax.broadcasted_iota(_I32, (qq.shape[0], LANE), 1)
    sink_term = jnp.where(lane < HEAD_DIM, sink_terms[0], sink_terms[1])
    return acc[:, :LANE] / (acc[:, LANE:] + sink_term)


def _group_attend(q_ref, o_ref, hk, rows, operands, masks, sink_ref):
    n_rows = rows.stop - rows.start
    pairs = [slice((2 * hk + j) * LANE, (2 * hk + j + 1) * LANE) for j in range(GROUP // 2)]
    qq = jnp.concatenate([q_ref[rows, sl] for sl in pairs], axis=0)
    first = lax.broadcasted_iota(_I32, (qq.shape[0], 1), 0) < n_rows
    sink_a = jnp.where(first, sink_ref[GROUP * hk], sink_ref[GROUP * hk + 2]) * LOG2E
    sink_b = jnp.where(first, sink_ref[GROUP * hk + 1], sink_ref[GROUP * hk + 3]) * LOG2E
    out = _pair_attend(qq, operands, masks, sink_a, sink_b)
    for j, sl in enumerate(pairs):
        o_ref[rows, sl] = out[j * n_rows:(j + 1) * n_rows].astype(_BF16)


def _attn_kernel(sink_ref, q_ref, kv_own_ref, kv_seq_ref, ck_ref, cv_ref, o_ref):
    i = pl.program_id(0)
    n_ctx_steps = N_CTX // TQ

    @pl.when(i < n_ctx_steps)
    def _():
        for hk in range(N_KV_HEADS):
            ks = slice(hk * HEAD_DIM, (hk + 1) * HEAD_DIM)
            vs = slice(KV_DIM + hk * HEAD_DIM, KV_DIM + (hk + 1) * HEAD_DIM)
            own = _pair_operands(kv_own_ref[:, ks], kv_own_ref[:, vs])
            _group_attend(q_ref, o_ref, hk, slice(0, TQ), [own], [None], sink_ref)

    @pl.when(i >= n_ctx_steps)
    def _():
        qb = (i - n_ctx_steps) % (DEC_SEQ // TQ)
        stacked = (GROUP // 2) * TQ_SUB
        starts, masks = [], []
        for sb in range(TQ // TQ_SUB):
            q_start = qb * TQ + sb * TQ_SUB
            k_start = pl.multiple_of(jnp.clip(q_start - WINDOW, 0, DEC_SEQ - KWIN), WINDOW)
            qpos = q_start + lax.broadcasted_iota(_I32, (stacked, KWIN), 0) % TQ_SUB
            kpos = k_start + lax.broadcasted_iota(_I32, (stacked, KWIN), 1)
            starts.append(k_start)
            masks.append(jnp.abs(kpos - qpos) <= WINDOW)
        for hk in range(N_KV_HEADS):
            ks = slice(hk * HEAD_DIM, (hk + 1) * HEAD_DIM)
            vs = slice(KV_DIM + hk * HEAD_DIM, KV_DIM + (hk + 1) * HEAD_DIM)
            cached = _pair_operands(ck_ref[:, ks], cv_ref[:, ks])
            for sb in range(TQ // TQ_SUB):
                local = _pair_operands(kv_seq_ref[pl.ds(starts[sb], KWIN), ks],
                                       kv_seq_ref[pl.ds(starts[sb], KWIN), vs])
                _group_attend(q_ref, o_ref, hk, slice(sb * TQ_SUB, (sb + 1) * TQ_SUB),
                              [local, cached], [masks[sb], None], sink_ref)


def _attention(q, kv, cache_k, cache_v, sink, layer):
    n_ctx_steps = N_CTX // TQ
    nq = DEC_SEQ // TQ
    kv_off = N_CTX // DEC_SEQ

    def lat_batch(i):
        return jnp.maximum(i - n_ctx_steps, 0) // nq

    return pl.pallas_call(
        _attn_kernel,
        grid=(N_TOK // TQ,),
        in_specs=[
            pl.BlockSpec(memory_space=pltpu.SMEM),
            pl.BlockSpec((TQ, Q_DIM), lambda i: (i, 0)),
            pl.BlockSpec((TQ, 2 * KV_DIM), lambda i: (i, 0)),
            pl.BlockSpec((DEC_SEQ, 2 * KV_DIM), lambda i: (kv_off + lat_batch(i), 0)),
            pl.BlockSpec((None, None, PAST_LEN, KV_DIM), lambda i: (lat_batch(i), layer, 0, 0)),
            pl.BlockSpec((None, None, PAST_LEN, KV_DIM), lambda i: (lat_batch(i), layer, 0, 0)),
        ],
        out_specs=pl.BlockSpec((TQ, Q_DIM), lambda i: (i, 0)),
        out_shape=jax.ShapeDtypeStruct((N_TOK, Q_DIM), _BF16),
        compiler_params=pltpu.CompilerParams(
            dimension_semantics=("arbitrary",), vmem_limit_bytes=VMEM_LIMIT),
        name="attention",
    )(sink, q, kv, kv, cache_k, cache_v)


def _conv_kernel(prev_ref, cur_ref, next_ref, w_ref, b_ref, lg_ref, lb_ref, y_ref, pad_ref, sh_ref):
    i = pl.program_id(0)
    n_ctx_tiles = N_CTX // SEQ
    tiles_per_seq = jnp.where(i < n_ctx_tiles, 1, DEC_SEQ // SEQ)
    j = jnp.where(i < n_ctx_tiles, 0, (i - n_ctx_tiles) % (DEC_SEQ // SEQ))
    pad_ref[0:HALO, :] = jnp.where(j > 0, prev_ref[...], 0.0)
    pad_ref[HALO:HALO + SEQ, :] = cur_ref[...]
    pad_ref[HALO + SEQ:HALO + SEQ + HALO, :] = jnp.where(j < tiles_per_seq - 1, next_ref[...], 0.0)

    for r in range(SUBLANE):
        sh_ref[r] = pad_ref[r:r + CONV_SPAN, :]

    rows = 64
    for r0 in range(0, SEQ, rows):
        acc = jnp.zeros((rows, C_CONV), _F32) + b_ref[...]
        for t in range(CONV_WIDTH):
            off = HALO - CONV_PAD + t
            start = (off // SUBLANE) * SUBLANE + r0
            acc = acc + sh_ref[off % SUBLANE, start:start + rows, :] * w_ref[t:t + 1, :]
        mu = jnp.mean(acc, axis=-1, keepdims=True)
        d = acc - mu
        var = jnp.mean(d * d, axis=-1, keepdims=True)
        y = d * lax.rsqrt(var + EPS) * lg_ref[...] + lb_ref[...]
        y_ref[r0:r0 + rows, :] = (y * jax.nn.sigmoid(y)).astype(_BF16)


def _conv_branch(u, conv_w, conv_b, ln_g, ln_b):
    n_tiles = N_TOK // SEQ
    hb = SEQ // HALO
    last = N_TOK // HALO - 1
    return pl.pallas_call(
        _conv_kernel,
        grid=(n_tiles,),
        in_specs=[
            pl.BlockSpec((HALO, C_CONV), lambda i: (jnp.maximum(i * hb - 1, 0), 0)),
            pl.BlockSpec((SEQ, C_CONV), lambda i: (i, 0)),
            pl.BlockSpec((HALO, C_CONV), lambda i: (jnp.minimum((i + 1) * hb, last), 0)),
            pl.BlockSpec((CONV_WIDTH, C_CONV), lambda i: (0, 0)),
            pl.BlockSpec((1, C_CONV), lambda i: (0, 0)),
            pl.BlockSpec((1, C_CONV), lambda i: (0, 0)),
            pl.BlockSpec((1, C_CONV), lambda i: (0, 0)),
        ],
        out_specs=pl.BlockSpec((SEQ, C_CONV), lambda i: (i, 0)),
        out_shape=jax.ShapeDtypeStruct((N_TOK, C_CONV), _BF16),
        scratch_shapes=[pltpu.VMEM((SEQ + 2 * HALO, C_CONV), _F32),
                        pltpu.VMEM((SUBLANE, CONV_SPAN, C_CONV), _F32)],
        compiler_params=pltpu.CompilerParams(
            dimension_semantics=("arbitrary",), vmem_limit_bytes=VMEM_LIMIT),
        name="conv_branch",
    )(u, u, u, conv_w, conv_b, ln_g, ln_b)


def _mix_kernel(x_ref, mod_ref, att_ref, cv_ref, sg_ref, waf_ref, wcf_ref, wof_ref,
                gpost_ref, gffn_ref, wr_ref, wrlo_ref, br_ref,
                x1_ref, xp_ref, qw_ref, meta_ref, wa_ref, wc_ref, wo_ref):
    @pl.when(pl.program_id(0) == 0)
    def _():
        wa_ref[...] = waf_ref[...].astype(_BF16)
        wc_ref[...] = wcf_ref[...].astype(_BF16)
        wo_ref[...] = wof_ref[...].astype(_BF16)

    a = jnp.dot(att_ref[...], wa_ref[...], preferred_element_type=_F32)
    cv = jnp.dot(cv_ref[...], wc_ref[...], preferred_element_type=_F32)
    m = sg_ref[:, 0:D_MODEL].astype(_F32) * a + sg_ref[:, D_MODEL:].astype(_F32) * cv
    mix = jnp.dot(m.astype(_BF16), wo_ref[...], preferred_element_type=_F32)
    gt1 = mod_ref[:, 2 * D_MODEL:3 * D_MODEL]
    sh2 = mod_ref[:, 3 * D_MODEL:4 * D_MODEL]
    sc2 = mod_ref[:, 4 * D_MODEL:5 * D_MODEL]
    x1 = x_ref[...] + gt1 * _rms(mix, gpost_ref[...])
    x1_ref[...] = x1
    h2 = _rms(x1, gffn_ref[...]) * (1.0 + sc2) + sh2
    h2b = h2.astype(_BF16)

    h2_lo = (h2 - h2b.astype(_F32)).astype(_BF16)
    logits = (jnp.dot(h2b, wr_ref[...], preferred_element_type=_F32)
              + jnp.dot(h2_lo, wr_ref[...], preferred_element_type=_F32)
              + jnp.dot(h2b, wrlo_ref[...], preferred_element_type=_F32) + br_ref[...])

    for blk in range(MIX_BLOCKS):
        rows = slice(blk * TB, (blk + 1) * TB)
        qw, xp, meta = _route_block(logits[rows], h2b[rows])
        qw_ref[rows, :] = qw
        xp_ref[blk * SLOTS:(blk + 1) * SLOTS, :] = xp
        meta_ref[blk] = meta


def _route_block(logits, h2b):
    lane = lax.broadcasted_iota(_I32, (TB, LANE), 1).astype(_F32)
    member = jnp.zeros((TB, LANE), _F32)
    hots, exps = [], []
    top = None
    total = jnp.zeros((TB, 1), _F32)
    for k in range(TOP_K):
        mval = jnp.max(logits, axis=-1, keepdims=True)
        sel = jnp.min(jnp.where(logits == mval, lane, float(LANE)), axis=-1, keepdims=True)
        if top is None:
            top = mval
        e = jnp.exp(mval - top)
        total = total + e
        hot = lane == sel
        hots.append(hot)
        exps.append(e)
        member = member + jnp.where(hot, 1.0, 0.0)
        logits = jnp.where(hot, -jnp.inf, logits)

    r_i = lax.broadcasted_iota(_I32, (TB, TB), 0)
    c_i = lax.broadcasted_iota(_I32, (TB, TB), 1)
    lower = jnp.where(r_i > c_i, 1.0, 0.0).astype(_BF16)
    rank = jnp.dot(lower, member.astype(_BF16), preferred_element_type=_F32)
    count = jnp.sum(member, axis=0, keepdims=True)
    units = jnp.floor((count + float(UNIT - 1)) * (1.0 / UNIT))
    r_l = lax.broadcasted_iota(_I32, (LANE, LANE), 0)
    c_l = lax.broadcasted_iota(_I32, (LANE, LANE), 1)
    upper = jnp.where(r_l < c_l, 1.0, 0.0).astype(_BF16)
    unit_off = jnp.dot(jnp.broadcast_to(units, (SUBLANE, LANE)).astype(_BF16), upper,
                       preferred_element_type=_F32)[0:1, :]
    base = unit_off * float(UNIT) + rank

    slot_lane = lax.broadcasted_iota(_I32, (TB, SLOTS), 1).astype(_F32)
    qw = jnp.zeros((TB, SLOTS), _F32)
    hit = jnp.zeros((TB, SLOTS), _F32)
    for k in range(TOP_K):
        slot = jnp.sum(jnp.where(hots[k], base, 0.0), axis=-1, keepdims=True)
        here = slot_lane == slot
        qw = jnp.where(here, exps[k] / total, qw)
        hit = jnp.where(here, 1.0, hit)

    xp = lax.dot_general(hit.astype(_BF16), h2b, (((0,), (0,)), ((), ())),
                         preferred_element_type=_F32).astype(_BF16)

    sub = lax.broadcasted_iota(_I32, (SUBLANE, LANE), 0)
    meta = jnp.where(sub == 0, units, jnp.where(sub == 1, unit_off, 0.0))
    return qw.astype(_BF16), xp, meta.astype(_I32)


def _mix(x, mod3, att, cvn, sg, wa, wc, wo, g_post, g_ffn, wr_hi, wr_lo, b_router_pad, layer):
    full = lambda shape: pl.BlockSpec(shape, lambda i: (0,) * len(shape))
    layer_weight = lambda rows: pl.BlockSpec((None, rows, D_MODEL), lambda i: (layer, 0, 0),
                                             pipeline_mode=pl.Buffered(1))
    return pl.pallas_call(
        _mix_kernel,
        grid=(N_BLOCKS // MIX_BLOCKS,),
        in_specs=[
            pl.BlockSpec((MIX_ROWS, D_MODEL), lambda i: (i, 0)),
            pl.BlockSpec((None, 1, 6 * D_MODEL),
                         lambda i: (layer * COND_ROWS + _cond_index(i, MIX_ROWS), 0, 0)),
            pl.BlockSpec((MIX_ROWS, Q_DIM), lambda i: (i, 0)),
            pl.BlockSpec((MIX_ROWS, C_CONV), lambda i: (i, 0)),
            pl.BlockSpec((MIX_ROWS, 2 * D_MODEL), lambda i: (i, 0)),
            layer_weight(Q_DIM),
            layer_weight(C_CONV),
            layer_weight(D_MODEL),
            full((1, D_MODEL)),
            full((1, D_MODEL)),
            full((D_MODEL, LANE)),
            full((D_MODEL, LANE)),
            full((1, LANE)),
        ],
        out_specs=[
            pl.BlockSpec((MIX_ROWS, D_MODEL), lambda i: (i, 0)),
            pl.BlockSpec((MIX_BLOCKS * SLOTS, D_MODEL), lambda i: (i, 0)),
            pl.BlockSpec((MIX_ROWS, SLOTS), lambda i: (i, 0)),
            pl.BlockSpec((MIX_BLOCKS, SUBLANE, LANE), lambda i: (i, 0, 0)),
        ],
        out_shape=[
            jax.ShapeDtypeStruct((N_TOK, D_MODEL), _F32),
            jax.ShapeDtypeStruct((N_BLOCKS * SLOTS, D_MODEL), _BF16),
            jax.ShapeDtypeStruct((N_TOK, SLOTS), _BF16),
            jax.ShapeDtypeStruct((N_BLOCKS, SUBLANE, LANE), _I32),
        ],
        scratch_shapes=[pltpu.VMEM((Q_DIM, D_MODEL), _BF16), pltpu.VMEM((C_CONV, D_MODEL), _BF16),
                        pltpu.VMEM((D_MODEL, D_MODEL), _BF16)],
        compiler_params=pltpu.CompilerParams(
            dimension_semantics=("arbitrary",), vmem_limit_bytes=VMEM_LIMIT),
        name="mix_router",
    )(x, mod3, att, cvn, sg, wa, wc, wo, g_post, g_ffn, wr_hi, wr_lo, b_router_pad)


def _plan(meta):
    units = meta[:, 0, :N_EXPERTS]
    seg_off = meta[:, 1, :N_EXPERTS]
    tiles = (jnp.sum(units, axis=0) + TILE_UNITS - 1) // TILE_UNITS
    tile_end = jnp.cumsum(tiles)
    n_tiles = tile_end[-1]
    region = (tile_end - tiles) * TILE_UNITS
    dst = region[None, :] + jnp.cumsum(units, axis=0) - units
    src = jnp.arange(N_BLOCKS, dtype=_I32)[:, None] * BLOCK_UNITS + seg_off

    tile_first = jnp.concatenate([jnp.zeros((1,), _I32), tile_end.astype(_I32)])
    last_units = jnp.sum(units, axis=0) - (tiles - 1) * TILE_UNITS
    half_expert = (tiles > 0) & (last_units <= TILE_UNITS // 2)
    tile_id = jnp.arange(N_ETILES, dtype=_I32)[:, None]
    tile_half = jnp.sum(((tile_id == tile_end[None, :] - 1) & half_expert[None, :]).astype(_I32), axis=1)

    dst_f, len_f, src_f = dst.reshape(1, -1), units.reshape(1, -1), src.reshape(1, -1)
    d = jnp.arange(N_ETILES * TILE_UNITS, dtype=_I32)[:, None]
    in_seg = (dst_f <= d) & (d < dst_f + len_f)
    src_unit = jnp.sum(jnp.where(in_seg, src_f + d - dst_f, 0), axis=1)

    u = jnp.arange(BLOCK_UNITS, dtype=_I32)[None, :, None]
    so, un = seg_off[:, None, :], units[:, None, :]
    in_blk = (so <= u) & (u < so + un)
    back_unit = jnp.sum(jnp.where(in_blk, dst[:, None, :] + u - so, 0), axis=2)
    back_unit = jnp.concatenate([back_unit.reshape(-1).astype(_I32),
                                 jnp.zeros(((COMBINE_BUFS - 1) * BLOCK_UNITS,), _I32)])
    return tile_first, n_tiles.reshape(1).astype(_I32), tile_half, src_unit.astype(_I32), back_unit


def _unit_gather(src_hbm, unit_ref, first, n_units, dst_buf, sem):
    for i in range(n_units):
        row = pl.multiple_of(unit_ref[first + i] * UNIT, UNIT)
        pltpu.make_async_copy(src_hbm.at[pl.ds(row, UNIT), :],
                              dst_buf.at[pl.ds(i * UNIT, UNIT), :], sem).start(priority=GATHER_PRIORITY)


def _unit_gather_wait(src_hbm, n_units, dst_buf, sem):
    pltpu.make_async_copy(src_hbm.at[pl.ds(0, n_units * UNIT), :], dst_buf, sem).wait()


def _tile_write(ybuf_slot, ys_hbm, tile, sem):
    row = pl.multiple_of(tile * TE, TE)
    return pltpu.make_async_copy(ybuf_slot, ys_hbm.at[pl.ds(row, TE), :], sem)


def _weight_chunk(wgu_hbm, wd_hbm, layer, expert, chunk, wgu_f32, wd_f32, buf, sem):
    r = pl.multiple_of(chunk * W_ROWS, W_ROWS)
    return (pltpu.make_async_copy(wgu_hbm.at[layer, expert, pl.ds(r, W_ROWS), :],
                                  wgu_f32.at[buf, pl.ds(r, W_ROWS), :], sem),
            pltpu.make_async_copy(wd_hbm.at[layer, expert, pl.ds(r, W_ROWS), :],
                                  wd_f32.at[buf, pl.ds(r, W_ROWS), :], sem))


def _expert_kernel(layer, reuse_out, first_ref, nt_ref, half_ref, src_ref, xp_hbm, wgu_hbm, bgu_ref,
                   wd_hbm, bd_ref, *rest):
    (ys_hbm, wgu_f32, wd_f32, wgu_bf, wd_bf, xbuf, ybuf, wsem, xsem,
     ysem) = rest[1:] if reuse_out else rest
    e = pl.program_id(0)
    n_live = nt_ref[0]
    t_lo = first_ref[e]
    t_hi = first_ref[e + 1]
    buf = e % 2
    has_next = e + 1 < N_EXPERTS

    def x_gather(tile):
        _unit_gather(xp_hbm, src_ref, tile * TILE_UNITS, TILE_UNITS,
                     xbuf.at[tile % X_BUFS], xsem.at[tile % X_BUFS])

    def start_chunk(expert, chunk, into):
        for cp in _weight_chunk(wgu_hbm, wd_hbm, layer, expert, chunk, wgu_f32, wd_f32,
                                into, wsem.at[into]):
            cp.start()

    @pl.when(e == 0)
    def _():
        for ahead in range(X_BUFS - 1):
            x_gather(ahead)
        for c in range(W_CHUNKS):
            start_chunk(0, c, 0)

    pltpu.make_async_copy(wgu_hbm.at[layer, e], wgu_f32.at[buf], wsem.at[buf]).wait()
    pltpu.make_async_copy(wd_hbm.at[layer, e], wd_f32.at[buf], wsem.at[buf]).wait()

    @pl.when(t_hi > t_lo)
    def _():
        wgu_bf[...] = wgu_f32[buf].astype(_BF16)
        wd_bf[...] = wd_f32[buf].astype(_BF16)

    def tile_body(t, carry):
        slot = t % 2

        x_gather(t + X_BUFS - 1)

        @pl.when(jnp.logical_and(has_next, t - t_lo < W_CHUNKS))
        def _():
            start_chunk(e + 1, t - t_lo, 1 - buf)

        _unit_gather_wait(xp_hbm, TILE_UNITS, xbuf.at[t % X_BUFS], xsem.at[t % X_BUFS])

        @pl.when(t >= 2)
        def _():
            _tile_write(ybuf.at[slot], ys_hbm, t - 2, ysem.at[slot]).wait()

        def ffn(rows):
            gu = jnp.dot(xbuf[t % X_BUFS, 0:rows, :], wgu_bf[...],
                         preferred_element_type=_F32) + bgu_ref[...]
            gate = jnp.minimum(gu[:, :D_FF], SWIGLU_LIMIT)
            lin = jnp.clip(gu[:, D_FF:], -SWIGLU_LIMIT, SWIGLU_LIMIT)
            act = gate * jax.nn.sigmoid(SWIGLU_ALPHA * gate) * (lin + 1.0)
            y = jnp.dot(act.astype(_BF16), wd_bf[...], preferred_element_type=_F32) + bd_ref[...]
            ybuf[slot, 0:rows, :] = y.astype(_BF16)

        @pl.when(half_ref[t] == 0)
        def _():
            ffn(TE)

        @pl.when(half_ref[t] != 0)
        def _():
            ffn(TE // 2)
            ybuf[slot, TE // 2:, :] = jnp.zeros((TE // 2, D_MODEL), _BF16)

        _tile_write(ybuf.at[slot], ys_hbm, t, ysem.at[slot]).start(priority=GATHER_PRIORITY)
        return carry

    lax.fori_loop(t_lo, t_hi, tile_body, 0)

    for c in range(W_CHUNKS):
        @pl.when(jnp.logical_and(has_next, c >= t_hi - t_lo))
        def _():
            start_chunk(e + 1, c, 1 - buf)

    @pl.when(e == N_EXPERTS - 1)
    def _():
        for ahead in range(X_BUFS - 1):
            t = n_live + ahead
            _unit_gather_wait(xp_hbm, TILE_UNITS, xbuf.at[t % X_BUFS], xsem.at[t % X_BUFS])
        for back in (2, 1):
            @pl.when(n_live >= back)
            def _():
                t = n_live - back
                _tile_write(ybuf.at[t % 2], ys_hbm, t, ysem.at[t % 2]).wait()

        if not reuse_out:
            ybuf[0] = jnp.zeros((TE, D_MODEL), _BF16)

            def zero_start(t, carry):
                _tile_write(ybuf.at[0], ys_hbm, t, ysem.at[0]).start()
                return carry

            def zero_wait(t, carry):
                _tile_write(ybuf.at[0], ys_hbm, t, ysem.at[0]).wait()
                return carry

            lax.fori_loop(n_live, N_ETILES, zero_start, 0)
            lax.fori_loop(n_live, N_ETILES, zero_wait, 0)


def _experts(tile_first, n_tiles, tile_half, src_unit, xp, w_gate_up, b_gate_up, w_down, b_down, layer,
             ys_prev):
    n_scalars, n_fixed_inputs = 4, 5
    grid_spec = pltpu.PrefetchScalarGridSpec(
        num_scalar_prefetch=4,
        grid=(N_EXPERTS,),
        in_specs=[
            pl.BlockSpec(memory_space=pl.ANY),
            pl.BlockSpec(memory_space=pl.ANY),
            pl.BlockSpec((None, None, 1, 2 * D_FF), lambda e, tf, nt, th, su: (layer, e, 0, 0)),
            pl.BlockSpec(memory_space=pl.ANY),
            pl.BlockSpec((None, None, 1, D_MODEL), lambda e, tf, nt, th, su: (layer, e, 0, 0)),
        ] + [pl.BlockSpec(memory_space=pl.ANY)] * len(ys_prev),
        out_specs=pl.BlockSpec(memory_space=pl.ANY),
        scratch_shapes=[
            pltpu.VMEM((2, D_MODEL, 2 * D_FF), _F32),
            pltpu.VMEM((2, D_FF, D_MODEL), _F32),
            pltpu.VMEM((D_MODEL, 2 * D_FF), _BF16),
            pltpu.VMEM((D_FF, D_MODEL), _BF16),
            pltpu.VMEM((X_BUFS, TE, D_MODEL), _BF16),
            pltpu.VMEM((2, TE, D_MODEL), _BF16),
            pltpu.SemaphoreType.DMA((2,)),
            pltpu.SemaphoreType.DMA((X_BUFS,)),
            pltpu.SemaphoreType.DMA((2,)),
        ],
    )
    return pl.pallas_call(
        functools.partial(_expert_kernel, layer, len(ys_prev) > 0),
        grid_spec=grid_spec,
        out_shape=jax.ShapeDtypeStruct((N_ETILES * TE, D_MODEL), _BF16),
        input_output_aliases={n_scalars + n_fixed_inputs + j: 0 for j in range(len(ys_prev))},
        compiler_params=pltpu.CompilerParams(
            dimension_semantics=("arbitrary",), vmem_limit_bytes=VMEM_LIMIT),
        name="experts",
    )(tile_first, n_tiles, tile_half, src_unit, xp, w_gate_up,
      b_gate_up.reshape(DEPTH, N_EXPERTS, 1, 2 * D_FF), w_down,
      b_down.reshape(DEPTH, N_EXPERTS, 1, D_MODEL), *ys_prev)


def _combine_kernel(split_out, back_ref, ys_hbm, x_ref, mod_ref, qw_ref, g_ref, *rest):
    outs, ybuf, sem = rest[:-2], rest[-2], rest[-1]
    b = pl.program_id(0)

    def gather(block):
        _unit_gather(ys_hbm, back_ref, block * BLOCK_UNITS, BLOCK_UNITS,
                     ybuf.at[block % COMBINE_BUFS], sem.at[block % COMBINE_BUFS])

    def gather_wait(block):
        _unit_gather_wait(ys_hbm, BLOCK_UNITS, ybuf.at[block % COMBINE_BUFS],
                          sem.at[block % COMBINE_BUFS])

    @pl.when(b == 0)
    def _():
        for ahead in range(COMBINE_BUFS - 1):
            gather(ahead)

    gather(b + COMBINE_BUFS - 1)
    gather_wait(b)
    moe = jnp.dot(qw_ref[...], ybuf[b % COMBINE_BUFS], preferred_element_type=_F32)

    @pl.when(b == N_BLOCKS - 1)
    def _():
        for ahead in range(1, COMBINE_BUFS):
            gather_wait(b + ahead)

    gt2 = mod_ref[:, 5 * D_MODEL:6 * D_MODEL]
    res = x_ref[...] + gt2 * _rms(moe, g_ref[...])
    if split_out:
        @pl.when(b < N_CTX // TB)
        def _():
            outs[0][...] = res

        @pl.when(b >= N_CTX // TB)
        def _():
            outs[1][...] = res
    else:
        outs[0][...] = res


def _combine(back_unit, ys, x1, mod3, qw, g_post, layer, split_out):
    n_ctx_blocks = N_CTX // TB
    if split_out:
        out_specs = [
            pl.BlockSpec((TB, D_MODEL), lambda b, bu: (jnp.minimum(b, n_ctx_blocks - 1), 0)),
            pl.BlockSpec((TB, D_MODEL), lambda b, bu: (jnp.maximum(b - n_ctx_blocks, 0), 0)),
        ]
        out_shape = [jax.ShapeDtypeStruct((N_CTX, D_MODEL), _F32),
                     jax.ShapeDtypeStruct((N_LAT, D_MODEL), _F32)]
    else:
        out_specs = [pl.BlockSpec((TB, D_MODEL), lambda b, bu: (b, 0))]
        out_shape = [jax.ShapeDtypeStruct((N_TOK, D_MODEL), _F32)]
    grid_spec = pltpu.PrefetchScalarGridSpec(
        num_scalar_prefetch=1,
        grid=(N_BLOCKS,),
        in_specs=[
            pl.BlockSpec(memory_space=pl.ANY),
            pl.BlockSpec((TB, D_MODEL), lambda b, bu: (b, 0)),
            pl.BlockSpec((None, 1, 6 * D_MODEL),
                         lambda b, bu: (layer * COND_ROWS + _cond_index(b, TB), 0, 0)),
            pl.BlockSpec((TB, SLOTS), lambda b, bu: (b, 0)),
            pl.BlockSpec((1, D_MODEL), lambda b, bu: (0, 0)),
        ],
        out_specs=out_specs,
        scratch_shapes=[
            pltpu.VMEM((COMBINE_BUFS, SLOTS, D_MODEL), _BF16),
            pltpu.SemaphoreType.DMA((COMBINE_BUFS,)),
        ],
    )
    return pl.pallas_call(
        functools.partial(_combine_kernel, split_out),
        grid_spec=grid_spec,
        out_shape=out_shape,
        compiler_params=pltpu.CompilerParams(
            dimension_semantics=("arbitrary",), vmem_limit_bytes=VMEM_LIMIT),
        name="combine_residual",
    )(back_unit, ys, x1, mod3, qw, g_post)


def _rope_tables():
    pos = jnp.arange(DEC_SEQ)
    row = (pos // GRID_W).astype(_F32)
    col = (pos % GRID_W).astype(_F32)
    inv = ROPE_THETA ** (-jnp.arange(ROPE_FREQS, dtype=_F32) / ROPE_FREQS)
    ang_r = row[:, None] * inv[None, :]
    ang_c = col[:, None] * inv[None, :]
    cos = jnp.concatenate([jnp.cos(ang_r)] * 2 + [jnp.cos(ang_c)] * 2, axis=-1)
    sin = jnp.concatenate([-jnp.sin(ang_r), jnp.sin(ang_r), -jnp.sin(ang_c), jnp.sin(ang_c)], axis=-1)
    reps = LANE // HEAD_DIM
    return jnp.tile(cos, (1, reps)), jnp.tile(sin, (1, reps))


def kernel(x_prompt, x_sample, cache_k, cache_v, c, c_ctx, w_ada, b_ada, g_pre_mix, g_post_mix,
           g_pre_ffn, g_post_ffn, w_in, attn_sink, w_attn_o, conv_w, conv_b, conv_ln_g, conv_ln_b,
           w_conv_o, w_out, w_router, b_router, w_gate_up, b_gate_up, w_down, b_down):
    cond =jnp.concatenate([c_ctx[None, :], c, jnp.zeros((COND_ROWS - N_COND, D_MODEL), _F32)], axis=0)
    mod = _modulation(cond, w_ada, b_ada)
    mod3 = mod.reshape(DEPTH * COND_ROWS, 1, 6 * D_MODEL)
    cos_t, sin_t = _rope_tables()
    ck = cache_k.reshape(DEC_BATCH, DEPTH, PAST_LEN, KV_DIM)
    cv = cache_v.reshape(DEC_BATCH, DEPTH, PAST_LEN, KV_DIM)
    w_router_pad = jnp.pad(w_router, ((0, 0), (0, 0), (0, LANE - N_EXPERTS)))
    wr_hi = w_router_pad.astype(_BF16)
    wr_lo = (w_router_pad - wr_hi.astype(_F32)).astype(_BF16)
    b_router_pad = jnp.pad(b_router, ((0, 0), (0, LANE - N_EXPERTS)), constant_values=-jnp.inf)

    kv_all = tuple(jnp.zeros((BATCH, DEPTH, SEQ, KV_DIM), _F32) for _ in range(2))
    x = None
    ys_prev = ()
    for l in range(DEPTH):
        row = lambda a: a[l][None, :]
        if l == 0:
            q, kv, u, sg, *kv_all, x = _inproj(
                x_prompt.reshape(N_CTX, D_MODEL), x_sample.reshape(N_LAT, D_MODEL), kv_all, mod3,
                row(g_pre_mix), w_in, cos_t, sin_t, l)
        else:
            q, kv, u, sg, *kv_all = _inproj(x, x, kv_all, mod3, row(g_pre_mix), w_in, cos_t, sin_t, l)
        att = _attention(q, kv, ck, cv, attn_sink[l], l)
        cvn = _conv_branch(u, conv_w[l], row(conv_b), row(conv_ln_g), row(conv_ln_b))
        x1, xp, qw, meta = _mix(
            x, mod3, att, cvn, sg, w_attn_o, w_conv_o, w_out, row(g_post_mix), row(g_pre_ffn),
            wr_hi[l], wr_lo[l], row(b_router_pad), l)
        tile_first, n_tiles, tile_half, src_unit, back_unit = _plan(meta)
        ys = _experts(tile_first, n_tiles, tile_half, src_unit, xp, w_gate_up, b_gate_up, w_down,
                      b_down, l, ys_prev)
        ys_prev = (ys,)
        outs = _combine(back_unit, ys, x1, mod3, qw, row(g_post_ffn), l, l == DEPTH - 1)
        x = outs[0]

    y_prompt = outs[0].reshape(BATCH, SEQ, D_MODEL)
    y_sample = outs[1].reshape(DEC_BATCH, DEC_SEQ, D_MODEL)
    new_k, new_v = (a.reshape(BATCH, DEPTH, SEQ, N_KV_HEADS, HEAD_DIM) for a in kv_all)
    return (y_prompt, y_sample, new_k, new_v)
```

```python
import functools

import jax
import jax.numpy as jnp
from jax import lax
from jax.experimental import pallas as pl
from jax.experimental.pallas import tpu as pltpu

D_MODEL = 1024
BATCH = 16
SEQ = 256
DEPTH = 2
DEC_BATCH = 2
DEC_SEQ = 2048
PAST_LEN = 256
GRID_W = 64
N_HEADS = 16
N_KV_HEADS = 4
GROUP = N_HEADS // N_KV_HEADS
HEAD_DIM = 64
Q_DIM = N_HEADS * HEAD_DIM
KV_DIM = N_KV_HEADS * HEAD_DIM
WINDOW = 128
ATTN_SCALE = HEAD_DIM ** -0.5
ROPE_THETA = 10000.0
ROPE_HALF = HEAD_DIM // 2
ROPE_FREQS = ROPE_HALF // 2
C_CONV = D_MODEL // 2
CONV_WIDTH = 31
CONV_PAD = (CONV_WIDTH - 1) // 2
N_EXPERTS = 32
TOP_K = 4
D_FF = D_MODEL
SWIGLU_LIMIT = 7.0
SWIGLU_ALPHA = 1.702
EPS = 1e-6
IN_COLS = Q_DIM + 2 * KV_DIM + 2 * C_CONV + 2 * D_MODEL

N_CTX = BATCH * SEQ
N_LAT = DEC_BATCH * DEC_SEQ
N_TOK = N_CTX + N_LAT
N_COND = 1 + DEC_BATCH

LANE = 128
SUBLANE = 8
COND_ROWS = SUBLANE
TM = 512
TQ = 256
TQ_SUB = 256
KWIN = TQ_SUB + 2 * WINDOW
assert GROUP == 4 and 2 * HEAD_DIM == LANE
LOG2E = 1.4426950408889634
QK_SCALE = ATTN_SCALE * LOG2E
HALO = 2 * SUBLANE
assert HALO >= CONV_PAD
CONV_SPAN = SEQ + ((HALO - CONV_PAD + CONV_WIDTH - 1) // SUBLANE) * SUBLANE
assert CONV_SPAN + SUBLANE - 1 <= SEQ + 2 * HALO

TB = 256
N_BLOCKS = N_TOK // TB
MIX_BLOCKS = 2
MIX_ROWS = MIX_BLOCKS * TB
UNIT = 2 * SUBLANE
SLOTS = -(-(TB * TOP_K + N_EXPERTS * (UNIT - 1)) // LANE) * LANE
BLOCK_UNITS = SLOTS // UNIT
TE = 256
TILE_UNITS = TE // UNIT
N_ETILES = (N_BLOCKS * BLOCK_UNITS) // TILE_UNITS + N_EXPERTS
COMBINE_BUFS = 3
X_BUFS = 3
Y_BUFS = 3
MAX_LIVE_TILES = (N_BLOCKS * ((TB * TOP_K + N_EXPERTS * (UNIT - 1)) // UNIT)) // TILE_UNITS + N_EXPERTS
assert MAX_LIVE_TILES + X_BUFS - 1 <= N_ETILES
W_CHUNKS = 4
W_ROWS = D_MODEL // W_CHUNKS
assert D_FF == D_MODEL
MOD_ROWS = 256
CONV_ROWS = 64
VMEM_LIMIT = 58 * 1024 * 1024

assert SLOTS >= TB * TOP_K + N_EXPERTS * (UNIT - 1) and SLOTS % UNIT == 0

_F32 = jnp.float32
_BF16 = jnp.bfloat16
_I32 = jnp.int32


def _rms(x, g):
    return x * lax.rsqrt(jnp.mean(x * x, axis=-1, keepdims=True) + EPS) * g


def _cond_index(i, tile):
    n_ctx_tiles = N_CTX // tile
    return jnp.where(i < n_ctx_tiles, 0, 1 + (i - n_ctx_tiles) // (DEC_SEQ // tile))


def _mod_kernel(cond_ref, w_ref, b_ref, out_ref):
    k = pl.program_id(1)
    cnd = cond_ref[...]
    s = cnd * jax.nn.sigmoid(cnd)
    w = w_ref[...]
    w_hi = w.astype(_BF16)
    w_lo = (w - w_hi.astype(_F32)).astype(_BF16)
    s_hi = s.astype(_BF16)
    s_lo = (s - s_hi.astype(_F32)).astype(_BF16)
    part = (jnp.dot(s_hi, w_hi, preferred_element_type=_F32)
            + jnp.dot(s_lo, w_hi, preferred_element_type=_F32)
            + jnp.dot(s_hi, w_lo, preferred_element_type=_F32))

    @pl.when(k == 0)
    def _():
        out_ref[...] = part + b_ref[...]

    @pl.when(k > 0)
    def _():
        out_ref[...] += part


def _modulation(cond, w_ada, b_ada):
    return pl.pallas_call(
        _mod_kernel,
        grid=(DEPTH, D_MODEL // MOD_ROWS),
        in_specs=[
            pl.BlockSpec((COND_ROWS, MOD_ROWS), lambda l, k: (0, k)),
            pl.BlockSpec((None, MOD_ROWS, 6 * D_MODEL), lambda l, k: (l, k, 0)),
            pl.BlockSpec((None, 1, 6 * D_MODEL), lambda l, k: (l, 0, 0)),
        ],
        out_specs=pl.BlockSpec((None, COND_ROWS, 6 * D_MODEL), lambda l, k: (l, 0, 0)),
        out_shape=jax.ShapeDtypeStruct((DEPTH, COND_ROWS, 6 * D_MODEL), _F32),
        compiler_params=pltpu.CompilerParams(
            dimension_semantics=("arbitrary", "arbitrary"), vmem_limit_bytes=VMEM_LIMIT),
        name="modulation",
    )(cond, w_ada, b_ada.reshape(DEPTH, 1, 6 * D_MODEL))


def _rope_chunk(x, cos, sin):
    lane = lax.broadcasted_iota(_I32, x.shape, 1)
    partner = jnp.where((lane & ROPE_FREQS) == 0,
                        pltpu.roll(x, LANE - ROPE_FREQS, 1), pltpu.roll(x, ROPE_FREQS, 1))
    return x * cos + partner * sin


def _inproj_kernel(first, *refs):
    n_in = 9
    xa_ref, xb_ref, mod_ref, g_ref, wf_ref, cos_ref, sin_ref = refs[:7]
    q_ref, kv_ref, u_ref, sg_ref = refs[n_in:n_in + 4]
    new_refs = refs[n_in + 4:n_in + 6]
    rest = refs[n_in + 6:]
    w_ref = rest[-1]
    i = pl.program_id(0)

    @pl.when(i == 0)
    def _():
        w_ref[...] = wf_ref[...].astype(_BF16)

    x = jnp.where(i < N_CTX // TM, xa_ref[...], xb_ref[...])
    if first:
        rest[0][...] = x
    sh = mod_ref[:, 0:D_MODEL]
    sc = mod_ref[:, D_MODEL:2 * D_MODEL]
    h = (_rms(x, g_ref[...]) * (1.0 + sc) + sh).astype(_BF16)

    c0 = 0
    q = jnp.dot(h, w_ref[:, c0:c0 + Q_DIM], preferred_element_type=_F32) * QK_SCALE
    c0 += Q_DIM
    kv = jnp.dot(h, w_ref[:, c0:c0 + 2 * KV_DIM], preferred_element_type=_F32)
    c0 += 2 * KV_DIM
    ua = jnp.dot(h, w_ref[:, c0:c0 + C_CONV], preferred_element_type=_F32)
    c0 += C_CONV
    ub = jnp.dot(h, w_ref[:, c0:c0 + C_CONV], preferred_element_type=_F32)
    c0 += C_CONV
    g = jnp.dot(h, w_ref[:, c0:c0 + 2 * D_MODEL], preferred_element_type=_F32)

    u_ref[...] = ua * jax.nn.sigmoid(ub)
    sg_ref[...] = jax.nn.sigmoid(g).astype(_BF16)

    is_latent = i >= N_CTX // TM

    @pl.when(jnp.logical_not(is_latent))
    def _():
        q_ref[...] = q.astype(_BF16)
        kv_ref[...] = kv
        for which, new_ref in enumerate(new_refs):
            cols = slice(which * KV_DIM, (which + 1) * KV_DIM)
            for r in range(TM // SEQ):
                rows = slice(r * SEQ, (r + 1) * SEQ)
                new_ref[r] = kv[rows, cols]

    @pl.when(is_latent)
    def _():
        cos = cos_ref[...]
        sin = sin_ref[...]
        for j in range(Q_DIM // LANE):
            sl = slice(j * LANE, (j + 1) * LANE)
            q_ref[:, sl] = _rope_chunk(q[:, sl], cos, sin).astype(_BF16)
        for j in range(KV_DIM // LANE):
            sl = slice(j * LANE, (j + 1) * LANE)
            kv_ref[:, sl] = _rope_chunk(kv[:, sl], cos, sin)
        kv_ref[:, KV_DIM:] = kv[:, KV_DIM:]


def _inproj(x_ctx, x_lat, kv_all, mod3, g_pre, w_in, cos_t, sin_t, layer):
    n_ctx_tiles = N_CTX // TM
    lat_tiles = DEC_SEQ // TM
    stack_x = layer == 0
    lat_off = 0 if stack_x else n_ctx_tiles

    def rope_map(i):
        return (jnp.where(i < n_ctx_tiles, 0, (i - n_ctx_tiles) % lat_tiles), 0)

    row_tile = lambda width: pl.BlockSpec((TM, width), lambda i: (i, 0))
    new_kv = pl.BlockSpec((TM // SEQ, None, SEQ, KV_DIM),
                          lambda i: (jnp.minimum(i, n_ctx_tiles - 1), layer, 0, 0))
    new_kv_shape = jax.ShapeDtypeStruct((BATCH, DEPTH, SEQ, KV_DIM), _F32)
    n_fixed_inputs = 7
    out_specs = [row_tile(Q_DIM), row_tile(2 * KV_DIM), row_tile(C_CONV), row_tile(2 * D_MODEL),
                 new_kv, new_kv]
    out_shape = [
        jax.ShapeDtypeStruct((N_TOK, Q_DIM), _BF16),
        jax.ShapeDtypeStruct((N_TOK, 2 * KV_DIM), _F32),
        jax.ShapeDtypeStruct((N_TOK, C_CONV), _F32),
        jax.ShapeDtypeStruct((N_TOK, 2 * D_MODEL), _BF16),
        new_kv_shape, new_kv_shape,
    ]
    if stack_x:
        out_specs.append(row_tile(D_MODEL))
        out_shape.append(jax.ShapeDtypeStruct((N_TOK, D_MODEL), _F32))
    return pl.pallas_call(
        functools.partial(_inproj_kernel, stack_x),
        grid=(N_TOK // TM,),
        in_specs=[
            pl.BlockSpec((TM, D_MODEL), lambda i: (jnp.minimum(i, n_ctx_tiles - 1), 0)),
            pl.BlockSpec((TM, D_MODEL), lambda i: (lat_off + jnp.maximum(i - n_ctx_tiles, 0), 0)),
            pl.BlockSpec((None, 1, 6 * D_MODEL),
                         lambda i: (layer * COND_ROWS + _cond_index(i, TM), 0, 0)),
            pl.BlockSpec((1, D_MODEL), lambda i: (0, 0)),
            pl.BlockSpec((None, D_MODEL, IN_COLS), lambda i: (layer, 0, 0),
                         pipeline_mode=pl.Buffered(1)),
            pl.BlockSpec((TM, LANE), rope_map),
            pl.BlockSpec((TM, LANE), rope_map),
        ] + [pl.BlockSpec(memory_space=pl.ANY)] * len(kv_all),
        out_specs=out_specs,
        out_shape=out_shape,
        input_output_aliases={n_fixed_inputs + j: 4 + j for j in range(len(kv_all))},
        scratch_shapes=[pltpu.VMEM((D_MODEL, IN_COLS), _BF16)],
        compiler_params=pltpu.CompilerParams(
            dimension_semantics=("arbitrary",), vmem_limit_bytes=VMEM_LIMIT),
        name="inproj",
    )(x_ctx, x_lat, mod3, g_pre, w_in, cos_t, sin_t, *kv_all)


def _pair_operands(k, v):
    zero = jnp.zeros_like(k)
    one = jnp.ones_like(v)
    ka = jnp.concatenate([k, zero], axis=1).astype(_BF16)
    kb = jnp.concatenate([zero, k], axis=1).astype(_BF16)
    va = jnp.concatenate([v, zero, one, zero], axis=1).astype(_BF16)
    vb = jnp.concatenate([zero, v, zero, one], axis=1).astype(_BF16)
    return ka, kb, va, vb


def _pair_attend(qq, operands, masks, sink_a, sink_b):
    def scores(which):
        out = []
        for ops, mask in zip(operands, masks):
            s = lax.dot_general(qq, ops[which], (((1,), (1,)), ((), ())),
                                preferred_element_type=_F32)
            out.append(s if mask is None else jnp.where(mask, s, -jnp.inf))
        return out

    acc = jnp.zeros((qq.shape[0], 2 * LANE), _F32)
    sink_terms = []
    for which, sink in ((0, sink_a), (1, sink_b)):
        ss = scores(which)
        m = sink
        for s in ss:
            m = jnp.maximum(m, jnp.max(s, axis=-1, keepdims=True))
        for s, ops in zip(ss, operands):
            acc = acc + jnp.dot(jnp.exp2(s - m).astype(_BF16), ops[2 + which],
                                preferred_element_type=_F32)
        sink_terms.append(jnp.exp2(sink - m))
    lane = lax.broadcasted_iota(_I32, (qq.shape[0], LANE), 1)
    sink_term = jnp.where(lane < HEAD_DIM, sink_terms[0], sink_terms[1])
    return acc[:, :LANE] / (acc[:, LANE:] + sink_term)


def _group_attend(q_ref, o_ref, hk, rows, operands, masks, sink_ref):
    n_rows = rows.stop - rows.start
    pairs = [slice((2 * hk + j) * LANE, (2 * hk + j + 1) * LANE) for j in range(GROUP // 2)]
    qq = jnp.concatenate([q_ref[rows, sl] for sl in pairs], axis=0)
    first = lax.broadcasted_iota(_I32, (qq.shape[0], 1), 0) < n_rows
    sink_a = jnp.where(first, sink_ref[GROUP * hk], sink_ref[GROUP * hk + 2]) * LOG2E
    sink_b = jnp.where(first, sink_ref[GROUP * hk + 1], sink_ref[GROUP * hk + 3]) * LOG2E
    out = _pair_attend(qq, operands, masks, sink_a, sink_b)
    for j, sl in enumerate(pairs):
        o_ref[rows, sl] = out[j * n_rows:(j + 1) * n_rows].astype(_BF16)


def _attn_kernel(sink_ref, q_ref, kv_own_ref, kv_seq_ref, ck_ref, cv_ref, o_ref):
    i = pl.program_id(0)
    n_ctx_steps = N_CTX // TQ

    @pl.when(i < n_ctx_steps)
    def _():
        for hk in range(N_KV_HEADS):
            ks = slice(hk * HEAD_DIM, (hk + 1) * HEAD_DIM)
            vs = slice(KV_DIM + hk * HEAD_DIM, KV_DIM + (hk + 1) * HEAD_DIM)
            own = _pair_operands(kv_own_ref[:, ks], kv_own_ref[:, vs])
            _group_attend(q_ref, o_ref, hk, slice(0, TQ), [own], [None], sink_ref)

    @pl.when(i >= n_ctx_steps)
    def _():
        qb = (i - n_ctx_steps) % (DEC_SEQ // TQ)
        stacked = (GROUP // 2) * TQ_SUB
        starts, masks = [], []
        for sb in range(TQ // TQ_SUB):
            q_start = qb * TQ + sb * TQ_SUB
            k_start = pl.multiple_of(jnp.clip(q_start - WINDOW, 0, DEC_SEQ - KWIN), WINDOW)
            qpos = q_start + lax.broadcasted_iota(_I32, (stacked, KWIN), 0) % TQ_SUB
            kpos = k_start + lax.broadcasted_iota(_I32, (stacked, KWIN), 1)
            starts.append(k_start)
            masks.append(jnp.abs(kpos - qpos) <= WINDOW)
        for hk in range(N_KV_HEADS):
            ks = slice(hk * HEAD_DIM, (hk + 1) * HEAD_DIM)
            vs = slice(KV_DIM + hk * HEAD_DIM, KV_DIM + (hk + 1) * HEAD_DIM)
            cached = _pair_operands(ck_ref[:, ks], cv_ref[:, ks])
            for sb in range(TQ // TQ_SUB):
                local = _pair_operands(kv_seq_ref[pl.ds(starts[sb], KWIN), ks],
                                       kv_seq_ref[pl.ds(starts[sb], KWIN), vs])
                _group_attend(q_ref, o_ref, hk, slice(sb * TQ_SUB, (sb + 1) * TQ_SUB),
                              [local, cached], [masks[sb], None], sink_ref)


def _attention(q, kv, cache_k, cache_v, sink, layer):
    n_ctx_steps = N_CTX // TQ
    nq = DEC_SEQ // TQ
    kv_off = N_CTX // DEC_SEQ

    def lat_batch(i):
        return jnp.maximum(i - n_ctx_steps, 0) // nq

    return pl.pallas_call(
        _attn_kernel,
        grid=(N_TOK // TQ,),
        in_specs=[
            pl.BlockSpec(memory_space=pltpu.SMEM),
            pl.BlockSpec((TQ, Q_DIM), lambda i: (i, 0)),
            pl.BlockSpec((TQ, 2 * KV_DIM), lambda i: (i, 0)),
            pl.BlockSpec((DEC_SEQ, 2 * KV_DIM), lambda i: (kv_off + lat_batch(i), 0)),
            pl.BlockSpec((None, None, PAST_LEN, KV_DIM), lambda i: (lat_batch(i), layer, 0, 0)),
            pl.BlockSpec((None, None, PAST_LEN, KV_DIM), lambda i: (lat_batch(i), layer, 0, 0)),
        ],
        out_specs=pl.BlockSpec((TQ, Q_DIM), lambda i: (i, 0)),
        out_shape=jax.ShapeDtypeStruct((N_TOK, Q_DIM), _BF16),
        compiler_params=pltpu.CompilerParams(
            dimension_semantics=("arbitrary",), vmem_limit_bytes=VMEM_LIMIT),
        name="attention",
    )(sink, q, kv, kv, cache_k, cache_v)


def _conv_kernel(prev_ref, cur_ref, next_ref, w_ref, b_ref, lg_ref, lb_ref, y_ref, pad_ref, sh_ref):
    i = pl.program_id(0)
    n_ctx_tiles = N_CTX // SEQ
    tiles_per_seq = jnp.where(i < n_ctx_tiles, 1, DEC_SEQ // SEQ)
    j = jnp.where(i < n_ctx_tiles, 0, (i - n_ctx_tiles) % (DEC_SEQ // SEQ))
    pad_ref[0:HALO, :] = jnp.where(j > 0, prev_ref[...], 0.0)
    pad_ref[HALO:HALO + SEQ, :] = cur_ref[...]
    pad_ref[HALO + SEQ:HALO + SEQ + HALO, :] = jnp.where(j < tiles_per_seq - 1, next_ref[...], 0.0)

    for r in range(SUBLANE):
        sh_ref[r] = pad_ref[r:r + CONV_SPAN, :]

    rows = CONV_ROWS
    for r0 in range(0, SEQ, rows):
        acc = jnp.zeros((rows, C_CONV), _F32) + b_ref[...]
        for t in range(CONV_WIDTH):
            off = HALO - CONV_PAD + t
            start = (off // SUBLANE) * SUBLANE + r0
            acc = acc + sh_ref[off % SUBLANE, start:start + rows, :] * w_ref[t:t + 1, :]
        mu = jnp.mean(acc, axis=-1, keepdims=True)
        d = acc - mu
        var = jnp.mean(d * d, axis=-1, keepdims=True)
        y = d * lax.rsqrt(var + EPS) * lg_ref[...] + lb_ref[...]
        y_ref[r0:r0 + rows, :] = (y * jax.nn.sigmoid(y)).astype(_BF16)


def _conv_branch(u, conv_w, conv_b, ln_g, ln_b):
    n_tiles = N_TOK // SEQ
    hb = SEQ // HALO
    last = N_TOK // HALO - 1
    return pl.pallas_call(
        _conv_kernel,
        grid=(n_tiles,),
        in_specs=[
            pl.BlockSpec((HALO, C_CONV), lambda i: (jnp.maximum(i * hb - 1, 0), 0)),
            pl.BlockSpec((SEQ, C_CONV), lambda i: (i, 0)),
            pl.BlockSpec((HALO, C_CONV), lambda i: (jnp.minimum((i + 1) * hb, last), 0)),
            pl.BlockSpec((CONV_WIDTH, C_CONV), lambda i: (0, 0)),
            pl.BlockSpec((1, C_CONV), lambda i: (0, 0)),
            pl.BlockSpec((1, C_CONV), lambda i: (0, 0)),
            pl.BlockSpec((1, C_CONV), lambda i: (0, 0)),
        ],
        out_specs=pl.BlockSpec((SEQ, C_CONV), lambda i: (i, 0)),
        out_shape=jax.ShapeDtypeStruct((N_TOK, C_CONV), _BF16),
        scratch_shapes=[pltpu.VMEM((SEQ + 2 * HALO, C_CONV), _F32),
                        pltpu.VMEM((SUBLANE, CONV_SPAN, C_CONV), _F32)],
        compiler_params=pltpu.CompilerParams(
            dimension_semantics=("arbitrary",), vmem_limit_bytes=VMEM_LIMIT),
        name="conv_branch",
    )(u, u, u, conv_w, conv_b, ln_g, ln_b)


def _mix_kernel(x_ref, mod_ref, att_ref, cv_ref, sg_ref, waf_ref, wcf_ref, wof_ref,
                gpost_ref, gffn_ref, wr_ref, wrlo_ref, br_ref,
                x1_ref, xp_ref, qw_ref, meta_ref, wa_ref, wc_ref, wo_ref):
    @pl.when(pl.program_id(0) == 0)
    def _():
        wa_ref[...] = waf_ref[...].astype(_BF16)
        wc_ref[...] = wcf_ref[...].astype(_BF16)
        wo_ref[...] = wof_ref[...].astype(_BF16)

    a = jnp.dot(att_ref[...], wa_ref[...], preferred_element_type=_F32)
    cv = jnp.dot(cv_ref[...], wc_ref[...], preferred_element_type=_F32)
    m = sg_ref[:, 0:D_MODEL].astype(_F32) * a + sg_ref[:, D_MODEL:].astype(_F32) * cv
    mix = jnp.dot(m.astype(_BF16), wo_ref[...], preferred_element_type=_F32)
    gt1 = mod_ref[:, 2 * D_MODEL:3 * D_MODEL]
    sh2 = mod_ref[:, 3 * D_MODEL:4 * D_MODEL]
    sc2 = mod_ref[:, 4 * D_MODEL:5 * D_MODEL]
    x1 = x_ref[...] + gt1 * _rms(mix, gpost_ref[...])
    x1_ref[...] = x1
    h2 = _rms(x1, gffn_ref[...]) * (1.0 + sc2) + sh2
    h2b = h2.astype(_BF16)

    h2_lo = (h2 - h2b.astype(_F32)).astype(_BF16)
    logits = (jnp.dot(h2b, wr_ref[...], preferred_element_type=_F32)
              + jnp.dot(h2_lo, wr_ref[...], preferred_element_type=_F32)
              + jnp.dot(h2b, wrlo_ref[...], preferred_element_type=_F32) + br_ref[...])

    for blk in range(MIX_BLOCKS):
        rows = slice(blk * TB, (blk + 1) * TB)
        qw, xp, meta = _route_block(logits[rows], h2b[rows])
        qw_ref[rows, :] = qw
        xp_ref[blk * SLOTS:(blk + 1) * SLOTS, :] = xp
        meta_ref[blk] = meta


def _route_block(logits, h2b):
    lane = lax.broadcasted_iota(_I32, (TB, LANE), 1).astype(_F32)
    member = jnp.zeros((TB, LANE), _F32)
    hots, exps = [], []
    top = None
    total = jnp.zeros((TB, 1), _F32)
    for k in range(TOP_K):
        mval = jnp.max(logits, axis=-1, keepdims=True)
        sel = jnp.min(jnp.where(logits == mval, lane, float(LANE)), axis=-1, keepdims=True)
        if top is None:
            top = mval
        e = jnp.exp(mval - top)
        total = total + e
        hot = lane == sel
        hots.append(hot)
        exps.append(e)
        member = member + jnp.where(hot, 1.0, 0.0)
        logits = jnp.where(hot, -jnp.inf, logits)

    r_i = lax.broadcasted_iota(_I32, (TB, TB), 0)
    c_i = lax.broadcasted_iota(_I32, (TB, TB), 1)
    lower = jnp.where(r_i > c_i, 1.0, 0.0).astype(_BF16)
    rank = jnp.dot(lower, member.astype(_BF16), preferred_element_type=_F32)
    count = jnp.sum(member, axis=0, keepdims=True)
    units = jnp.floor((count + float(UNIT - 1)) * (1.0 / UNIT))
    r_l = lax.broadcasted_iota(_I32, (LANE, LANE), 0)
    c_l = lax.broadcasted_iota(_I32, (LANE, LANE), 1)
    upper = jnp.where(r_l < c_l, 1.0, 0.0).astype(_BF16)
    unit_off = jnp.dot(jnp.broadcast_to(units, (SUBLANE, LANE)).astype(_BF16), upper,
                       preferred_element_type=_F32)[0:1, :]
    base = unit_off * float(UNIT) + rank

    slot_lane = lax.broadcasted_iota(_I32, (TB, SLOTS), 1).astype(_F32)
    qw = jnp.zeros((TB, SLOTS), _F32)
    hit = jnp.zeros((TB, SLOTS), _F32)
    for k in range(TOP_K):
        slot = jnp.sum(jnp.where(hots[k], base, 0.0), axis=-1, keepdims=True)
        here = slot_lane == slot
        qw = jnp.where(here, exps[k] / total, qw)
        hit = jnp.where(here, 1.0, hit)

    xp = lax.dot_general(hit.astype(_BF16), h2b, (((0,), (0,)), ((), ())),
                         preferred_element_type=_F32).astype(_BF16)

    sub = lax.broadcasted_iota(_I32, (SUBLANE, LANE), 0)
    meta = jnp.where(sub == 0, units, jnp.where(sub == 1, unit_off, 0.0))
    return qw.astype(_BF16), xp, meta.astype(_I32)


def _mix(x, mod3, att, cvn, sg, wa, wc, wo, g_post, g_ffn, wr_hi, wr_lo, b_router_pad, layer):
    full = lambda shape: pl.BlockSpec(shape, lambda i: (0,) * len(shape))
    layer_weight = lambda rows: pl.BlockSpec((None, rows, D_MODEL), lambda i: (layer, 0, 0),
                                             pipeline_mode=pl.Buffered(1))
    return pl.pallas_call(
        _mix_kernel,
        grid=(N_BLOCKS // MIX_BLOCKS,),
        in_specs=[
            pl.BlockSpec((MIX_ROWS, D_MODEL), lambda i: (i, 0)),
            pl.BlockSpec((None, 1, 6 * D_MODEL),
                         lambda i: (layer * COND_ROWS + _cond_index(i, MIX_ROWS), 0, 0)),
            pl.BlockSpec((MIX_ROWS, Q_DIM), lambda i: (i, 0)),
            pl.BlockSpec((MIX_ROWS, C_CONV), lambda i: (i, 0)),
            pl.BlockSpec((MIX_ROWS, 2 * D_MODEL), lambda i: (i, 0)),
            layer_weight(Q_DIM),
            layer_weight(C_CONV),
            layer_weight(D_MODEL),
            full((1, D_MODEL)),
            full((1, D_MODEL)),
            full((D_MODEL, LANE)),
            full((D_MODEL, LANE)),
            full((1, LANE)),
        ],
        out_specs=[
            pl.BlockSpec((MIX_ROWS, D_MODEL), lambda i: (i, 0)),
            pl.BlockSpec((MIX_BLOCKS * SLOTS, D_MODEL), lambda i: (i, 0)),
            pl.BlockSpec((MIX_ROWS, SLOTS), lambda i: (i, 0)),
            pl.BlockSpec((MIX_BLOCKS, SUBLANE, LANE), lambda i: (i, 0, 0)),
        ],
        out_shape=[
            jax.ShapeDtypeStruct((N_TOK, D_MODEL), _F32),
            jax.ShapeDtypeStruct((N_BLOCKS * SLOTS, D_MODEL), _BF16),
            jax.ShapeDtypeStruct((N_TOK, SLOTS), _BF16),
            jax.ShapeDtypeStruct((N_BLOCKS, SUBLANE, LANE), _I32),
        ],
        scratch_shapes=[pltpu.VMEM((Q_DIM, D_MODEL), _BF16), pltpu.VMEM((C_CONV, D_MODEL), _BF16),
                        pltpu.VMEM((D_MODEL, D_MODEL), _BF16)],
        compiler_params=pltpu.CompilerParams(
            dimension_semantics=("arbitrary",), vmem_limit_bytes=VMEM_LIMIT),
        name="mix_router",
    )(x, mod3, att, cvn, sg, wa, wc, wo, g_post, g_ffn, wr_hi, wr_lo, b_router_pad)


def _plan(meta):
    units = meta[:, 0, :N_EXPERTS]
    seg_off = meta[:, 1, :N_EXPERTS]
    tiles = (jnp.sum(units, axis=0) + TILE_UNITS - 1) // TILE_UNITS
    tile_end = jnp.cumsum(tiles)
    n_tiles = tile_end[-1]
    region = (tile_end - tiles) * TILE_UNITS
    dst = region[None, :] + jnp.cumsum(units, axis=0) - units
    src = jnp.arange(N_BLOCKS, dtype=_I32)[:, None] * BLOCK_UNITS + seg_off

    tile_first = jnp.concatenate([jnp.zeros((1,), _I32), tile_end.astype(_I32)])
    last_units = jnp.sum(units, axis=0) - (tiles - 1) * TILE_UNITS
    half_expert = (tiles > 0) & (last_units <= TILE_UNITS // 2)
    tile_id = jnp.arange(N_ETILES, dtype=_I32)[:, None]
    tile_half = jnp.sum(((tile_id == tile_end[None, :] - 1) & half_expert[None, :]).astype(_I32), axis=1)

    dst_f, len_f, src_f = dst.reshape(1, -1), units.reshape(1, -1), src.reshape(1, -1)
    d = jnp.arange(N_ETILES * TILE_UNITS, dtype=_I32)[:, None]
    in_seg = (dst_f <= d) & (d < dst_f + len_f)
    src_unit = jnp.sum(jnp.where(in_seg, src_f + d - dst_f, 0), axis=1)

    u = jnp.arange(BLOCK_UNITS, dtype=_I32)[None, :, None]
    so, un = seg_off[:, None, :], units[:, None, :]
    in_blk = (so <= u) & (u < so + un)
    back_unit = jnp.sum(jnp.where(in_blk, dst[:, None, :] + u - so, 0), axis=2)
    back_unit = jnp.concatenate([back_unit.reshape(-1).astype(_I32),
                                 jnp.zeros(((COMBINE_BUFS - 1) * BLOCK_UNITS,), _I32)])
    return tile_first, n_tiles.reshape(1).astype(_I32), tile_half, src_unit.astype(_I32), back_unit


def _unit_gather(src_hbm, unit_ref, first, n_units, dst_buf, sem):
    for i in range(n_units):
        row = pl.multiple_of(unit_ref[first + i] * UNIT, UNIT)
        pltpu.make_async_copy(src_hbm.at[pl.ds(row, UNIT), :],
                              dst_buf.at[pl.ds(i * UNIT, UNIT), :], sem).start()


def _unit_gather_wait(src_hbm, n_units, dst_buf, sem):
    pltpu.make_async_copy(src_hbm.at[pl.ds(0, n_units * UNIT), :], dst_buf, sem).wait()


def _tile_write(ybuf_slot, ys_hbm, tile, sem):
    row = pl.multiple_of(tile * TE, TE)
    return pltpu.make_async_copy(ybuf_slot, ys_hbm.at[pl.ds(row, TE), :], sem)


def _weight_chunk(wgu_hbm, wd_hbm, layer, expert, chunk, wgu_f32, wd_f32, buf, sem):
    r = pl.multiple_of(chunk * W_ROWS, W_ROWS)
    return (pltpu.make_async_copy(wgu_hbm.at[layer, expert, pl.ds(r, W_ROWS), :],
                                  wgu_f32.at[buf, pl.ds(r, W_ROWS), :], sem),
            pltpu.make_async_copy(wd_hbm.at[layer, expert, pl.ds(r, W_ROWS), :],
                                  wd_f32.at[buf, pl.ds(r, W_ROWS), :], sem))


def _expert_kernel(layer, reuse_out, first_ref, nt_ref, half_ref, src_ref, xp_hbm, wgu_hbm, bgu_ref,
                   wd_hbm, bd_ref, *rest):
    (ys_hbm, wgu_f32, wd_f32, wgu_bf, wd_bf, xbuf, ybuf, wsem, xsem,
     ysem) = rest[1:] if reuse_out else rest
    e = pl.program_id(0)
    n_live = nt_ref[0]
    t_lo = first_ref[e]
    t_hi = first_ref[e + 1]
    buf = e % 2
    has_next = e + 1 < N_EXPERTS

    def x_gather(tile):
        _unit_gather(xp_hbm, src_ref, tile * TILE_UNITS, TILE_UNITS,
                     xbuf.at[tile % X_BUFS], xsem.at[tile % X_BUFS])

    def start_chunk(expert, chunk, into):
        for cp in _weight_chunk(wgu_hbm, wd_hbm, layer, expert, chunk, wgu_f32, wd_f32,
                                into, wsem.at[into]):
            cp.start()

    @pl.when(e == 0)
    def _():
        for ahead in range(X_BUFS - 1):
            x_gather(ahead)
        for c in range(W_CHUNKS):
            start_chunk(0, c, 0)

    pltpu.make_async_copy(wgu_hbm.at[layer, e], wgu_f32.at[buf], wsem.at[buf]).wait()
    pltpu.make_async_copy(wd_hbm.at[layer, e], wd_f32.at[buf], wsem.at[buf]).wait()

    @pl.when(t_hi > t_lo)
    def _():
        wgu_bf[...] = wgu_f32[buf].astype(_BF16)
        wd_bf[...] = wd_f32[buf].astype(_BF16)

    def tile_body(t, carry):
        slot = t % Y_BUFS

        x_gather(t + X_BUFS - 1)

        @pl.when(jnp.logical_and(has_next, t - t_lo < W_CHUNKS))
        def _():
            start_chunk(e + 1, t - t_lo, 1 - buf)

        _unit_gather_wait(xp_hbm, TILE_UNITS, xbuf.at[t % X_BUFS], xsem.at[t % X_BUFS])

        @pl.when(t >= Y_BUFS)
        def _():
            _tile_write(ybuf.at[slot], ys_hbm, t - Y_BUFS, ysem.at[slot]).wait()

        def ffn(rows):
            gu = jnp.dot(xbuf[t % X_BUFS, 0:rows, :], wgu_bf[...],
                         preferred_element_type=_F32) + bgu_ref[...]
            gate = jnp.minimum(gu[:, :D_FF], SWIGLU_LIMIT)
            lin = jnp.clip(gu[:, D_FF:], -SWIGLU_LIMIT, SWIGLU_LIMIT)
            act = gate * jax.nn.sigmoid(SWIGLU_ALPHA * gate) * (lin + 1.0)
            y = jnp.dot(act.astype(_BF16), wd_bf[...], preferred_element_type=_F32) + bd_ref[...]
            ybuf[slot, 0:rows, :] = y.astype(_BF16)

        @pl.when(half_ref[t] == 0)
        def _():
            ffn(TE)

        @pl.when(half_ref[t] != 0)
        def _():
            ffn(TE // 2)
            ybuf[slot, TE // 2:, :] = jnp.zeros((TE // 2, D_MODEL), _BF16)

        _tile_write(ybuf.at[slot], ys_hbm, t, ysem.at[slot]).start()
        return carry

    lax.fori_loop(t_lo, t_hi, tile_body, 0)

    for c in range(W_CHUNKS):
        @pl.when(jnp.logical_and(has_next, c >= t_hi - t_lo))
        def _():
            start_chunk(e + 1, c, 1 - buf)

    @pl.when(e == N_EXPERTS - 1)
    def _():
        for ahead in range(X_BUFS - 1):
            t = n_live + ahead
            _unit_gather_wait(xp_hbm, TILE_UNITS, xbuf.at[t % X_BUFS], xsem.at[t % X_BUFS])
        for back in range(Y_BUFS, 0, -1):
            @pl.when(n_live >= back)
            def _():
                t = n_live - back
                _tile_write(ybuf.at[t % Y_BUFS], ys_hbm, t, ysem.at[t % Y_BUFS]).wait()

        if not reuse_out:
            ybuf[0] = jnp.zeros((TE, D_MODEL), _BF16)

            def zero_start(t, carry):
                _tile_write(ybuf.at[0], ys_hbm, t, ysem.at[0]).start()
                return carry

            def zero_wait(t, carry):
                _tile_write(ybuf.at[0], ys_hbm, t, ysem.at[0]).wait()
                return carry

            lax.fori_loop(n_live, N_ETILES, zero_start, 0)
            lax.fori_loop(n_live, N_ETILES, zero_wait, 0)


def _experts(tile_first, n_tiles, tile_half, src_unit, xp, w_gate_up, b_gate_up, w_down, b_down, layer,
             ys_prev):
    n_scalars, n_fixed_inputs = 4, 5
    grid_spec = pltpu.PrefetchScalarGridSpec(
        num_scalar_prefetch=4,
        grid=(N_EXPERTS,),
        in_specs=[
            pl.BlockSpec(memory_space=pl.ANY),
            pl.BlockSpec(memory_space=pl.ANY),
            pl.BlockSpec((None, None, 1, 2 * D_FF), lambda e, tf, nt, th, su: (layer, e, 0, 0)),
            pl.BlockSpec(memory_space=pl.ANY),
            pl.BlockSpec((None, None, 1, D_MODEL), lambda e, tf, nt, th, su: (layer, e, 0, 0)),
        ] + [pl.BlockSpec(memory_space=pl.ANY)] * len(ys_prev),
        out_specs=pl.BlockSpec(memory_space=pl.ANY),
        scratch_shapes=[
            pltpu.VMEM((2, D_MODEL, 2 * D_FF), _F32),
            pltpu.VMEM((2, D_FF, D_MODEL), _F32),
            pltpu.VMEM((D_MODEL, 2 * D_FF), _BF16),
            pltpu.VMEM((D_FF, D_MODEL), _BF16),
            pltpu.VMEM((X_BUFS, TE, D_MODEL), _BF16),
            pltpu.VMEM((Y_BUFS, TE, D_MODEL), _BF16),
            pltpu.SemaphoreType.DMA((2,)),
            pltpu.SemaphoreType.DMA((X_BUFS,)),
            pltpu.SemaphoreType.DMA((Y_BUFS,)),
        ],
    )
    return pl.pallas_call(
        functools.partial(_expert_kernel, layer, len(ys_prev) > 0),
        grid_spec=grid_spec,
        out_shape=jax.ShapeDtypeStruct((N_ETILES * TE, D_MODEL), _BF16),
        input_output_aliases={n_scalars + n_fixed_inputs + j: 0 for j in range(len(ys_prev))},
        compiler_params=pltpu.CompilerParams(
            dimension_semantics=("arbitrary",), vmem_limit_bytes=VMEM_LIMIT),
        name="experts",
    )(tile_first, n_tiles, tile_half, src_unit, xp, w_gate_up,
      b_gate_up.reshape(DEPTH, N_EXPERTS, 1, 2 * D_FF), w_down,
      b_down.reshape(DEPTH, N_EXPERTS, 1, D_MODEL), *ys_prev)


def _combine_kernel(split_out, back_ref, ys_hbm, x_ref, mod_ref, qw_ref, g_ref, *rest):
    outs, ybuf, sem = rest[:-2], rest[-2], rest[-1]
    b = pl.program_id(0)

    def gather(block):
        _unit_gather(ys_hbm, back_ref, block * BLOCK_UNITS, BLOCK_UNITS,
                     ybuf.at[block % COMBINE_BUFS], sem.at[block % COMBINE_BUFS])

    def gather_wait(block):
        _unit_gather_wait(ys_hbm, BLOCK_UNITS, ybuf.at[block % COMBINE_BUFS],
                          sem.at[block % COMBINE_BUFS])

    @pl.when(b == 0)
    def _():
        for ahead in range(COMBINE_BUFS - 1):
            gather(ahead)

    gather(b + COMBINE_BUFS - 1)
    gather_wait(b)
    moe = jnp.dot(qw_ref[...], ybuf[b % COMBINE_BUFS], preferred_element_type=_F32)

    @pl.when(b == N_BLOCKS - 1)
    def _():
        for ahead in range(1, COMBINE_BUFS):
            gather_wait(b + ahead)

    gt2 = mod_ref[:, 5 * D_MODEL:6 * D_MODEL]
    res = x_ref[...] + gt2 * _rms(moe, g_ref[...])
    if split_out:
        @pl.when(b < N_CTX // TB)
        def _():
            outs[0][...] = res

        @pl.when(b >= N_CTX // TB)
        def _():
            outs[1][...] = res
    else:
        outs[0][...] = res


def _combine(back_unit, ys, x1, mod3, qw, g_post, layer, split_out):
    n_ctx_blocks = N_CTX // TB
    if split_out:
        out_specs = [
            pl.BlockSpec((TB, D_MODEL), lambda b, bu: (jnp.minimum(b, n_ctx_blocks - 1), 0)),
            pl.BlockSpec((TB, D_MODEL), lambda b, bu: (jnp.maximum(b - n_ctx_blocks, 0), 0)),
        ]
        out_shape = [jax.ShapeDtypeStruct((N_CTX, D_MODEL), _F32),
                     jax.ShapeDtypeStruct((N_LAT, D_MODEL), _F32)]
    else:
        out_specs = [pl.BlockSpec((TB, D_MODEL), lambda b, bu: (b, 0))]
        out_shape = [jax.ShapeDtypeStruct((N_TOK, D_MODEL), _F32)]
    grid_spec = pltpu.PrefetchScalarGridSpec(
        num_scalar_prefetch=1,
        grid=(N_BLOCKS,),
        in_specs=[
            pl.BlockSpec(memory_space=pl.ANY),
            pl.BlockSpec((TB, D_MODEL), lambda b, bu: (b, 0)),
            pl.BlockSpec((None, 1, 6 * D_MODEL),
                         lambda b, bu: (layer * COND_ROWS + _cond_index(b, TB), 0, 0)),
            pl.BlockSpec((TB, SLOTS), lambda b, bu: (b, 0)),
            pl.BlockSpec((1, D_MODEL), lambda b, bu: (0, 0)),
        ],
        out_specs=out_specs,
        scratch_shapes=[
            pltpu.VMEM((COMBINE_BUFS, SLOTS, D_MODEL), _BF16),
            pltpu.SemaphoreType.DMA((COMBINE_BUFS,)),
        ],
    )
    return pl.pallas_call(
        functools.partial(_combine_kernel, split_out),
        grid_spec=grid_spec,
        out_shape=out_shape,
        compiler_params=pltpu.CompilerParams(
            dimension_semantics=("arbitrary",), vmem_limit_bytes=VMEM_LIMIT),
        name="combine_residual",
    )(back_unit, ys, x1, mod3, qw, g_post)


def _rope_tables():
    pos = jnp.arange(DEC_SEQ)
    row = (pos // GRID_W).astype(_F32)
    col = (pos % GRID_W).astype(_F32)
    inv = ROPE_THETA ** (-jnp.arange(ROPE_FREQS, dtype=_F32) / ROPE_FREQS)
    ang_r = row[:, None] * inv[None, :]
    ang_c = col[:, None] * inv[None, :]
    cos = jnp.concatenate([jnp.cos(ang_r)] * 2 + [jnp.cos(ang_c)] * 2, axis=-1)
    sin = jnp.concatenate([-jnp.sin(ang_r), jnp.sin(ang_r), -jnp.sin(ang_c), jnp.sin(ang_c)], axis=-1)
    reps = LANE // HEAD_DIM
    return jnp.tile(cos, (1, reps)), jnp.tile(sin, (1, reps))


def kernel(x_prompt, x_sample, cache_k, cache_v, c, c_ctx, w_ada, b_ada, g_pre_mix, g_post_mix,
           g_pre_ffn, g_post_ffn, w_in, attn_sink, w_attn_o, conv_w, conv_b, conv_ln_g, conv_ln_b,
           w_conv_o, w_out, w_router, b_router, w_gate_up, b_gate_up, w_down, b_down):
    cond =jnp.concatenate([c_ctx[None, :], c, jnp.zeros((COND_ROWS - N_COND, D_MODEL), _F32)], axis=0)
    mod = _modulation(cond, w_ada, b_ada)
    mod3 = mod.reshape(DEPTH * COND_ROWS, 1, 6 * D_MODEL)
    cos_t, sin_t = _rope_tables()
    ck = cache_k.reshape(DEC_BATCH, DEPTH, PAST_LEN, KV_DIM)
    cv = cache_v.reshape(DEC_BATCH, DEPTH, PAST_LEN, KV_DIM)
    w_router_pad = jnp.pad(w_router, ((0, 0), (0, 0), (0, LANE - N_EXPERTS)))
    wr_hi = w_router_pad.astype(_BF16)
    wr_lo = (w_router_pad - wr_hi.astype(_F32)).astype(_BF16)
    b_router_pad = jnp.pad(b_router, ((0, 0), (0, LANE - N_EXPERTS)), constant_values=-jnp.inf)

    kv_all = tuple(jnp.zeros((BATCH, DEPTH, SEQ, KV_DIM), _F32) for _ in range(2))
    x = None
    ys_prev = ()
    for l in range(DEPTH):
        row = lambda a: a[l][None, :]
        if l == 0:
            q, kv, u, sg, *kv_all, x = _inproj(
                x_prompt.reshape(N_CTX, D_MODEL), x_sample.reshape(N_LAT, D_MODEL), kv_all, mod3,
                row(g_pre_mix), w_in, cos_t, sin_t, l)
        else:
            q, kv, u, sg, *kv_all = _inproj(x, x, kv_all, mod3, row(g_pre_mix), w_in, cos_t, sin_t, l)
        att = _attention(q, kv, ck, cv, attn_sink[l], l)
        cvn = _conv_branch(u, conv_w[l], row(conv_b), row(conv_ln_g), row(conv_ln_b))
        x1, xp, qw, meta = _mix(
            x, mod3, att, cvn, sg, w_attn_o, w_conv_o, w_out, row(g_post_mix), row(g_pre_ffn),
            wr_hi[l], wr_lo[l], row(b_router_pad), l)
        tile_first, n_tiles, tile_half, src_unit, back_unit = _plan(meta)
        ys = _experts(tile_first, n_tiles, tile_half, src_unit, xp, w_gate_up, b_gate_up, w_down,
                      b_down, l, ys_prev)
        ys_prev = (ys,)
        outs = _combine(back_unit, ys, x1, mod3, qw, row(g_post_ffn), l, l == DEPTH - 1)
        x = outs[0]

    y_prompt = outs[0].reshape(BATCH, SEQ, D_MODEL)
    y_sample = outs[1].reshape(DEC_BATCH, DEC_SEQ, D_MODEL)
    new_k, new_v = (a.reshape(BATCH, DEPTH, SEQ, N_KV_HEADS, HEAD_DIM) for a in kv_all)
    return (y_prompt, y_sample, new_k, new_v)
```

```python
import functools

import jax
import jax.numpy as jnp
from jax import lax
from jax.experimental import pallas as pl
from jax.experimental.pallas import tpu as pltpu

D_MODEL = 1024
BATCH = 16
SEQ = 256
DEPTH = 2
DEC_BATCH = 2
DEC_SEQ = 2048
PAST_LEN = 256
GRID_W = 64
N_HEADS = 16
N_KV_HEADS = 4
GROUP = N_HEADS // N_KV_HEADS
HEAD_DIM = 64
Q_DIM = N_HEADS * HEAD_DIM
KV_DIM = N_KV_HEADS * HEAD_DIM
WINDOW = 128
ATTN_SCALE = HEAD_DIM ** -0.5
ROPE_THETA = 10000.0
ROPE_HALF = HEAD_DIM // 2
ROPE_FREQS = ROPE_HALF // 2
C_CONV = D_MODEL // 2
CONV_WIDTH = 31
CONV_PAD = (CONV_WIDTH - 1) // 2
N_EXPERTS = 32
TOP_K = 4
D_FF = D_MODEL
SWIGLU_LIMIT = 7.0
SWIGLU_ALPHA = 1.702
EPS = 1e-6
IN_COLS = Q_DIM + 2 * KV_DIM + 2 * C_CONV + 2 * D_MODEL

N_CTX = BATCH * SEQ
N_LAT = DEC_BATCH * DEC_SEQ
N_TOK = N_CTX + N_LAT
N_COND = 1 + DEC_BATCH

LANE = 128
SUBLANE = 8
COND_ROWS = SUBLANE
TM = 512
TQ = 256
TQ_SUB = 256
KWIN = TQ_SUB + 2 * WINDOW
assert GROUP == 4 and 2 * HEAD_DIM == LANE
LOG2E = 1.4426950408889634
QK_SCALE = ATTN_SCALE * LOG2E
HALO = 2 * SUBLANE
assert HALO >= CONV_PAD
CONV_SPAN = SEQ + ((HALO - CONV_PAD + CONV_WIDTH - 1) // SUBLANE) * SUBLANE
assert CONV_SPAN + SUBLANE - 1 <= SEQ + 2 * HALO

TB = 256
N_BLOCKS = N_TOK // TB
MIX_BLOCKS = 2
MIX_ROWS = MIX_BLOCKS * TB
UNIT = 2 * SUBLANE
SLOTS = -(-(TB * TOP_K + N_EXPERTS * (UNIT - 1)) // LANE) * LANE
BLOCK_UNITS = SLOTS // UNIT
TE = 256
TILE_UNITS = TE // UNIT
N_ETILES = (N_BLOCKS * BLOCK_UNITS) // TILE_UNITS + N_EXPERTS
COMBINE_BUFS = 3
X_BUFS = 3
Y_BUFS = 3
MAX_LIVE_TILES = (N_BLOCKS * ((TB * TOP_K + N_EXPERTS * (UNIT - 1)) // UNIT)) // TILE_UNITS + N_EXPERTS
assert MAX_LIVE_TILES + X_BUFS - 1 <= N_ETILES
W_CHUNKS = 2
W_ROWS = D_MODEL // W_CHUNKS
assert D_FF == D_MODEL
MOD_ROWS = 256
CONV_ROWS = 64
VMEM_LIMIT = 58 * 1024 * 1024

assert SLOTS >= TB * TOP_K + N_EXPERTS * (UNIT - 1) and SLOTS % UNIT == 0

_F32 = jnp.float32
_BF16 = jnp.bfloat16
_I32 = jnp.int32


def _rms(x, g):
    return x * lax.rsqrt(jnp.mean(x * x, axis=-1, keepdims=True) + EPS) * g


def _cond_index(i, tile):
    n_ctx_tiles = N_CTX // tile
    return jnp.where(i < n_ctx_tiles, 0, 1 + (i - n_ctx_tiles) // (DEC_SEQ // tile))


def _mod_kernel(cond_ref, w_ref, b_ref, out_ref):
    k = pl.program_id(1)
    cnd = cond_ref[...]
    s = cnd * jax.nn.sigmoid(cnd)
    w = w_ref[...]
    w_hi = w.astype(_BF16)
    w_lo = (w - w_hi.astype(_F32)).astype(_BF16)
    s_hi = s.astype(_BF16)
    s_lo = (s - s_hi.astype(_F32)).astype(_BF16)
    part = (jnp.dot(s_hi, w_hi, preferred_element_type=_F32)
            + jnp.dot(s_lo, w_hi, preferred_element_type=_F32)
            + jnp.dot(s_hi, w_lo, preferred_element_type=_F32))

    @pl.when(k == 0)
    def _():
        out_ref[...] = part + b_ref[...]

    @pl.when(k > 0)
    def _():
        out_ref[...] += part


def _modulation(cond, w_ada, b_ada):
    return pl.pallas_call(
        _mod_kernel,
        grid=(DEPTH, D_MODEL // MOD_ROWS),
        in_specs=[
            pl.BlockSpec((COND_ROWS, MOD_ROWS), lambda l, k: (0, k)),
            pl.BlockSpec((None, MOD_ROWS, 6 * D_MODEL), lambda l, k: (l, k, 0)),
            pl.BlockSpec((None, 1, 6 * D_MODEL), lambda l, k: (l, 0, 0)),
        ],
        out_specs=pl.BlockSpec((None, COND_ROWS, 6 * D_MODEL), lambda l, k: (l, 0, 0)),
        out_shape=jax.ShapeDtypeStruct((DEPTH, COND_ROWS, 6 * D_MODEL), _F32),
        compiler_params=pltpu.CompilerParams(
            dimension_semantics=("arbitrary", "arbitrary"), vmem_limit_bytes=VMEM_LIMIT),
        name="modulation",
    )(cond, w_ada, b_ada.reshape(DEPTH, 1, 6 * D_MODEL))


def _rope_chunk(x, cos, sin):
    lane = lax.broadcasted_iota(_I32, x.shape, 1)
    partner = jnp.where((lane & ROPE_FREQS) == 0,
                        pltpu.roll(x, LANE - ROPE_FREQS, 1), pltpu.roll(x, ROPE_FREQS, 1))
    return x * cos + partner * sin


def _inproj_kernel(first, *refs):
    n_in = 9
    xa_ref, xb_ref, mod_ref, g_ref, wf_ref, cos_ref, sin_ref = refs[:7]
    q_ref, kv_ref, u_ref, sg_ref = refs[n_in:n_in + 4]
    new_refs = refs[n_in + 4:n_in + 6]
    rest = refs[n_in + 6:]
    w_ref = rest[-1]
    i = pl.program_id(0)

    @pl.when(i == 0)
    def _():
        w_ref[...] = wf_ref[...].astype(_BF16)

    x = jnp.where(i < N_CTX // TM, xa_ref[...], xb_ref[...])
    if first:
        rest[0][...] = x
    sh = mod_ref[:, 0:D_MODEL]
    sc = mod_ref[:, D_MODEL:2 * D_MODEL]
    h = (_rms(x, g_ref[...]) * (1.0 + sc) + sh).astype(_BF16)

    c0 = 0
    q = jnp.dot(h, w_ref[:, c0:c0 + Q_DIM], preferred_element_type=_F32) * QK_SCALE
    c0 += Q_DIM
    kv = jnp.dot(h, w_ref[:, c0:c0 + 2 * KV_DIM], preferred_element_type=_F32)
    c0 += 2 * KV_DIM
    ua = jnp.dot(h, w_ref[:, c0:c0 + C_CONV], preferred_element_type=_F32)
    c0 += C_CONV
    ub = jnp.dot(h, w_ref[:, c0:c0 + C_CONV], preferred_element_type=_F32)
    c0 += C_CONV
    g = jnp.dot(h, w_ref[:, c0:c0 + 2 * D_MODEL], preferred_element_type=_F32)

    u_ref[...] = ua * jax.nn.sigmoid(ub)
    sg_ref[...] = jax.nn.sigmoid(g).astype(_BF16)

    is_latent = i >= N_CTX // TM

    @pl.when(jnp.logical_not(is_latent))
    def _():
        q_ref[...] = q.astype(_BF16)
        kv_ref[...] = kv
        for which, new_ref in enumerate(new_refs):
            cols = slice(which * KV_DIM, (which + 1) * KV_DIM)
            for r in range(TM // SEQ):
                rows = slice(r * SEQ, (r + 1) * SEQ)
                new_ref[r] = kv[rows, cols]

    @pl.when(is_latent)
    def _():
        cos = cos_ref[...]
        sin = sin_ref[...]
        for j in range(Q_DIM // LANE):
            sl = slice(j * LANE, (j + 1) * LANE)
            q_ref[:, sl] = _rope_chunk(q[:, sl], cos, sin).astype(_BF16)
        for j in range(KV_DIM // LANE):
            sl = slice(j * LANE, (j + 1) * LANE)
            kv_ref[:, sl] = _rope_chunk(kv[:, sl], cos, sin)
        kv_ref[:, KV_DIM:] = kv[:, KV_DIM:]


def _inproj(x_ctx, x_lat, kv_all, mod3, g_pre, w_in, cos_t, sin_t, layer):
    n_ctx_tiles = N_CTX // TM
    lat_tiles = DEC_SEQ // TM
    stack_x = layer == 0
    lat_off = 0 if stack_x else n_ctx_tiles

    def rope_map(i):
        return (jnp.where(i < n_ctx_tiles, 0, (i - n_ctx_tiles) % lat_tiles), 0)

    row_tile = lambda width: pl.BlockSpec((TM, width), lambda i: (i, 0))
    new_kv = pl.BlockSpec((TM // SEQ, None, SEQ, KV_DIM),
                          lambda i: (jnp.minimum(i, n_ctx_tiles - 1), layer, 0, 0))
    new_kv_shape = jax.ShapeDtypeStruct((BATCH, DEPTH, SEQ, KV_DIM), _F32)
    n_fixed_inputs = 7
    out_specs = [row_tile(Q_DIM), row_tile(2 * KV_DIM), row_tile(C_CONV), row_tile(2 * D_MODEL),
                 new_kv, new_kv]
    out_shape = [
        jax.ShapeDtypeStruct((N_TOK, Q_DIM), _BF16),
        jax.ShapeDtypeStruct((N_TOK, 2 * KV_DIM), _F32),
        jax.ShapeDtypeStruct((N_TOK, C_CONV), _F32),
        jax.ShapeDtypeStruct((N_TOK, 2 * D_MODEL), _BF16),
        new_kv_shape, new_kv_shape,
    ]
    if stack_x:
        out_specs.append(row_tile(D_MODEL))
        out_shape.append(jax.ShapeDtypeStruct((N_TOK, D_MODEL), _F32))
    return pl.pallas_call(
        functools.partial(_inproj_kernel, stack_x),
        grid=(N_TOK // TM,),
        in_specs=[
            pl.BlockSpec((TM, D_MODEL), lambda i: (jnp.minimum(i, n_ctx_tiles - 1), 0)),
            pl.BlockSpec((TM, D_MODEL), lambda i: (lat_off + jnp.maximum(i - n_ctx_tiles, 0), 0)),
            pl.BlockSpec((None, 1, 6 * D_MODEL),
                         lambda i: (layer * COND_ROWS + _cond_index(i, TM), 0, 0)),
            pl.BlockSpec((1, D_MODEL), lambda i: (0, 0)),
            pl.BlockSpec((None, D_MODEL, IN_COLS), lambda i: (layer, 0, 0),
                         pipeline_mode=pl.Buffered(1)),
            pl.BlockSpec((TM, LANE), rope_map),
            pl.BlockSpec((TM, LANE), rope_map),
        ] + [pl.BlockSpec(memory_space=pl.ANY)] * len(kv_all),
        out_specs=out_specs,
        out_shape=out_shape,
        input_output_aliases={n_fixed_inputs + j: 4 + j for j in range(len(kv_all))},
        scratch_shapes=[pltpu.VMEM((D_MODEL, IN_COLS), _BF16)],
        compiler_params=pltpu.CompilerParams(
            dimension_semantics=("arbitrary",), vmem_limit_bytes=VMEM_LIMIT),
        name="inproj",
    )(x_ctx, x_lat, mod3, g_pre, w_in, cos_t, sin_t, *kv_all)


def _pair_operands(k, v):
    zero = jnp.zeros_like(k)
    one = jnp.ones_like(v)
    ka = jnp.concatenate([k, zero], axis=1).astype(_BF16)
    kb = jnp.concatenate([zero, k], axis=1).astype(_BF16)
    va = jnp.concatenate([v, zero, one, zero], axis=1).astype(_BF16)
    vb = jnp.concatenate([zero, v, zero, one], axis=1).astype(_BF16)
    return ka, kb, va, vb


def _pair_attend(qq, operands, masks, sink_a, sink_b):
    def scores(which):
        out = []
        for ops, mask in zip(operands, masks):
            s = lax.dot_general(qq, ops[which], (((1,), (1,)), ((), ())),
                                preferred_element_type=_F32)
            out.append(s if mask is None else jnp.where(mask, s, -jnp.inf))
        return out

    acc = jnp.zeros((qq.shape[0], 2 * LANE), _F32)
    sink_terms = []
    for which, sink in ((0, sink_a), (1, sink_b)):
        ss = scores(which)
        m = sink
        for s in ss:
            m = jnp.maximum(m, jnp.max(s, axis=-1, keepdims=True))
        for s, ops in zip(ss, operands):
            acc = acc + jnp.dot(jnp.exp2(s - m).astype(_BF16), ops[2 + which],
                                preferred_element_type=_F32)
        sink_terms.append(jnp.exp2(sink - m))
    lane = lax.broadcasted_iota(_I32, (qq.shape[0], LANE), 1)
    sink_term = jnp.where(lane < HEAD_DIM, sink_terms[0], sink_terms[1])
    return acc[:, :LANE] / (acc[:, LANE:] + sink_term)


def _group_attend(q_ref, o_ref, hk, rows, operands, masks, sink_ref):
    n_rows = rows.stop - rows.start
    pairs = [slice((2 * hk + j) * LANE, (2 * hk + j + 1) * LANE) for j in range(GROUP // 2)]
    qq = jnp.concatenate([q_ref[rows, sl] for sl in pairs], axis=0)
    first = lax.broadcasted_iota(_I32, (qq.shape[0], 1), 0) < n_rows
    sink_a = jnp.where(first, sink_ref[GROUP * hk], sink_ref[GROUP * hk + 2]) * LOG2E
    sink_b = jnp.where(first, sink_ref[GROUP * hk + 1], sink_ref[GROUP * hk + 3]) * LOG2E
    out = _pair_attend(qq, operands, masks, sink_a, sink_b)
    for j, sl in enumerate(pairs):
        o_ref[rows, sl] = out[j * n_rows:(j + 1) * n_rows].astype(_BF16)


def _attn_kernel(sink_ref, q_ref, kv_own_ref, kv_seq_ref, ck_ref, cv_ref, o_ref):
    i = pl.program_id(0)
    n_ctx_steps = N_CTX // TQ

    @pl.when(i < n_ctx_steps)
    def _():
        for hk in range(N_KV_HEADS):
            ks = slice(hk * HEAD_DIM, (hk + 1) * HEAD_DIM)
            vs = slice(KV_DIM + hk * HEAD_DIM, KV_DIM + (hk + 1) * HEAD_DIM)
            own = _pair_operands(kv_own_ref[:, ks], kv_own_ref[:, vs])
            _group_attend(q_ref, o_ref, hk, slice(0, TQ), [own], [None], sink_ref)

    @pl.when(i >= n_ctx_steps)
    def _():
        qb = (i - n_ctx_steps) % (DEC_SEQ // TQ)
        stacked = (GROUP // 2) * TQ_SUB
        starts, masks = [], []
        for sb in range(TQ // TQ_SUB):
            q_start = qb * TQ + sb * TQ_SUB
            k_start = pl.multiple_of(jnp.clip(q_start - WINDOW, 0, DEC_SEQ - KWIN), WINDOW)
            qpos = q_start + lax.broadcasted_iota(_I32, (stacked, KWIN), 0) % TQ_SUB
            kpos = k_start + lax.broadcasted_iota(_I32, (stacked, KWIN), 1)
            starts.append(k_start)
            masks.append(jnp.abs(kpos - qpos) <= WINDOW)
        for hk in range(N_KV_HEADS):
            ks = slice(hk * HEAD_DIM, (hk + 1) * HEAD_DIM)
            vs = slice(KV_DIM + hk * HEAD_DIM, KV_DIM + (hk + 1) * HEAD_DIM)
            cached = _pair_operands(ck_ref[:, ks], cv_ref[:, ks])
            for sb in range(TQ // TQ_SUB):
                local = _pair_operands(kv_seq_ref[pl.ds(starts[sb], KWIN), ks],
                                       kv_seq_ref[pl.ds(starts[sb], KWIN), vs])
                _group_attend(q_ref, o_ref, hk, slice(sb * TQ_SUB, (sb + 1) * TQ_SUB),
                              [local, cached], [masks[sb], None], sink_ref)


def _attention(q, kv, cache_k, cache_v, sink, layer):
    n_ctx_steps = N_CTX // TQ
    nq = DEC_SEQ // TQ
    kv_off = N_CTX // DEC_SEQ

    def lat_batch(i):
        return jnp.maximum(i - n_ctx_steps, 0) // nq

    return pl.pallas_call(
        _attn_kernel,
        grid=(N_TOK // TQ,),
        in_specs=[
            pl.BlockSpec(memory_space=pltpu.SMEM),
            pl.BlockSpec((TQ, Q_DIM), lambda i: (i, 0)),
            pl.BlockSpec((TQ, 2 * KV_DIM), lambda i: (i, 0)),
            pl.BlockSpec((DEC_SEQ, 2 * KV_DIM), lambda i: (kv_off + lat_batch(i), 0)),
            pl.BlockSpec((None, None, PAST_LEN, KV_DIM), lambda i: (lat_batch(i), layer, 0, 0)),
            pl.BlockSpec((None, None, PAST_LEN, KV_DIM), lambda i: (lat_batch(i), layer, 0, 0)),
        ],
        out_specs=pl.BlockSpec((TQ, Q_DIM), lambda i: (i, 0)),
        out_shape=jax.ShapeDtypeStruct((N_TOK, Q_DIM), _BF16),
        compiler_params=pltpu.CompilerParams(
            dimension_semantics=("arbitrary",), vmem_limit_bytes=VMEM_LIMIT),
        name="attention",
    )(sink, q, kv, kv, cache_k, cache_v)


def _conv_kernel(prev_ref, cur_ref, next_ref, w_ref, b_ref, lg_ref, lb_ref, y_ref, pad_ref, sh_ref):
    i = pl.program_id(0)
    n_ctx_tiles = N_CTX // SEQ
    tiles_per_seq = jnp.where(i < n_ctx_tiles, 1, DEC_SEQ // SEQ)
    j = jnp.where(i < n_ctx_tiles, 0, (i - n_ctx_tiles) % (DEC_SEQ // SEQ))
    pad_ref[0:HALO, :] = jnp.where(j > 0, prev_ref[...], 0.0)
    pad_ref[HALO:HALO + SEQ, :] = cur_ref[...]
    pad_ref[HALO + SEQ:HALO + SEQ + HALO, :] = jnp.where(j < tiles_per_seq - 1, next_ref[...], 0.0)

    for r in range(SUBLANE):
        sh_ref[r] = pad_ref[r:r + CONV_SPAN, :]

    rows = CONV_ROWS
    for r0 in range(0, SEQ, rows):
        acc = jnp.zeros((rows, C_CONV), _F32) + b_ref[...]
        for t in range(CONV_WIDTH):
            off = HALO - CONV_PAD + t
            start = (off // SUBLANE) * SUBLANE + r0
            acc = acc + sh_ref[off % SUBLANE, start:start + rows, :] * w_ref[t:t + 1, :]
        mu = jnp.mean(acc, axis=-1, keepdims=True)
        d = acc - mu
        var = jnp.mean(d * d, axis=-1, keepdims=True)
        y = d * lax.rsqrt(var + EPS) * lg_ref[...] + lb_ref[...]
        y_ref[r0:r0 + rows, :] = (y * jax.nn.sigmoid(y)).astype(_BF16)


def _conv_branch(u, conv_w, conv_b, ln_g, ln_b):
    n_tiles = N_TOK // SEQ
    hb = SEQ // HALO
    last = N_TOK // HALO - 1
    return pl.pallas_call(
        _conv_kernel,
        grid=(n_tiles,),
        in_specs=[
            pl.BlockSpec((HALO, C_CONV), lambda i: (jnp.maximum(i * hb - 1, 0), 0)),
            pl.BlockSpec((SEQ, C_CONV), lambda i: (i, 0)),
            pl.BlockSpec((HALO, C_CONV), lambda i: (jnp.minimum((i + 1) * hb, last), 0)),
            pl.BlockSpec((CONV_WIDTH, C_CONV), lambda i: (0, 0)),
            pl.BlockSpec((1, C_CONV), lambda i: (0, 0)),
            pl.BlockSpec((1, C_CONV), lambda i: (0, 0)),
            pl.BlockSpec((1, C_CONV), lambda i: (0, 0)),
        ],
        out_specs=pl.BlockSpec((SEQ, C_CONV), lambda i: (i, 0)),
        out_shape=jax.ShapeDtypeStruct((N_TOK, C_CONV), _BF16),
        scratch_shapes=[pltpu.VMEM((SEQ + 2 * HALO, C_CONV), _F32),
                        pltpu.VMEM((SUBLANE, CONV_SPAN, C_CONV), _F32)],
        compiler_params=pltpu.CompilerParams(
            dimension_semantics=("arbitrary",), vmem_limit_bytes=VMEM_LIMIT),
        name="conv_branch",
    )(u, u, u, conv_w, conv_b, ln_g, ln_b)


def _mix_kernel(x_ref, mod_ref, att_ref, cv_ref, sg_ref, waf_ref, wcf_ref, wof_ref,
                gpost_ref, gffn_ref, wr_ref, wrlo_ref, br_ref,
                x1_ref, xp_ref, qw_ref, meta_ref, wa_ref, wc_ref, wo_ref):
    @pl.when(pl.program_id(0) == 0)
    def _():
        wa_ref[...] = waf_ref[...].astype(_BF16)
        wc_ref[...] = wcf_ref[...].astype(_BF16)
        wo_ref[...] = wof_ref[...].astype(_BF16)

    a = jnp.dot(att_ref[...], wa_ref[...], preferred_element_type=_F32)
    cv = jnp.dot(cv_ref[...], wc_ref[...], preferred_element_type=_F32)
    m = sg_ref[:, 0:D_MODEL].astype(_F32) * a + sg_ref[:, D_MODEL:].astype(_F32) * cv
    mix = jnp.dot(m.astype(_BF16), wo_ref[...], preferred_element_type=_F32)
    gt1 = mod_ref[:, 2 * D_MODEL:3 * D_MODEL]
    sh2 = mod_ref[:, 3 * D_MODEL:4 * D_MODEL]
    sc2 = mod_ref[:, 4 * D_MODEL:5 * D_MODEL]
    x1 = x_ref[...] + gt1 * _rms(mix, gpost_ref[...])
    x1_ref[...] = x1
    h2 = _rms(x1, gffn_ref[...]) * (1.0 + sc2) + sh2
    h2b = h2.astype(_BF16)

    h2_lo = (h2 - h2b.astype(_F32)).astype(_BF16)
    logits = (jnp.dot(h2b, wr_ref[...], preferred_element_type=_F32)
              + jnp.dot(h2_lo, wr_ref[...], preferred_element_type=_F32)
              + jnp.dot(h2b, wrlo_ref[...], preferred_element_type=_F32) + br_ref[...])

    for blk in range(MIX_BLOCKS):
        rows = slice(blk * TB, (blk + 1) * TB)
        qw, xp, meta = _route_block(logits[rows], h2b[rows])
        qw_ref[rows, :] = qw
        xp_ref[blk * SLOTS:(blk + 1) * SLOTS, :] = xp
        meta_ref[blk] = meta


def _route_block(logits, h2b):
    lane = lax.broadcasted_iota(_I32, (TB, LANE), 1).astype(_F32)
    member = jnp.zeros((TB, LANE), _F32)
    hots, exps = [], []
    top = None
    total = jnp.zeros((TB, 1), _F32)
    for k in range(TOP_K):
        mval = jnp.max(logits, axis=-1, keepdims=True)
        sel = jnp.min(jnp.where(logits == mval, lane, float(LANE)), axis=-1, keepdims=True)
        if top is None:
            top = mval
        e = jnp.exp(mval - top)
        total = total + e
        hot = lane == sel
        hots.append(hot)
        exps.append(e)
        member = member + jnp.where(hot, 1.0, 0.0)
        logits = jnp.where(hot, -jnp.inf, logits)

    r_i = lax.broadcasted_iota(_I32, (TB, TB), 0)
    c_i = lax.broadcasted_iota(_I32, (TB, TB), 1)
    lower = jnp.where(r_i > c_i, 1.0, 0.0).astype(_BF16)
    rank = jnp.dot(lower, member.astype(_BF16), preferred_element_type=_F32)
    count = jnp.sum(member, axis=0, keepdims=True)
    units = jnp.floor((count + float(UNIT - 1)) * (1.0 / UNIT))
    r_l = lax.broadcasted_iota(_I32, (LANE, LANE), 0)
    c_l = lax.broadcasted_iota(_I32, (LANE, LANE), 1)
    upper = jnp.where(r_l < c_l, 1.0, 0.0).astype(_BF16)
    unit_off = jnp.dot(jnp.broadcast_to(units, (SUBLANE, LANE)).astype(_BF16), upper,
                       preferred_element_type=_F32)[0:1, :]
    base = unit_off * float(UNIT) + rank

    slot_lane = lax.broadcasted_iota(_I32, (TB, SLOTS), 1).astype(_F32)
    qw = jnp.zeros((TB, SLOTS), _F32)
    hit = jnp.zeros((TB, SLOTS), _F32)
    for k in range(TOP_K):
        slot = jnp.sum(jnp.where(hots[k], base, 0.0), axis=-1, keepdims=True)
        here = slot_lane == slot
        qw = jnp.where(here, exps[k] / total, qw)
        hit = jnp.where(here, 1.0, hit)

    xp = lax.dot_general(hit.astype(_BF16), h2b, (((0,), (0,)), ((), ())),
                         preferred_element_type=_F32).astype(_BF16)

    sub = lax.broadcasted_iota(_I32, (SUBLANE, LANE), 0)
    meta = jnp.where(sub == 0, units, jnp.where(sub == 1, unit_off, 0.0))
    return qw.astype(_BF16), xp, meta.astype(_I32)


def _mix(x, mod3, att, cvn, sg, wa, wc, wo, g_post, g_ffn, wr_hi, wr_lo, b_router_pad, layer):
    full = lambda shape: pl.BlockSpec(shape, lambda i: (0,) * len(shape))
    layer_weight = lambda rows: pl.BlockSpec((None, rows, D_MODEL), lambda i: (layer, 0, 0),
                                             pipeline_mode=pl.Buffered(1))
    return pl.pallas_call(
        _mix_kernel,
        grid=(N_BLOCKS // MIX_BLOCKS,),
        in_specs=[
            pl.BlockSpec((MIX_ROWS, D_MODEL), lambda i: (i, 0)),
            pl.BlockSpec((None, 1, 6 * D_MODEL),
                         lambda i: (layer * COND_ROWS + _cond_index(i, MIX_ROWS), 0, 0)),
            pl.BlockSpec((MIX_ROWS, Q_DIM), lambda i: (i, 0)),
            pl.BlockSpec((MIX_ROWS, C_CONV), lambda i: (i, 0)),
            pl.BlockSpec((MIX_ROWS, 2 * D_MODEL), lambda i: (i, 0)),
            layer_weight(Q_DIM),
            layer_weight(C_CONV),
            layer_weight(D_MODEL),
            full((1, D_MODEL)),
            full((1, D_MODEL)),
            full((D_MODEL, LANE)),
            full((D_MODEL, LANE)),
            full((1, LANE)),
        ],
        out_specs=[
            pl.BlockSpec((MIX_ROWS, D_MODEL), lambda i: (i, 0)),
            pl.BlockSpec((MIX_BLOCKS * SLOTS, D_MODEL), lambda i: (i, 0)),
            pl.BlockSpec((MIX_ROWS, SLOTS), lambda i: (i, 0)),
            pl.BlockSpec((MIX_BLOCKS, SUBLANE, LANE), lambda i: (i, 0, 0)),
        ],
        out_shape=[
            jax.ShapeDtypeStruct((N_TOK, D_MODEL), _F32),
            jax.ShapeDtypeStruct((N_BLOCKS * SLOTS, D_MODEL), _BF16),
            jax.ShapeDtypeStruct((N_TOK, SLOTS), _BF16),
            jax.ShapeDtypeStruct((N_BLOCKS, SUBLANE, LANE), _I32),
        ],
        scratch_shapes=[pltpu.VMEM((Q_DIM, D_MODEL), _BF16), pltpu.VMEM((C_CONV, D_MODEL), _BF16),
                        pltpu.VMEM((D_MODEL, D_MODEL), _BF16)],
        compiler_params=pltpu.CompilerParams(
            dimension_semantics=("arbitrary",), vmem_limit_bytes=VMEM_LIMIT),
        name="mix_router",
    )(x, mod3, att, cvn, sg, wa, wc, wo, g_post, g_ffn, wr_hi, wr_lo, b_router_pad)


def _plan(meta):
    units = meta[:, 0, :N_EXPERTS]
    seg_off = meta[:, 1, :N_EXPERTS]
    tiles = (jnp.sum(units, axis=0) + TILE_UNITS - 1) // TILE_UNITS
    tile_end = jnp.cumsum(tiles)
    n_tiles = tile_end[-1]
    region = (tile_end - tiles) * TILE_UNITS
    dst = region[None, :] + jnp.cumsum(units, axis=0) - units
    src = jnp.arange(N_BLOCKS, dtype=_I32)[:, None] * BLOCK_UNITS + seg_off

    tile_first = jnp.concatenate([jnp.zeros((1,), _I32), tile_end.astype(_I32)])
    last_units = jnp.sum(units, axis=0) - (tiles - 1) * TILE_UNITS
    half_expert = (tiles > 0) & (last_units <= TILE_UNITS // 2)
    tile_id = jnp.arange(N_ETILES, dtype=_I32)[:, None]
    tile_half = jnp.sum(((tile_id == tile_end[None, :] - 1) & half_expert[None, :]).astype(_I32), axis=1)

    dst_f, len_f, src_f = dst.reshape(1, -1), units.reshape(1, -1), src.reshape(1, -1)
    d = jnp.arange(N_ETILES * TILE_UNITS, dtype=_I32)[:, None]
    in_seg = (dst_f <= d) & (d < dst_f + len_f)
    src_unit = jnp.sum(jnp.where(in_seg, src_f + d - dst_f, 0), axis=1)

    u = jnp.arange(BLOCK_UNITS, dtype=_I32)[None, :, None]
    so, un = seg_off[:, None, :], units[:, None, :]
    in_blk = (so <= u) & (u < so + un)
    back_unit = jnp.sum(jnp.where(in_blk, dst[:, None, :] + u - so, 0), axis=2)
    back_unit = jnp.concatenate([back_unit.reshape(-1).astype(_I32),
                                 jnp.zeros(((COMBINE_BUFS - 1) * BLOCK_UNITS,), _I32)])
    return tile_first, n_tiles.reshape(1).astype(_I32), tile_half, src_unit.astype(_I32), back_unit


def _unit_gather(src_hbm, unit_ref, first, n_units, dst_buf, sem):
    for i in range(n_units):
        row = pl.multiple_of(unit_ref[first + i] * UNIT, UNIT)
        pltpu.make_async_copy(src_hbm.at[pl.ds(row, UNIT), :],
                              dst_buf.at[pl.ds(i * UNIT, UNIT), :], sem).start()


def _unit_gather_wait(src_hbm, n_units, dst_buf, sem):
    pltpu.make_async_copy(src_hbm.at[pl.ds(0, n_units * UNIT), :], dst_buf, sem).wait()


def _tile_write(ybuf_slot, ys_hbm, tile, sem):
    row = pl.multiple_of(tile * TE, TE)
    return pltpu.make_async_copy(ybuf_slot, ys_hbm.at[pl.ds(row, TE), :], sem)


def _weight_chunk(wgu_hbm, wd_hbm, layer, expert, chunk, wgu_f32, wd_f32, buf, sem):
    r = pl.multiple_of(chunk * W_ROWS, W_ROWS)
    return (pltpu.make_async_copy(wgu_hbm.at[layer, expert, pl.ds(r, W_ROWS), :],
                                  wgu_f32.at[buf, pl.ds(r, W_ROWS), :], sem),
            pltpu.make_async_copy(wd_hbm.at[layer, expert, pl.ds(r, W_ROWS), :],
                                  wd_f32.at[buf, pl.ds(r, W_ROWS), :], sem))


def _expert_kernel(layer, reuse_out, first_ref, nt_ref, half_ref, src_ref, xp_hbm, wgu_hbm, bgu_ref,
                   wd_hbm, bd_ref, *rest):
    (ys_hbm, wgu_f32, wd_f32, wgu_bf, wd_bf, xbuf, ybuf, wsem, xsem,
     ysem) = rest[1:] if reuse_out else rest
    e = pl.program_id(0)
    n_live = nt_ref[0]
    t_lo = first_ref[e]
    t_hi = first_ref[e + 1]
    buf = e % 2
    has_next = e + 1 < N_EXPERTS

    def x_gather(tile):
        _unit_gather(xp_hbm, src_ref, tile * TILE_UNITS, TILE_UNITS,
                     xbuf.at[tile % X_BUFS], xsem.at[tile % X_BUFS])

    def start_chunk(expert, chunk, into):
        for cp in _weight_chunk(wgu_hbm, wd_hbm, layer, expert, chunk, wgu_f32, wd_f32,
                                into, wsem.at[into]):
            cp.start()

    @pl.when(e == 0)
    def _():
        for ahead in range(X_BUFS - 1):
            x_gather(ahead)
        for c in range(W_CHUNKS):
            start_chunk(0, c, 0)

    pltpu.make_async_copy(wgu_hbm.at[layer, e], wgu_f32.at[buf], wsem.at[buf]).wait()
    pltpu.make_async_copy(wd_hbm.at[layer, e], wd_f32.at[buf], wsem.at[buf]).wait()

    @pl.when(t_hi > t_lo)
    def _():
        wgu_bf[...] = wgu_f32[buf].astype(_BF16)
        wd_bf[...] = wd_f32[buf].astype(_BF16)

    def tile_body(t, carry):
        slot = t % Y_BUFS

        x_gather(t + X_BUFS - 1)

        @pl.when(jnp.logical_and(has_next, t - t_lo < W_CHUNKS))
        def _():
            start_chunk(e + 1, t - t_lo, 1 - buf)

        _unit_gather_wait(xp_hbm, TILE_UNITS, xbuf.at[t % X_BUFS], xsem.at[t % X_BUFS])

        @pl.when(t >= Y_BUFS)
        def _():
            _tile_write(ybuf.at[slot], ys_hbm, t - Y_BUFS, ysem.at[slot]).wait()

        def ffn(rows):
            gu = jnp.dot(xbuf[t % X_BUFS, 0:rows, :], wgu_bf[...],
                         preferred_element_type=_F32) + bgu_ref[...]
            gate = jnp.minimum(gu[:, :D_FF], SWIGLU_LIMIT)
            lin = jnp.clip(gu[:, D_FF:], -SWIGLU_LIMIT, SWIGLU_LIMIT)
            act = gate * jax.nn.sigmoid(SWIGLU_ALPHA * gate) * (lin + 1.0)
            y = jnp.dot(act.astype(_BF16), wd_bf[...], preferred_element_type=_F32) + bd_ref[...]
            ybuf[slot, 0:rows, :] = y.astype(_BF16)

        @pl.when(half_ref[t] == 0)
        def _():
            ffn(TE)

        @pl.when(half_ref[t] != 0)
        def _():
            ffn(TE // 2)
            ybuf[slot, TE // 2:, :] = jnp.zeros((TE // 2, D_MODEL), _BF16)

        _tile_write(ybuf.at[slot], ys_hbm, t, ysem.at[slot]).start()
        return carry

    lax.fori_loop(t_lo, t_hi, tile_body, 0)

    for c in range(W_CHUNKS):
        @pl.when(jnp.logical_and(has_next, c >= t_hi - t_lo))
        def _():
            start_chunk(e + 1, c, 1 - buf)

    @pl.when(e == N_EXPERTS - 1)
    def _():
        for ahead in range(X_BUFS - 1):
            t = n_live + ahead
            _unit_gather_wait(xp_hbm, TILE_UNITS, xbuf.at[t % X_BUFS], xsem.at[t % X_BUFS])
        for back in range(Y_BUFS, 0, -1):
            @pl.when(n_live >= back)
            def _():
                t = n_live - back
                _tile_write(ybuf.at[t % Y_BUFS], ys_hbm, t, ysem.at[t % Y_BUFS]).wait()

        if not reuse_out:
            ybuf[0] = jnp.zeros((TE, D_MODEL), _BF16)

            def zero_start(t, carry):
                _tile_write(ybuf.at[0], ys_hbm, t, ysem.at[0]).start()
                return carry

            def zero_wait(t, carry):
                _tile_write(ybuf.at[0], ys_hbm, t, ysem.at[0]).wait()
                return carry

            lax.fori_loop(n_live, N_ETILES, zero_start, 0)
            lax.fori_loop(n_live, N_ETILES, zero_wait, 0)


def _experts(tile_first, n_tiles, tile_half, src_unit, xp, w_gate_up, b_gate_up, w_down, b_down, layer,
             ys_prev):
    n_scalars, n_fixed_inputs = 4, 5
    grid_spec = pltpu.PrefetchScalarGridSpec(
        num_scalar_prefetch=4,
        grid=(N_EXPERTS,),
        in_specs=[
            pl.BlockSpec(memory_space=pl.ANY),
            pl.BlockSpec(memory_space=pl.ANY),
            pl.BlockSpec((None, None, 1, 2 * D_FF), lambda e, tf, nt, th, su: (layer, e, 0, 0)),
            pl.BlockSpec(memory_space=pl.ANY),
            pl.BlockSpec((None, None, 1, D_MODEL), lambda e, tf, nt, th, su: (layer, e, 0, 0)),
        ] + [pl.BlockSpec(memory_space=pl.ANY)] * len(ys_prev),
        out_specs=pl.BlockSpec(memory_space=pl.ANY),
        scratch_shapes=[
            pltpu.VMEM((2, D_MODEL, 2 * D_FF), _F32),
            pltpu.VMEM((2, D_FF, D_MODEL), _F32),
            pltpu.VMEM((D_MODEL, 2 * D_FF), _BF16),
            pltpu.VMEM((D_FF, D_MODEL), _BF16),
            pltpu.VMEM((X_BUFS, TE, D_MODEL), _BF16),
            pltpu.VMEM((Y_BUFS, TE, D_MODEL), _BF16),
            pltpu.SemaphoreType.DMA((2,)),
            pltpu.SemaphoreType.DMA((X_BUFS,)),
            pltpu.SemaphoreType.DMA((Y_BUFS,)),
        ],
    )
    return pl.pallas_call(
        functools.partial(_expert_kernel, layer, len(ys_prev) > 0),
        grid_spec=grid_spec,
        out_shape=jax.ShapeDtypeStruct((N_ETILES * TE, D_MODEL), _BF16),
        input_output_aliases={n_scalars + n_fixed_inputs + j: 0 for j in range(len(ys_prev))},
        compiler_params=pltpu.CompilerParams(
            dimension_semantics=("arbitrary",), vmem_limit_bytes=VMEM_LIMIT),
        name="experts",
    )(tile_first, n_tiles, tile_half, src_unit, xp, w_gate_up,
      b_gate_up.reshape(DEPTH, N_EXPERTS, 1, 2 * D_FF), w_down,
      b_down.reshape(DEPTH, N_EXPERTS, 1, D_MODEL), *ys_prev)


def _combine_kernel(split_out, back_ref, ys_hbm, x_ref, mod_ref, qw_ref, g_ref, *rest):
    outs, ybuf, sem = rest[:-2], rest[-2], rest[-1]
    b = pl.program_id(0)

    def gather(block):
        _unit_gather(ys_hbm, back_ref, block * BLOCK_UNITS, BLOCK_UNITS,
                     ybuf.at[block % COMBINE_BUFS], sem.at[block % COMBINE_BUFS])

    def gather_wait(block):
        _unit_gather_wait(ys_hbm, BLOCK_UNITS, ybuf.at[block % COMBINE_BUFS],
                          sem.at[block % COMBINE_BUFS])

    @pl.when(b == 0)
    def _():
        for ahead in range(COMBINE_BUFS - 1):
            gather(ahead)

    gather(b + COMBINE_BUFS - 1)
    gather_wait(b)
    moe = jnp.dot(qw_ref[...], ybuf[b % COMBINE_BUFS], preferred_element_type=_F32)

    @pl.when(b == N_BLOCKS - 1)
    def _():
        for ahead in range(1, COMBINE_BUFS):
            gather_wait(b + ahead)

    gt2 = mod_ref[:, 5 * D_MODEL:6 * D_MODEL]
    res = x_ref[...] + gt2 * _rms(moe, g_ref[...])
    if split_out:
        @pl.when(b < N_CTX // TB)
        def _():
            outs[0][...] = res

        @pl.when(b >= N_CTX // TB)
        def _():
            outs[1][...] = res
    else:
        outs[0][...] = res


def _combine(back_unit, ys, x1, mod3, qw, g_post, layer, split_out):
    n_ctx_blocks = N_CTX // TB
    if split_out:
        out_specs = [
            pl.BlockSpec((TB, D_MODEL), lambda b, bu: (jnp.minimum(b, n_ctx_blocks - 1), 0)),
            pl.BlockSpec((TB, D_MODEL), lambda b, bu: (jnp.maximum(b - n_ctx_blocks, 0), 0)),
        ]
        out_shape = [jax.ShapeDtypeStruct((N_CTX, D_MODEL), _F32),
                     jax.ShapeDtypeStruct((N_LAT, D_MODEL), _F32)]
    else:
        out_specs = [pl.BlockSpec((TB, D_MODEL), lambda b, bu: (b, 0))]
        out_shape = [jax.ShapeDtypeStruct((N_TOK, D_MODEL), _F32)]
    grid_spec = pltpu.PrefetchScalarGridSpec(
        num_scalar_prefetch=1,
        grid=(N_BLOCKS,),
        in_specs=[
            pl.BlockSpec(memory_space=pl.ANY),
            pl.BlockSpec((TB, D_MODEL), lambda b, bu: (b, 0)),
            pl.BlockSpec((None, 1, 6 * D_MODEL),
                         lambda b, bu: (layer * COND_ROWS + _cond_index(b, TB), 0, 0)),
            pl.BlockSpec((TB, SLOTS), lambda b, bu: (b, 0)),
            pl.BlockSpec((1, D_MODEL), lambda b, bu: (0, 0)),
        ],
        out_specs=out_specs,
        scratch_shapes=[
            pltpu.VMEM((COMBINE_BUFS, SLOTS, D_MODEL), _BF16),
            pltpu.SemaphoreType.DMA((COMBINE_BUFS,)),
        ],
    )
    return pl.pallas_call(
        functools.partial(_combine_kernel, split_out),
        grid_spec=grid_spec,
        out_shape=out_shape,
        compiler_params=pltpu.CompilerParams(
            dimension_semantics=("arbitrary",), vmem_limit_bytes=VMEM_LIMIT),
        name="combine_residual",
    )(back_unit, ys, x1, mod3, qw, g_post)


def _rope_tables():
    pos = jnp.arange(DEC_SEQ)
    row = (pos // GRID_W).astype(_F32)
    col = (pos % GRID_W).astype(_F32)
    inv = ROPE_THETA ** (-jnp.arange(ROPE_FREQS, dtype=_F32) / ROPE_FREQS)
    ang_r = row[:, None] * inv[None, :]
    ang_c = col[:, None] * inv[None, :]
    cos = jnp.concatenate([jnp.cos(ang_r)] * 2 + [jnp.cos(ang_c)] * 2, axis=-1)
    sin = jnp.concatenate([-jnp.sin(ang_r), jnp.sin(ang_r), -jnp.sin(ang_c), jnp.sin(ang_c)], axis=-1)
    reps = LANE // HEAD_DIM
    return jnp.tile(cos, (1, reps)), jnp.tile(sin, (1, reps))


def kernel(x_prompt, x_sample, cache_k, cache_v, c, c_ctx, w_ada, b_ada, g_pre_mix, g_post_mix,
           g_pre_ffn, g_post_ffn, w_in, attn_sink, w_attn_o, conv_w, conv_b, conv_ln_g, conv_ln_b,
           w_conv_o, w_out, w_router, b_router, w_gate_up, b_gate_up, w_down, b_down):
    cond =jnp.concatenate([c_ctx[None, :], c, jnp.zeros((COND_ROWS - N_COND, D_MODEL), _F32)], axis=0)
    mod = _modulation(cond, w_ada, b_ada)
    mod3 = mod.reshape(DEPTH * COND_ROWS, 1, 6 * D_MODEL)
    cos_t, sin_t = _rope_tables()
    ck = cache_k.reshape(DEC_BATCH, DEPTH, PAST_LEN, KV_DIM)
    cv = cache_v.reshape(DEC_BATCH, DEPTH, PAST_LEN, KV_DIM)
    w_router_pad = jnp.pad(w_router, ((0, 0), (0, 0), (0, LANE - N_EXPERTS)))
    wr_hi = w_router_pad.astype(_BF16)
    wr_lo = (w_router_pad - wr_hi.astype(_F32)).astype(_BF16)
    b_router_pad = jnp.pad(b_router, ((0, 0), (0, LANE - N_EXPERTS)), constant_values=-jnp.inf)

    kv_all = tuple(jnp.zeros((BATCH, DEPTH, SEQ, KV_DIM), _F32) for _ in range(2))
    x = None
    ys_prev = ()
    for l in range(DEPTH):
        row = lambda a: a[l][None, :]
        if l == 0:
            q, kv, u, sg, *kv_all, x = _inproj(
                x_prompt.reshape(N_CTX, D_MODEL), x_sample.reshape(N_LAT, D_MODEL), kv_all, mod3,
                row(g_pre_mix), w_in, cos_t, sin_t, l)
        else:
            q, kv, u, sg, *kv_all = _inproj(x, x, kv_all, mod3, row(g_pre_mix), w_in, cos_t, sin_t, l)
        att = _attention(q, kv, ck, cv, attn_sink[l], l)
        cvn = _conv_branch(u, conv_w[l], row(conv_b), row(conv_ln_g), row(conv_ln_b))
        x1, xp, qw, meta = _mix(
            x, mod3, att, cvn, sg, w_attn_o, w_conv_o, w_out, row(g_post_mix), row(g_pre_ffn),
            wr_hi[l], wr_lo[l], row(b_router_pad), l)
        tile_first, n_tiles, tile_half, src_unit, back_unit = _plan(meta)
        ys = _experts(tile_first, n_tiles, tile_half, src_unit, xp, w_gate_up, b_gate_up, w_down,
                      b_down, l, ys_prev)
        ys_prev = (ys,)
        outs = _combine(back_unit, ys, x1, mod3, qw, row(g_post_ffn), l, l == DEPTH - 1)
        x = outs[0]

    y_prompt = outs[0].reshape(BATCH, SEQ, D_MODEL)
    y_sample = outs[1].reshape(DEC_BATCH, DEC_SEQ, D_MODEL)
    new_k, new_v = (a.reshape(BATCH, DEPTH, SEQ, N_KV_HEADS, HEAD_DIM) for a in kv_all)
    return (y_prompt, y_sample, new_k, new_v)
```
